```python
import math
import jax, jax.numpy as jnp
from jax import lax
import numpy as np

D_MODEL = 1024
BATCH = 32
SEQ = 2048
DEPTH = 1
DEC_BATCH = 32
DEC_SEQ = 32
PAST_LEN = 1024

CHUNK = 64
Q_BLOCK = 128
N_HEADS_A = 8
HEAD_DIM_A = 64
V_DIM_A = 2 * HEAD_DIM_A
N_HEADS_H = 8
KEY_DIM_H = 128
VAL_DIM_H = D_MODEL // N_HEADS_H
GLA_BLOCK = 16
N_EXPERTS = 256
TOP_K = 8
N_GROUPS = 8
TOPK_GROUPS = 4
D_EXPERT = D_MODEL // 4
D_SHARED = D_EXPERT
ROUTED_SCALE = 2.5
MOE_BLOCK = 128
EPS = 1e-6
QA = N_HEADS_A * 2 * HEAD_DIM_A
VA = N_HEADS_A * V_DIM_A
FH = N_HEADS_H * KEY_DIM_H
IH = N_HEADS_H * VAL_DIM_H
IN_WIDTHS = (QA, QA, VA, FH, FH, IH, IH, D_MODEL, D_MODEL)
IN_WIDTH = 2 * QA + VA + 2 * FH + 2 * IH + 2 * D_MODEL

kernel_name = 'hybrid_diffattn_hgrn2_moe_stream_step'


def rms_norm(x, w):
    xf = x.astype(jnp.float32)
    y = xf * lax.rsqrt(jnp.mean(xf * xf, axis=-1, keepdims=True) + EPS)
    return (y * w.astype(jnp.float32)).astype(x.dtype)


def lambda_init(layer):
    return 0.8 - 0.6 * math.exp(-0.3 * layer)


def diff_attn_block(q, k, v, q_pos, k_pos, lam):
    s = jnp.einsum('bqhmd,bkhmd->bhmqk', q.astype(jnp.float32), k.astype(jnp.float32)) * (HEAD_DIM_A ** -0.5)
    allowed = (k_pos[None, :] // CHUNK) <= (q_pos[:, None] // CHUNK)
    s = jnp.where(allowed, s, -jnp.inf)
    p = jax.nn.softmax(s, axis=-1)
    p = p[:, :, 0] - lam * p[:, :, 1]
    return jnp.einsum('bhqk,bkhe->bqhe', p, v.astype(jnp.float32))


def diff_attn_prompt(q, k, v, lam):
    B, L = q.shape[:2]
    nb = L // Q_BLOCK
    qb = q.reshape((B, nb, Q_BLOCK) + q.shape[2:]).swapaxes(0, 1)
    pos = jnp.arange(L, dtype=jnp.int32)
    out = lax.map(lambda a: diff_attn_block(a[0], k, v, a[1], pos, lam), (qb, pos.reshape(nb, Q_BLOCK)))
    return out.swapaxes(0, 1).reshape(B, L, N_HEADS_A, V_DIM_A)


def gla_chunked(q, k, v, log_f, S0):
    B, L, H, Dk = q.shape
    Dv = v.shape[-1]
    pad = (-L) % GLA_BLOCK
    widths = ((0, 0), (0, pad), (0, 0), (0, 0))
    q, k, v, log_f = (jnp.pad(a, widths) for a in (q, k, v, log_f))
    n = (L + pad) // GLA_BLOCK

    def blocks(a):
        return a.reshape(B, n, GLA_BLOCK, H, a.shape[-1]).transpose(1, 0, 3, 2, 4)

    causal = jnp.tril(jnp.ones((GLA_BLOCK, GLA_BLOCK), dtype=bool))

    def step(S, blk):
        qc, kc, vc, gc = blk
        b = jnp.cumsum(gc, axis=2)
        inter = jnp.einsum('bhtd,bhde->bhte', qc * jnp.exp(b), S)
        rel = jnp.where(causal[:, :, None], b[:, :, :, None, :] - b[:, :, None, :, :], -jnp.inf)
        att = jnp.einsum('bhtd,bhsd,bhtsd->bhts', qc, kc, jnp.exp(rel))
        o = inter + jnp.einsum('bhts,bhse->bhte', att, vc)
        bl = b[:, :, -1:, :]
        S = jnp.exp(bl[:, :, 0, :, None]) * S + jnp.einsum('bhsd,bhse->bhde', kc * jnp.exp(bl - b), vc)
        return S, o

    S, o = lax.scan(step, S0, (blocks(q), blocks(k), blocks(v), blocks(log_f)))
    o = o.transpose(1, 0, 3, 2, 4).reshape(B, n * GLA_BLOCK, H, Dv)[:, :L]
    return o, S


def hgrn2(qh, fh, ih, gh, lb, norm_w, S0):
    B, L = qh.shape[:2]
    shp_k = (B, L, N_HEADS_H, KEY_DIM_H)
    shp_v = (B, L, N_HEADS_H, VAL_DIM_H)
    q = (jax.nn.silu(qh.astype(jnp.float32)) * (KEY_DIM_H ** -0.5)).reshape(shp_k)
    f = lb.astype(jnp.float32) + (1.0 - lb.astype(jnp.float32)) * jax.nn.sigmoid(fh.astype(jnp.float32))
    log_f = jnp.log(f).reshape(shp_k)
    k = (1.0 - f).reshape(shp_k)
    v = ih.astype(jnp.float32).reshape(shp_v)
    o, S = gla_chunked(q, k, v, log_f, S0)
    o = rms_norm(o, norm_w) * jax.nn.silu(gh.astype(jnp.float32).reshape(shp_v))
    return o.reshape(B, L, IH), S


def moe(h, w_router, b_router, w_eg, w_eu, w_ed, w_sg, w_su, w_sd):
    T, D = h.shape
    scores = jax.nn.sigmoid(h.astype(jnp.float32) @ w_router.astype(jnp.float32))
    biased = scores + b_router.astype(jnp.float32)
    grp = biased.reshape(T, N_GROUPS, N_EXPERTS // N_GROUPS)
    grp_score = lax.top_k(grp, 2)[0].sum(-1)
    _, gidx = lax.top_k(grp_score, TOPK_GROUPS)
    gmask = jax.nn.one_hot(gidx, N_GROUPS).sum(1) > 0
    masked = jnp.where(gmask[:, :, None], grp, -jnp.inf).reshape(T, N_EXPERTS)
    _, eidx = lax.top_k(masked, TOP_K)
    wts = jnp.take_along_axis(scores, eidx, axis=1)
    wts = wts / wts.sum(-1, keepdims=True) * ROUTED_SCALE
    A = T * TOP_K
    flat_e = eidx.reshape(A)
    flat_t = jnp.repeat(jnp.arange(T, dtype=jnp.int32), TOP_K)
    flat_w = wts.reshape(A)
    order = jnp.argsort(flat_e)
    se = flat_e[order]
    counts = jnp.bincount(flat_e, length=N_EXPERTS)
    start = jnp.cumsum(counts) - counts
    pcounts = (counts + MOE_BLOCK - 1) // MOE_BLOCK * MOE_BLOCK
    pend = jnp.cumsum(pcounts)
    pstart = pend - pcounts
    dest = pstart[se] + jnp.arange(A) - start[se]
    nb = -(-(A + N_EXPERTS * (MOE_BLOCK - 1)) // MOE_BLOCK)
    rows = nb * MOE_BLOCK
    row_tok = jnp.full((rows,), T, dtype=jnp.int32).at[dest].set(flat_t[order])
    row_w = jnp.zeros((rows,), jnp.float32).at[dest].set(flat_w[order])
    blk_e = jnp.minimum(jnp.searchsorted(pend, jnp.arange(nb) * MOE_BLOCK, side='right'), N_EXPERTS - 1)
    h_pad = jnp.concatenate([h, jnp.zeros((1, D), h.dtype)], axis=0)

    def step(y, blk):
        toks, w, e = blk
        xb = h_pad[toks]
        out = (jax.nn.silu(xb @ w_eg[e]) * (xb @ w_eu[e])) @ w_ed[e]
        return y.at[toks].add(out * w[:, None]), None

    y, _ = lax.scan(step, jnp.zeros((T + 1, D), jnp.float32),
                    (row_tok.reshape(nb, MOE_BLOCK), row_w.reshape(nb, MOE_BLOCK), blk_e))
    shared = (jax.nn.silu(h @ w_sg) * (h @ w_su)) @ w_sd
    return (y[:T] + shared).astype(h.dtype)


def block(x, c, past_k, past_v, S0, lb, lam_init, w_ada, b_ada, norm1_w, norm2_w, w_in, q_norm_w, k_norm_w,
          lq1, lk1, lq2, lk2, subln_w, hgrn_norm_w, w_proj_a, w_proj_b, w_out, w_router, b_router,
          w_eg, w_eu, w_ed, w_sg, w_su, w_sd):
    B, L, D = x.shape
    shift1, scale1, gate1, shift2, scale2, gate2 = jnp.split(jax.nn.silu(c) @ w_ada + b_ada, 6, axis=-1)
    h = rms_norm(x, norm1_w) * (1.0 + scale1[:, None]) + shift1[:, None]
    offs = [int(o) for o in np.cumsum(IN_WIDTHS)[:-1]]
    qa, ka, va, qh, fh, ih, gh, ga, gb = jnp.split(h @ w_in, offs, axis=-1)
    q = rms_norm(qa.reshape(B, L, N_HEADS_A, 2, HEAD_DIM_A), q_norm_w)
    k = rms_norm(ka.reshape(B, L, N_HEADS_A, 2, HEAD_DIM_A), k_norm_w)
    v = va.reshape(B, L, N_HEADS_A, V_DIM_A)
    f32 = jnp.float32
    lam = (jnp.exp(jnp.sum(lq1.astype(f32) * lk1.astype(f32)))
           - jnp.exp(jnp.sum(lq2.astype(f32) * lk2.astype(f32))) + lam_init)
    if past_k is None:
        o_a = diff_attn_prompt(q, k, v, lam)
        S0 = jnp.zeros((B, N_HEADS_H, KEY_DIM_H, VAL_DIM_H), f32)
    else:
        P = past_k.shape[1]
        kk = jnp.concatenate([past_k.astype(k.dtype), k], axis=1)
        vv = jnp.concatenate([past_v.astype(v.dtype), v], axis=1)
        o_a = diff_attn_block(q, kk, vv, P + jnp.arange(L, dtype=jnp.int32), jnp.arange(P + L, dtype=jnp.int32), lam)
        S0 = S0.astype(f32)
    o_a = (rms_norm(o_a, subln_w) * (1.0 - lam_init)).reshape(B, L, VA).astype(x.dtype)
    o_b, S = hgrn2(qh, fh, ih, gh, lb, hgrn_norm_w, S0)
    o_b = o_b.astype(x.dtype)
    u = jax.nn.sigmoid(ga) * (o_a @ w_proj_a) + jax.nn.sigmoid(gb) * (o_b @ w_proj_b)
    x = x + gate1[:, None] * (u @ w_out)
    h2 = rms_norm(x, norm2_w) * (1.0 + scale2[:, None]) + shift2[:, None]
    y = moe(h2.reshape(B * L, D), w_router, b_router, w_eg, w_eu, w_ed, w_sg, w_su, w_sd).reshape(B, L, D)
    x = x + gate2[:, None] * y
    return x, k, v, S.astype(x.dtype)


def setup_inputs(seed: int = 0) -> dict:
    key = jax.random.key(seed)
    ks = jax.random.split(key, 32)
    D = D_MODEL

    def nrm(k, shape, s):
        return jax.random.normal(k, shape, jnp.float32) * s

    return {
        'x_prompt': nrm(ks[0], (BATCH, SEQ, D), 1.0),
        'x_sample': nrm(ks[1], (DEC_BATCH, DEC_SEQ, D), 1.0),
        'cache_attn_k': nrm(ks[2], (DEPTH, DEC_BATCH, PAST_LEN, N_HEADS_A, 2, HEAD_DIM_A), 1.0),
        'cache_attn_v': nrm(ks[3], (DEPTH, DEC_BATCH, PAST_LEN, N_HEADS_A, V_DIM_A), 1.0),
        'state_hgrn': nrm(ks[4], (DEPTH, DEC_BATCH, N_HEADS_H, KEY_DIM_H, VAL_DIM_H), 0.5),
        'c_prompt': nrm(ks[5], (BATCH, D), 1.0),
        'c_sample': nrm(ks[6], (DEC_BATCH, D), 1.0),
        'w_ada': nrm(ks[7], (DEPTH, D, 6 * D), 0.5 * D ** -0.5),
        'b_ada': nrm(ks[8], (DEPTH, 6 * D), 0.01),
        'norm1_w': 1.0 + nrm(ks[9], (DEPTH, D), 0.02),
        'norm2_w': 1.0 + nrm(ks[10], (DEPTH, D), 0.02),
        'w_in': nrm(ks[11], (DEPTH, D, IN_WIDTH), D ** -0.5),
        'q_norm_w': 1.0 + nrm(ks[12], (DEPTH, HEAD_DIM_A), 0.02),
        'k_norm_w': 1.0 + nrm(ks[13], (DEPTH, HEAD_DIM_A), 0.02),
        'lambda_q1': nrm(ks[14], (DEPTH, HEAD_DIM_A), 0.1),
        'lambda_k1': nrm(ks[15], (DEPTH, HEAD_DIM_A), 0.1),
        'lambda_q2': nrm(ks[16], (DEPTH, HEAD_DIM_A), 0.1),
        'lambda_k2': nrm(ks[17], (DEPTH, HEAD_DIM_A), 0.1),
        'subln_w': 1.0 + nrm(ks[18], (DEPTH, V_DIM_A), 0.02),
        'hgrn_lb': nrm(ks[19], (DEPTH + 1, FH), 0.1),
        'hgrn_norm_w': 1.0 + nrm(ks[20], (DEPTH, VAL_DIM_H), 0.02),
        'w_proj_a': nrm(ks[21], (DEPTH, VA, D), VA ** -0.5),
        'w_proj_b': nrm(ks[22], (DEPTH, IH, D), IH ** -0.5),
        'w_out': nrm(ks[23], (DEPTH, D, D), D ** -0.5),
        'w_router': nrm(ks[24], (DEPTH, D, N_EXPERTS), D ** -0.5),
        'b_router': nrm(ks[25], (DEPTH, N_EXPERTS), 0.01),
        'w_exp_gate': nrm(ks[26], (DEPTH, N_EXPERTS, D, D_EXPERT), D ** -0.5),
        'w_exp_up': nrm(ks[27], (DEPTH, N_EXPERTS, D, D_EXPERT), D ** -0.5),
        'w_exp_down': nrm(ks[28], (DEPTH, N_EXPERTS, D_EXPERT, D), D_EXPERT ** -0.5),
        'w_sh_gate': nrm(ks[29], (DEPTH, D, D_SHARED), D ** -0.5),
        'w_sh_up': nrm(ks[30], (DEPTH, D, D_SHARED), D ** -0.5),
        'w_sh_down': nrm(ks[31], (DEPTH, D_SHARED, D), D_SHARED ** -0.5),
    }


def reference(x_prompt, x_sample, cache_attn_k, cache_attn_v, state_hgrn, c_prompt, c_sample,
              w_ada, b_ada, norm1_w, norm2_w, w_in, q_norm_w, k_norm_w,
              lambda_q1, lambda_k1, lambda_q2, lambda_k2, subln_w, hgrn_lb, hgrn_norm_w,
              w_proj_a, w_proj_b, w_out, w_router, b_router,
              w_exp_gate, w_exp_up, w_exp_down, w_sh_gate, w_sh_up, w_sh_down):
    lbs = jnp.cumsum(jax.nn.softmax(hgrn_lb.astype(jnp.float32), axis=0), axis=0)
    yp, ys = x_prompt, x_sample
    kp, vp, sp, ks_, vs_, ss_ = [], [], [], [], [], []
    for l in range(DEPTH):
        li = lambda_init(l)
        lw = (w_ada[l], b_ada[l], norm1_w[l], norm2_w[l], w_in[l], q_norm_w[l], k_norm_w[l],
              lambda_q1[l], lambda_k1[l], lambda_q2[l], lambda_k2[l], subln_w[l], hgrn_norm_w[l],
              w_proj_a[l], w_proj_b[l], w_out[l], w_router[l], b_router[l],
              w_exp_gate[l], w_exp_up[l], w_exp_down[l], w_sh_gate[l], w_sh_up[l], w_sh_down[l])
        yp, k1, v1, s1 = block(yp, c_prompt, None, None, None, lbs[l], li, *lw)
        ys, k2, v2, s2 = block(ys, c_sample, cache_attn_k[l], cache_attn_v[l], state_hgrn[l], lbs[l], li, *lw)
        kp.append(k1)
        vp.append(v1)
        sp.append(s1)
        ks_.append(k2)
        vs_.append(v2)
        ss_.append(s2)
    return (yp, ys, jnp.stack(kp), jnp.stack(vp), jnp.stack(sp), jnp.stack(ks_), jnp.stack(vs_), jnp.stack(ss_))
```

```python
import functools
import math

import jax
import jax.numpy as jnp
from jax import lax
from jax.experimental import pallas as pl
from jax.experimental.pallas import tpu as pltpu

F32 = jnp.float32
BF16 = jnp.bfloat16
U32 = jnp.uint32
I32 = jnp.int32

D_MODEL = 1024
N_HEADS = 8
HEAD_W = 128
HEAD_DIM_A = 64
CHUNK = 64
N_EXPERTS = 256
TOP_K = 8
GROUP_W = 32
TOPK_GROUPS = 4
D_EXPERT = 256
ROUTED_SCALE = 2.5
EPS = 1e-6
LAMBDA_INIT = 0.8 - 0.6 * math.exp(-0.3 * 0)
N_IN_GROUPS = 9
LANES = 128
NEG = -1e30

GLA_CHUNK = 32
ROW_BLOCK = 256
VMEM_LIMIT = 48 * 1024 * 1024


def _sigmoid(x):
    return 1.0 / (1.0 + jnp.exp(-x))


def _silu(x):
    return x * _sigmoid(x)


def _cparams(sem):
    return pltpu.CompilerParams(dimension_semantics=sem, vmem_limit_bytes=VMEM_LIMIT)


def _pack_halves(x):
    half = x.shape[1] // 2
    lo = pltpu.bitcast(x[:, :half].astype(BF16).astype(F32), U32)
    hi = pltpu.bitcast(x[:, half:].astype(BF16).astype(F32), U32)
    return (hi & jnp.uint32(0xFFFF0000)) | (lo >> 16)


def _unpack_halves(w):
    lo = pltpu.bitcast(w << 16, F32)
    hi = pltpu.bitcast(w & jnp.uint32(0xFFFF0000), F32)
    return jnp.concatenate([lo, hi], axis=1)


def _seg_allreduce(x, width, op):
    lane = lax.broadcasted_iota(I32, x.shape, 1)
    sh = 1
    while sh < width:
        up = pltpu.roll(x, sh, axis=1)
        dn = pltpu.roll(x, LANES - sh, axis=1)
        x = op(x, jnp.where((lane & sh) != 0, up, dn))
        sh *= 2
    return x


def _ada_kernel(c_ref, w_ref, b_ref, o_ref):
    c = _silu(c_ref[...])
    o_ref[...] = jnp.dot(c, w_ref[...], precision=lax.Precision.HIGHEST,
                         preferred_element_type=F32) + b_ref[...]


def _ada(c, w_ada, b_ada):
    n = c.shape[0]
    nj = w_ada.shape[1] // D_MODEL
    return pl.pallas_call(
        _ada_kernel,
        grid=(nj,),
        in_specs=[pl.BlockSpec((n, D_MODEL), lambda j: (0, 0)),
                  pl.BlockSpec((D_MODEL, D_MODEL), lambda j: (0, j)),
                  pl.BlockSpec((1, D_MODEL), lambda j: (0, j))],
        out_specs=pl.BlockSpec((n, D_MODEL), lambda j: (0, j)),
        out_shape=jax.ShapeDtypeStruct((n, w_ada.shape[1]), F32),
        compiler_params=_cparams(("arbitrary",)),
    )(c, w_ada, b_ada.reshape(1, -1))


def _norm_mod_kernel(x_ref, w_ref, sc_ref, sh_ref, o_ref):
    x = x_ref[...]
    y = x * lax.rsqrt(jnp.mean(x * x, axis=-1, keepdims=True) + EPS) * w_ref[...]
    o_ref[...] = (y * (1.0 + sc_ref[0]) + sh_ref[0]).astype(o_ref.dtype)


def _norm_mod(x2, w, scale, shift, seq, tm):
    t = x2.shape[0]
    per_b = seq // tm
    return pl.pallas_call(
        _norm_mod_kernel,
        grid=(t // tm,),
        in_specs=[pl.BlockSpec((tm, D_MODEL), lambda i: (i, 0)),
                  pl.BlockSpec((1, D_MODEL), lambda i: (0, 0)),
                  pl.BlockSpec((1, 1, D_MODEL), lambda i: (i // per_b, 0, 0)),
                  pl.BlockSpec((1, 1, D_MODEL), lambda i: (i // per_b, 0, 0))],
        out_specs=pl.BlockSpec((tm, D_MODEL), lambda i: (i, 0)),
        out_shape=jax.ShapeDtypeStruct((t, D_MODEL), BF16),
        compiler_params=_cparams(("parallel",)),
    )(x2, w.reshape(1, -1), scale, shift)


def _head_norm(acc, w):
    outs = []
    for h in range(N_HEADS):
        a = acc[:, h * HEAD_W:(h + 1) * HEAD_W]
        ms = _seg_allreduce(a * a, HEAD_DIM_A, jnp.add) * (1.0 / HEAD_DIM_A)
        outs.append(a * lax.rsqrt(ms + EPS) * w[:, h * HEAD_W:(h + 1) * HEAD_W])
    return jnp.concatenate(outs, axis=1)


def _proj_kernel(h_ref, w_ref, nw_ref, *o_refs, head_norm):
    acc = jnp.dot(h_ref[...], w_ref[...], preferred_element_type=F32)
    if head_norm:
        acc = _head_norm(acc, nw_ref[...])
    for o_ref in o_refs:
        o_ref[...] = acc.astype(o_ref.dtype)


def _proj(h, w_in_bf, group, out_dtypes, norm_w=None, tm=1024):
    t = h.shape[0]
    tm = min(tm, t)
    nw = jnp.ones((1, D_MODEL), F32) if norm_w is None else norm_w
    outs = pl.pallas_call(
        functools.partial(_proj_kernel, head_norm=norm_w is not None),
        grid=(t // tm,),
        in_specs=[pl.BlockSpec((tm, D_MODEL), lambda i: (i, 0)),
                  pl.BlockSpec((D_MODEL, D_MODEL), lambda i: (0, group)),
                  pl.BlockSpec((1, D_MODEL), lambda i: (0, 0))],
        out_specs=[pl.BlockSpec((tm, D_MODEL), lambda i: (i, 0)) for _ in out_dtypes],
        out_shape=[jax.ShapeDtypeStruct((t, D_MODEL), dt) for dt in out_dtypes],
        compiler_params=_cparams(("parallel",)),
    )(h, w_in_bf, nw)
    return outs


def _attn_kernel(q_ref, k_ref, v_ref, lam_ref, sw_ref, o_ref,
                 m_s, l_s, acc_s, *, tq, tk, lk, pos0):
    qi = pl.program_id(2)
    lane = lax.broadcasted_iota(I32, (tq, HEAD_W), 1)
    q = q_ref[...] * jnp.asarray(HEAD_DIM_A ** -0.5, BF16)
    zero = jnp.zeros_like(q)
    qs = (jnp.where(lane < HEAD_DIM_A, q, zero), jnp.where(lane >= HEAD_DIM_A, q, zero))

    m_s[...] = jnp.full(m_s.shape, NEG, F32)
    l_s[...] = jnp.zeros(l_s.shape, F32)
    acc_s[...] = jnp.zeros(acc_s.shape, F32)

    q_lo = pos0 + qi * tq
    n_free = ((q_lo // CHUNK + 1) * CHUNK) // tk
    k_hi = jnp.minimum(lk, ((q_lo + tq - 1) // CHUNK + 1) * CHUNK)
    n_all = (k_hi + tk - 1) // tk

    def step(kt, masked):
        k = k_ref[pl.ds(pl.multiple_of(kt * tk, tk), tk), :]
        v = v_ref[pl.ds(pl.multiple_of(kt * tk, tk), tk), :]
        if masked:
            qpos = q_lo + lax.broadcasted_iota(I32, (tq, tk), 0)
            kpos = kt * tk + lax.broadcasted_iota(I32, (tq, tk), 1)
            allowed = (kpos // CHUNK) <= (qpos // CHUNK)
        for m in range(2):
            s = lax.dot_general(qs[m], k, (((1,), (1,)), ((), ())), preferred_element_type=F32)
            if masked:
                s = jnp.where(allowed, s, NEG)
            m_old = m_s[m]
            m_new = jnp.maximum(m_old, jnp.max(s, axis=-1, keepdims=True))
            alpha = jnp.exp(m_old - m_new)
            p = jnp.exp(s - m_new)
            l_s[m] = alpha * l_s[m] + jnp.sum(p, axis=-1, keepdims=True)
            acc_s[m] = alpha * acc_s[m] + jnp.dot(p.astype(BF16), v, preferred_element_type=F32)
            m_s[m] = m_new

    def free_body(kt, c):
        step(kt, False)
        return c

    def masked_body(kt, c):
        step(kt, True)
        return c

    lax.fori_loop(0, n_free, free_body, 0)
    lax.fori_loop(n_free, n_all, masked_body, 0)

    lam = lam_ref[0, 0]
    o = acc_s[0] / l_s[0] - lam * (acc_s[1] / l_s[1])
    o = o * lax.rsqrt(jnp.mean(o * o, axis=-1, keepdims=True) + EPS) * sw_ref[...]
    o_ref[...] = (o * (1.0 - LAMBDA_INIT)).astype(o_ref.dtype)


def _attention(q, kb, vb, lam, subln_w, batch, lq, lk, tq, tk):
    nq = lq // tq
    kern = functools.partial(_attn_kernel, tq=tq, tk=tk, lk=lk, pos0=lk - lq)
    return pl.pallas_call(
        kern,
        grid=(batch, N_HEADS, nq),
        in_specs=[pl.BlockSpec((tq, HEAD_W), lambda b, h, i: (b * nq + i, h)),
                  pl.BlockSpec((lk, HEAD_W), lambda b, h, i: (b, h)),
                  pl.BlockSpec((lk, HEAD_W), lambda b, h, i: (b, h)),
                  pl.BlockSpec((1, 1), lambda b, h, i: (0, 0), memory_space=pltpu.SMEM),
                  pl.BlockSpec((1, HEAD_W), lambda b, h, i: (0, 0))],
        out_specs=pl.BlockSpec((tq, HEAD_W), lambda b, h, i: (b * nq + i, h)),
        out_shape=jax.ShapeDtypeStruct((batch * lq, D_MODEL), BF16),
        scratch_shapes=[pltpu.VMEM((2, tq, 1), F32), pltpu.VMEM((2, tq, 1), F32),
                        pltpu.VMEM((2, tq, HEAD_W), F32)],
        compiler_params=_cparams(("parallel", "parallel", "arbitrary")),
    )(q, kb, vb, lam, subln_w.reshape(1, -1))


def _hgrn_kernel(qh_ref, fh_ref, ih_ref, gh_ref, lb_ref, nw_ref, s0_ref, o_ref, s_out_ref,
                 st_s, *, ct, c):
    t = pl.program_id(1)
    nt = pl.num_programs(1)

    @pl.when(t == 0)
    def _():
        for h in range(N_HEADS):
            st_s[h] = s0_ref[0, h].T

    a = lb_ref[...]
    amax = jnp.max(a, axis=0, keepdims=True)
    e = jnp.exp(a - amax)
    lb = e[0:1] / jnp.sum(e, axis=0, keepdims=True)

    row = lax.broadcasted_iota(I32, (c, c), 0)
    col = lax.broadcasted_iota(I32, (c, c), 1)
    causal = col <= row
    tril = causal.astype(F32)

    def chunk(ci, carry):
        rows = pl.ds(pl.multiple_of(ci * c, c), c)
        f = lb + (1.0 - lb) * _sigmoid(fh_ref[rows, :])
        g = jnp.log(f)
        b = jnp.dot(tril, g, precision=lax.Precision.HIGHEST, preferred_element_type=F32)
        b_last = b[c - 1:c, :]
        b_mid = b[c // 2 - 1:c // 2, :]
        q = _silu(qh_ref[rows, :].astype(F32)) * (HEAD_W ** -0.5)
        kk = 1.0 - f
        q_in = (q * jnp.exp(b)).astype(BF16)
        q_mid = (q * jnp.exp(b - b_mid)).astype(BF16)
        k_mid = (kk * jnp.exp(b_mid - b)).astype(BF16)
        k_end = (kk * jnp.exp(b_last - b)).astype(BF16)
        decay = jnp.exp(b_last)
        v = ih_ref[rows, :]
        gate = _silu(gh_ref[rows, :].astype(F32))
        nw = nw_ref[...]
        for h in range(N_HEADS):
            sl = slice(h * HEAD_W, (h + 1) * HEAD_W)
            st = st_s[h]
            inter = lax.dot_general(q_in[:, sl], st.astype(BF16), (((1,), (1,)), ((), ())),
                                    preferred_element_type=F32)
            att = lax.dot_general(q_mid[:, sl], k_mid[:, sl], (((1,), (1,)), ((), ())),
                                  preferred_element_type=F32)
            att = jnp.where(causal, att, 0.0)
            o = inter + jnp.dot(att.astype(BF16), v[:, sl], preferred_element_type=F32)
            upd = lax.dot_general(v[:, sl], k_end[:, sl], (((0,), (0,)), ((), ())),
                                  preferred_element_type=F32)
            st_s[h] = decay[:, sl] * st + upd
            o = o * lax.rsqrt(jnp.mean(o * o, axis=-1, keepdims=True) + EPS) * nw[:, sl]
            o_ref[rows, sl] = (o * gate[:, sl]).astype(o_ref.dtype)
        return carry

    lax.fori_loop(0, ct // c, chunk, 0)

    @pl.when(t == nt - 1)
    def _():
        for h in range(N_HEADS):
            s_out_ref[0, h] = st_s[h].T


def _hgrn(qh, fh, ih, gh, hgrn_lb, norm_w, s0, batch, seq):
    ct = min(seq, 256)
    c = min(GLA_CHUNK, ct)
    nt = seq // ct
    kern = functools.partial(_hgrn_kernel, ct=ct, c=c)
    tok = pl.BlockSpec((ct, D_MODEL), lambda b, t: (b * nt + t, 0))
    st_spec = pl.BlockSpec((1, N_HEADS, HEAD_W, HEAD_W), lambda b, t: (b, 0, 0, 0))
    return pl.pallas_call(
        kern,
        grid=(batch, nt),
        in_specs=[tok, tok, tok, tok,
                  pl.BlockSpec((2, D_MODEL), lambda b, t: (0, 0)),
                  pl.BlockSpec((1, D_MODEL), lambda b, t: (0, 0)),
                  st_spec],
        out_specs=[tok, st_spec],
        out_shape=[jax.ShapeDtypeStruct((batch * seq, D_MODEL), BF16),
                   jax.ShapeDtypeStruct((batch, N_HEADS, HEAD_W, HEAD_W), F32)],
        scratch_shapes=[pltpu.VMEM((N_HEADS, HEAD_W, HEAD_W), F32)],
        compiler_params=_cparams(("parallel", "arbitrary")),
    )(qh, fh, ih, gh, hgrn_lb, jnp.tile(norm_w.reshape(1, -1), (1, N_HEADS)), s0)


def _halves(fn, x):
    return jnp.concatenate([fn(x[:, :LANES]), fn(x[:, LANES:])], axis=1)


def _post_kernel(oa_ref, ob_ref, ga_ref, gb_ref, x_ref, g1_ref, sc2_ref, sh2_ref, g2_ref,
                 wpa_ref, wpb_ref, wo_ref, n2_ref, wr_ref, br_ref, wsg_ref, wsu_ref, wsd_ref,
                 base_ref, hp_ref, eidx_ref, wts_ref, pos_ref, cnt_ref, carry_s, *, tm):
    i = pl.program_id(0)

    @pl.when(i == 0)
    def _():
        carry_s[...] = jnp.zeros(carry_s.shape, F32)

    pa = jnp.dot(oa_ref[...], wpa_ref[...], preferred_element_type=F32)
    pb = jnp.dot(ob_ref[...], wpb_ref[...], preferred_element_type=F32)
    u = _sigmoid(ga_ref[...].astype(F32)) * pa + _sigmoid(gb_ref[...].astype(F32)) * pb
    x1 = x_ref[...] + g1_ref[0] * jnp.dot(u.astype(BF16), wo_ref[...], preferred_element_type=F32)
    y = x1 * lax.rsqrt(jnp.mean(x1 * x1, axis=-1, keepdims=True) + EPS) * n2_ref[...]
    h2 = y * (1.0 + sc2_ref[0]) + sh2_ref[0]
    hp_ref[...] = _pack_halves(h2)

    hb = h2.astype(BF16)
    sg = jnp.dot(hb, wsg_ref[...], preferred_element_type=F32)
    su = jnp.dot(hb, wsu_ref[...], preferred_element_type=F32)
    shared = jnp.dot((_silu(sg) * su).astype(BF16), wsd_ref[...], preferred_element_type=F32)
    base_ref[...] = x1 + g2_ref[0] * shared

    logits = jnp.dot(h2, wr_ref[...], precision=lax.Precision.HIGHEST, preferred_element_type=F32)
    scores = _sigmoid(logits)
    biased = scores + br_ref[...]
    lane = lax.broadcasted_iota(I32, (tm, N_EXPERTS), 1).astype(F32)
    lane_in = lax.broadcasted_iota(I32, (tm, LANES), 1).astype(F32)
    lane2 = jnp.concatenate([lane_in, lane_in], axis=1)
    big = float(2 * N_EXPERTS)

    segmax = functools.partial(_seg_allreduce, width=GROUP_W, op=jnp.maximum)
    segmin = functools.partial(_seg_allreduce, width=GROUP_W, op=jnp.minimum)
    m1 = _halves(segmax, biased)
    i1 = _halves(segmin, jnp.where(biased == m1, lane2, big))
    m2 = _halves(segmax, jnp.where(lane2 == i1, NEG, biased))
    gs = m1 + m2

    grp = jnp.floor(lane * (1.0 / GROUP_W))
    rem = gs
    gmask = jnp.zeros((tm, N_EXPERTS), jnp.bool_)
    for _ in range(TOPK_GROUPS):
        gm = jnp.max(rem, axis=-1, keepdims=True)
        first = jnp.min(jnp.where(rem == gm, lane, big), axis=-1, keepdims=True)
        sel = grp == jnp.floor(first * (1.0 / GROUP_W))
        gmask = jnp.logical_or(gmask, sel)
        rem = jnp.where(sel, NEG, rem)

    masked = jnp.where(gmask, biased, NEG)
    onehot = jnp.zeros((tm, N_EXPERTS), F32)
    idxs, ws = [], []
    for _ in range(TOP_K):
        mk = jnp.max(masked, axis=-1, keepdims=True)
        ik = jnp.min(jnp.where(masked == mk, lane, big), axis=-1, keepdims=True)
        selk = lane == ik
        ws.append(jnp.sum(jnp.where(selk, scores, 0.0), axis=-1, keepdims=True))
        idxs.append(ik)
        masked = jnp.where(selk, NEG, masked)
        onehot = jnp.where(selk, 1.0, onehot)

    wsum = ws[0]
    for k in range(1, TOP_K):
        wsum = wsum + ws[k]

    r = lax.broadcasted_iota(I32, (tm, tm), 0)
    cidx = lax.broadcasted_iota(I32, (tm, tm), 1)
    strict = (cidx < r).astype(BF16)
    cum = jnp.dot(strict, onehot.astype(BF16), preferred_element_type=F32) + carry_s[...]
    carry_s[...] = carry_s[...] + jnp.sum(onehot, axis=0, keepdims=True)
    cnt_ref[...] = carry_s[...]

    lane_o = lax.broadcasted_iota(I32, (tm, LANES), 1)
    e_out = jnp.zeros((tm, LANES), F32)
    w_out = jnp.zeros((tm, LANES), F32)
    p_out = jnp.zeros((tm, LANES), F32)
    for k in range(TOP_K):
        pk = jnp.sum(jnp.where(lane == idxs[k], cum, 0.0), axis=-1, keepdims=True)
        e_out = jnp.where(lane_o == k, idxs[k], e_out)
        w_out = jnp.where(lane_o == k, ws[k] / wsum * ROUTED_SCALE, w_out)
        p_out = jnp.where(lane_o == k, pk, p_out)
    eidx_ref[...] = e_out.astype(I32)
    wts_ref[...] = w_out
    pos_ref[...] = p_out.astype(I32)


def _post(oa, ob, ga, gb, x2, g1, sc2, sh2, g2, wpa, wpb, wo, n2, wr, br, wsg, wsu, wsd, seq, tm):
    t = x2.shape[0]
    per_b = seq // tm
    tok = lambda: pl.BlockSpec((tm, D_MODEL), lambda i: (i, 0))
    mod = lambda: pl.BlockSpec((1, 1, D_MODEL), lambda i: (i // per_b, 0, 0))
    full = lambda a: pl.BlockSpec(a.shape, lambda i: (0,) * a.ndim)
    n2 = n2.reshape(1, -1)
    br = br.reshape(1, -1)
    lane_out = lambda: pl.BlockSpec((tm, LANES), lambda i: (i, 0))
    return pl.pallas_call(
        functools.partial(_post_kernel, tm=tm),
        grid=(t // tm,),
        in_specs=[tok(), tok(), tok(), tok(), tok(), mod(), mod(), mod(), mod(),
                  full(wpa), full(wpb), full(wo), full(n2), full(wr), full(br),
                  full(wsg), full(wsu), full(wsd)],
        out_specs=[tok(), pl.BlockSpec((tm, D_MODEL // 2), lambda i: (i, 0)),
                   lane_out(), lane_out(), lane_out(),
                   pl.BlockSpec((1, N_EXPERTS), lambda i: (0, 0))],
        out_shape=[jax.ShapeDtypeStruct((t, D_MODEL), F32),
                   jax.ShapeDtypeStruct((t, D_MODEL // 2), U32),
                   jax.ShapeDtypeStruct((t, LANES), I32),
                   jax.ShapeDtypeStruct((t, LANES), F32),
                   jax.ShapeDtypeStruct((t, LANES), I32),
                   jax.ShapeDtypeStruct((1, N_EXPERTS), F32)],
        scratch_shapes=[pltpu.VMEM((1, N_EXPERTS), F32)],
        compiler_params=_cparams(("arbitrary",)),
    )(oa, ob, ga, gb, x2, g1, sc2, sh2, g2, wpa, wpb, wo, n2, wr, br, wsg, wsu, wsd)


def _row_copy(src_ref, s, dst_ref, d, sem):
    return pltpu.make_async_copy(src_ref.at[pl.ds(s, 1), :], dst_ref.at[pl.ds(d, 1), :], sem)


def _dispatch_kernel(dest_ref, hp_ref, xs_in_ref, xs_ref, sem, *, tm):
    del xs_in_ref

    def issue(t, c):
        for k in range(TOP_K):
            _row_copy(hp_ref, t, xs_ref, dest_ref[t * TOP_K + k], sem).start()
        return c

    def drain(t, c):
        for k in range(TOP_K):
            _row_copy(hp_ref, 0, xs_ref, 0, sem).wait()
        return c

    lax.fori_loop(0, tm, issue, 0)
    lax.fori_loop(0, tm, drain, 0)


def _dispatch(hp, dest_flat, rows, tm):
    t = hp.shape[0]
    zeros = jnp.zeros((rows, D_MODEL // 2), U32)
    return pl.pallas_call(
        functools.partial(_dispatch_kernel, tm=tm),
        grid=(t // tm,),
        in_specs=[pl.BlockSpec((tm * TOP_K,), lambda i: (i,), memory_space=pltpu.SMEM),
                  pl.BlockSpec((tm, D_MODEL // 2), lambda i: (i, 0)),
                  pl.BlockSpec(memory_space=pl.ANY)],
        out_specs=pl.BlockSpec(memory_space=pl.ANY),
        out_shape=jax.ShapeDtypeStruct((rows, D_MODEL // 2), U32),
        scratch_shapes=[pltpu.SemaphoreType.DMA],
        input_output_aliases={2: 0},
        compiler_params=_cparams(("arbitrary",)),
    )(dest_flat, hp, zeros)


def _expert_kernel(be_ref, nu_ref, xs_ref, wg_ref, wu_ref, wd_ref, ys_ref):
    i = pl.program_id(0)

    @pl.when(i < nu_ref[0])
    def _():
        x = _unpack_halves(xs_ref[...]).astype(BF16)
        g = jnp.dot(x, wg_ref[0].astype(BF16), preferred_element_type=F32)
        u = jnp.dot(x, wu_ref[0].astype(BF16), preferred_element_type=F32)
        hmid = (_silu(g) * u).astype(BF16)
        ys_ref[...] = _pack_halves(jnp.dot(hmid, wd_ref[0].astype(BF16), preferred_element_type=F32))

    @pl.when(i >= nu_ref[0])
    def _():
        ys_ref[...] = jnp.zeros(ys_ref.shape, ys_ref.dtype)


def _experts(xs, blk_e, n_used, w_eg, w_eu, w_ed):
    rows = xs.shape[0]
    nb = rows // ROW_BLOCK
    grid_spec = pltpu.PrefetchScalarGridSpec(
        num_scalar_prefetch=2,
        grid=(nb,),
        in_specs=[pl.BlockSpec((ROW_BLOCK, D_MODEL // 2), lambda i, be, nu: (i, 0)),
                  pl.BlockSpec((1, D_MODEL, D_EXPERT), lambda i, be, nu: (be[i], 0, 0)),
                  pl.BlockSpec((1, D_MODEL, D_EXPERT), lambda i, be, nu: (be[i], 0, 0)),
                  pl.BlockSpec((1, D_EXPERT, D_MODEL), lambda i, be, nu: (be[i], 0, 0))],
        out_specs=pl.BlockSpec((ROW_BLOCK, D_MODEL // 2), lambda i, be, nu: (i, 0)),
    )
    return pl.pallas_call(
        _expert_kernel,
        grid_spec=grid_spec,
        out_shape=jax.ShapeDtypeStruct((rows, D_MODEL // 2), U32),
        compiler_params=_cparams(("arbitrary",)),
    )(blk_e, n_used, xs, w_eg, w_eu, w_ed)


def _combine_kernel(dest_ref, base_ref, wts_ref, g2_ref, ys_ref, o_ref, buf, sem, *, tm):
    def issue(t, c):
        for k in range(TOP_K):
            pltpu.make_async_copy(ys_ref.at[pl.ds(dest_ref[t * TOP_K + k], 1), :],
                                  buf.at[k, pl.ds(t, 1), :], sem).start()
        return c

    def drain(t, c):
        for k in range(TOP_K):
            pltpu.make_async_copy(ys_ref.at[pl.ds(0, 1), :], buf.at[0, pl.ds(0, 1), :], sem).wait()
        return c

    lax.fori_loop(0, tm, issue, 0)
    lax.fori_loop(0, tm, drain, 0)

    w = wts_ref[...]
    y = jnp.zeros((tm, D_MODEL), F32)
    for k in range(TOP_K):
        y = y + w[:, k:k + 1] * _unpack_halves(buf[k])
    o_ref[...] = base_ref[...] + g2_ref[0] * y


def _combine(base, wts, g2, ys, dest_flat, seq, tm):
    t = base.shape[0]
    per_b = seq // tm
    return pl.pallas_call(
        functools.partial(_combine_kernel, tm=tm),
        grid=(t // tm,),
        in_specs=[pl.BlockSpec((tm * TOP_K,), lambda i: (i,), memory_space=pltpu.SMEM),
                  pl.BlockSpec((tm, D_MODEL), lambda i: (i, 0)),
                  pl.BlockSpec((tm, LANES), lambda i: (i, 0)),
                  pl.BlockSpec((1, 1, D_MODEL), lambda i: (i // per_b, 0, 0)),
                  pl.BlockSpec(memory_space=pl.ANY)],
        out_specs=pl.BlockSpec((tm, D_MODEL), lambda i: (i, 0)),
        out_shape=jax.ShapeDtypeStruct((t, D_MODEL), F32),
        scratch_shapes=[pltpu.VMEM((TOP_K, tm, D_MODEL // 2), U32), pltpu.SemaphoreType.DMA],
        compiler_params=_cparams(("arbitrary",)),
    )(dest_flat, base, wts, g2, ys)


def _moe_tile(seq):
    return 128 if seq % 128 == 0 else seq


def _layer(x, mods, past_k, past_v, s0, p):
    batch, seq, _ = x.shape
    t = batch * seq
    x2 = x.reshape(t, D_MODEL)
    shift1, scale1, gate1, shift2, scale2, gate2 = mods

    h = _norm_mod(x2, p["norm1_w"], scale1, shift1, seq, min(seq, 512))
    qn = jnp.tile(p["q_norm_w"].reshape(1, -1), (1, D_MODEL // HEAD_DIM_A))
    kn = jnp.tile(p["k_norm_w"].reshape(1, -1), (1, D_MODEL // HEAD_DIM_A))
    w_in = p["w_in"]
    (q,) = _proj(h, w_in, 0, [BF16], qn)
    k, kb = _proj(h, w_in, 1, [F32, BF16], kn)
    v, vb = _proj(h, w_in, 2, [F32, BF16])
    (qh,) = _proj(h, w_in, 3, [BF16])
    (fh,) = _proj(h, w_in, 4, [F32])
    (ih,) = _proj(h, w_in, 5, [BF16])
    (gh,) = _proj(h, w_in, 6, [BF16])
    (ga,) = _proj(h, w_in, 7, [BF16])
    (gb,) = _proj(h, w_in, 8, [BF16])

    if past_k is None:
        lk = seq
        tq = tk = min(seq, 256)
    else:
        past = past_k.shape[1]
        lk = past + seq
        kb = jnp.concatenate([past_k.reshape(batch, past, D_MODEL).astype(BF16),
                              kb.reshape(batch, seq, D_MODEL)], axis=1).reshape(batch * lk, D_MODEL)
        vb = jnp.concatenate([past_v.reshape(batch, past, D_MODEL).astype(BF16),
                              vb.reshape(batch, seq, D_MODEL)], axis=1).reshape(batch * lk, D_MODEL)
        tq, tk = seq, lk
    o_a = _attention(q, kb, vb, p["lam"], p["subln_w"], batch, seq, lk, tq, tk)
    o_b, s_new = _hgrn(qh, fh, ih, gh, p["hgrn_lb"], p["hgrn_norm_w"], s0, batch, seq)

    tm = min(seq, 256)
    base, hp, eidx, wts, pos, counts = _post(
        o_a, o_b, ga, gb, x2, gate1, scale2, shift2, gate2,
        p["w_proj_a"], p["w_proj_b"], p["w_out"], p["norm2_w"], p["w_router"], p["b_router"],
        p["w_sh_gate"], p["w_sh_up"], p["w_sh_down"], seq, tm)

    counts = counts.reshape(N_EXPERTS).astype(I32)
    pcounts = (counts + ROW_BLOCK - 1) // ROW_BLOCK * ROW_BLOCK
    pend = jnp.cumsum(pcounts)
    pstart = pend - pcounts
    n_assign = t * TOP_K
    nb = -(-(n_assign + N_EXPERTS * (ROW_BLOCK - 1)) // ROW_BLOCK)
    rows = nb * ROW_BLOCK
    dest = (pstart[eidx[:, :TOP_K]] + pos[:, :TOP_K]).reshape(n_assign)
    blk_e = jnp.minimum(jnp.searchsorted(pend, jnp.arange(nb, dtype=I32) * ROW_BLOCK, side="right"),
                        N_EXPERTS - 1).astype(I32)
    n_used = (pend[-1] // ROW_BLOCK).astype(I32).reshape(1)

    tmd = _moe_tile(seq)
    xs = _dispatch(hp, dest, rows, tmd)
    ys = _experts(xs, blk_e, n_used, p["w_exp_gate"], p["w_exp_up"], p["w_exp_down"])
    y = _combine(base, wts, gate2, ys, dest, seq, tmd)
    return y.reshape(batch, seq, D_MODEL), k, v, s_new


def kernel(x_prompt, x_sample, cache_attn_k, cache_attn_v, state_hgrn, c_prompt, c_sample, w_ada, b_ada, norm1_w, norm2_w, w_in, q_norm_w, k_norm_w, lambda_q1, lambda_k1, lambda_q2, lambda_k2, subln_w, hgrn_lb, hgrn_norm_w, w_proj_a, w_proj_b, w_out, w_router, b_router, w_exp_gate, w_exp_up, w_exp_down, w_sh_gate, w_sh_up, w_sh_down):
    bp, lp, _ = x_prompt.shape
    bs, ls, _ = x_sample.shape
    l = 0
    lam = (jnp.exp(jnp.sum(lambda_q1[l] * lambda_k1[l])) - jnp.exp(jnp.sum(lambda_q2[l] * lambda_k2[l]))
           + LAMBDA_INIT).astype(F32).reshape(1, 1)
    p = dict(
        norm1_w=norm1_w[l], norm2_w=norm2_w[l], w_in=w_in[l].astype(BF16),
        q_norm_w=q_norm_w[l], k_norm_w=k_norm_w[l], lam=lam, subln_w=subln_w[l],
        hgrn_lb=hgrn_lb, hgrn_norm_w=hgrn_norm_w[l],
        w_proj_a=w_proj_a[l].astype(BF16), w_proj_b=w_proj_b[l].astype(BF16), w_out=w_out[l].astype(BF16),
        w_router=w_router[l], b_router=b_router[l],
        w_exp_gate=w_exp_gate[l], w_exp_up=w_exp_up[l], w_exp_down=w_exp_down[l],
        w_sh_gate=w_sh_gate[l].astype(BF16), w_sh_up=w_sh_up[l].astype(BF16), w_sh_down=w_sh_down[l].astype(BF16),
    )
    mod = _ada(jnp.concatenate([c_prompt, c_sample], axis=0), w_ada[l], b_ada[l])
    mod = mod.reshape(bp + bs, 6, 1, D_MODEL)
    mods_p = [mod[:bp, j] for j in range(6)]
    mods_s = [mod[bp:, j] for j in range(6)]

    zero_state = jnp.zeros((bp, N_HEADS, HEAD_W, HEAD_W), F32)
    yp, kp, vp, sp = _layer(x_prompt, mods_p, None, None, zero_state, p)
    ys, ks, vs, ss = _layer(x_sample, mods_s, cache_attn_k[l], cache_attn_v[l], state_hgrn[l], p)

    def heads(a, b_, s_):
        return a.reshape(1, b_, s_, N_HEADS, 2, HEAD_DIM_A), None

    return (yp, ys,
            kp.reshape(1, bp, lp, N_HEADS, 2, HEAD_DIM_A), vp.reshape(1, bp, lp, N_HEADS, HEAD_W), sp[None],
            ks.reshape(1, bs, ls, N_HEADS, 2, HEAD_DIM_A), vs.reshape(1, bs, ls, N_HEADS, HEAD_W), ss[None])
```

```python
import functools
import math

import jax
import jax.numpy as jnp
from jax import lax
from jax.experimental import pallas as pl
from jax.experimental.pallas import tpu as pltpu

F32 = jnp.float32
BF16 = jnp.bfloat16
U32 = jnp.uint32
I32 = jnp.int32

D_MODEL = 1024
N_HEADS = 8
HEAD_W = 128
HEAD_DIM_A = 64
CHUNK = 64
N_EXPERTS = 256
TOP_K = 8
N_GROUPS = 8
GROUP_W = N_EXPERTS // N_GROUPS
TOPK_GROUPS = 4
D_EXPERT = 256
ROUTED_SCALE = 2.5
EPS = 1e-6
LAMBDA_INIT = 0.8 - 0.6 * math.exp(-0.3 * 0)
LANES = 128
NEG = -1e30
LOG2E = math.log2(math.e)

GLA_CHUNK = 32
ROW_BLOCK = 256
VMEM_LIMIT = 48 * 1024 * 1024


def _sigmoid(x):
    return 1.0 / (1.0 + jnp.exp(-x))


def _silu(x):
    return x * _sigmoid(x)


def _cparams(sem):
    return pltpu.CompilerParams(dimension_semantics=sem, vmem_limit_bytes=VMEM_LIMIT)


def _pack_halves(x):
    half = x.shape[1] // 2
    lo = pltpu.bitcast(x[:, :half].astype(BF16).astype(F32), U32)
    hi = pltpu.bitcast(x[:, half:].astype(BF16).astype(F32), U32)
    return (hi & jnp.uint32(0xFFFF0000)) | (lo >> 16)


def _mod_spec(a, seq, tm):
    if a.ndim == 3:
        per_b = seq // tm
        return pl.BlockSpec((1, 1, D_MODEL), lambda i: (i // per_b, 0, 0))
    return pl.BlockSpec((tm, D_MODEL), lambda i: (i, 0))


def _mod_val(ref):
    v = ref[...]
    return v.reshape(v.shape[-2], v.shape[-1])


def _unpack_halves(w):
    lo = pltpu.bitcast(w << 16, F32)
    hi = pltpu.bitcast(w & jnp.uint32(0xFFFF0000), F32)
    return jnp.concatenate([lo, hi], axis=1)


def _ada_kernel(c_ref, w_ref, b_ref, o_ref):
    c = _silu(c_ref[...])
    o_ref[...] = jnp.dot(c, w_ref[...], precision=lax.Precision.HIGHEST,
                         preferred_element_type=F32) + b_ref[...]


def _ada(c, w_ada, b_ada):
    n = c.shape[0]
    nj = w_ada.shape[1] // D_MODEL
    return pl.pallas_call(
        _ada_kernel,
        grid=(nj,),
        in_specs=[pl.BlockSpec((n, D_MODEL), lambda j: (0, 0)),
                  pl.BlockSpec((D_MODEL, D_MODEL), lambda j: (0, j)),
                  pl.BlockSpec((1, D_MODEL), lambda j: (0, j))],
        out_specs=pl.BlockSpec((n, D_MODEL), lambda j: (0, j)),
        out_shape=jax.ShapeDtypeStruct((n, w_ada.shape[1]), F32),
        compiler_params=_cparams(("arbitrary",)),
    )(c, w_ada, b_ada.reshape(1, -1))


def _norm_mod_kernel(x_ref, w_ref, sc_ref, sh_ref, o_ref):
    x = x_ref[...]
    y = x * lax.rsqrt(jnp.mean(x * x, axis=-1, keepdims=True) + EPS) * w_ref[...]
    o_ref[...] = (y * (1.0 + _mod_val(sc_ref)) + _mod_val(sh_ref)).astype(o_ref.dtype)


def _norm_mod(x2, w, scale, shift, seq, tm):
    t = x2.shape[0]
    return pl.pallas_call(
        _norm_mod_kernel,
        grid=(t // tm,),
        in_specs=[pl.BlockSpec((tm, D_MODEL), lambda i: (i, 0)),
                  pl.BlockSpec((1, D_MODEL), lambda i: (0, 0)),
                  _mod_spec(scale, seq, tm), _mod_spec(shift, seq, tm)],
        out_specs=pl.BlockSpec((tm, D_MODEL), lambda i: (i, 0)),
        out_shape=jax.ShapeDtypeStruct((t, D_MODEL), BF16),
        compiler_params=_cparams(("parallel",)),
    )(x2, w.reshape(1, -1), scale, shift)


def _head_norm(acc, w):
    r = lax.broadcasted_iota(I32, (HEAD_W, HEAD_W), 0) // HEAD_DIM_A
    c = lax.broadcasted_iota(I32, (HEAD_W, HEAD_W), 1) // HEAD_DIM_A
    ones = (r == c).astype(BF16)
    outs = []
    for h in range(N_HEADS):
        a = acc[:, h * HEAD_W:(h + 1) * HEAD_W]
        sq = a * a
        hi = sq.astype(BF16)
        lo = (sq - hi.astype(F32)).astype(BF16)
        ss = (jnp.dot(hi, ones, preferred_element_type=F32)
              + jnp.dot(lo, ones, preferred_element_type=F32))
        outs.append(a * lax.rsqrt(ss * (1.0 / HEAD_DIM_A) + EPS) * w[:, h * HEAD_W:(h + 1) * HEAD_W])
    return jnp.concatenate(outs, axis=1)


def _proj_kernel(h_ref, w_ref, nw_ref, *o_refs, head_norm, out_scale):
    acc = jnp.dot(h_ref[...], w_ref[...], preferred_element_type=F32)
    if head_norm:
        acc = _head_norm(acc, nw_ref[...])
    for o_ref, s in zip(o_refs, out_scale):
        o_ref[...] = (acc if s == 1.0 else acc * s).astype(o_ref.dtype)


def _proj(h, w_in_bf, group, out_dtypes, norm_w=None, out_scale=None, tm=1024):
    t = h.shape[0]
    tm = min(tm, t)
    nw = jnp.ones((1, D_MODEL), F32) if norm_w is None else norm_w
    out_scale = tuple(out_scale or (1.0,) * len(out_dtypes))
    outs = pl.pallas_call(
        functools.partial(_proj_kernel, head_norm=norm_w is not None, out_scale=out_scale),
        grid=(t // tm,),
        in_specs=[pl.BlockSpec((tm, D_MODEL), lambda i: (i, 0)),
                  pl.BlockSpec((D_MODEL, D_MODEL), lambda i: (0, group)),
                  pl.BlockSpec((1, D_MODEL), lambda i: (0, 0))],
        out_specs=[pl.BlockSpec((tm, D_MODEL), lambda i: (i, 0)) for _ in out_dtypes],
        out_shape=[jax.ShapeDtypeStruct((t, D_MODEL), dt) for dt in out_dtypes],
        compiler_params=_cparams(("parallel",)),
    )(h, w_in_bf, nw)
    return outs


def _split_maps(q):
    lane = lax.broadcasted_iota(I32, q.shape, 1)
    zero = jnp.zeros_like(q)
    return jnp.where(lane < HEAD_DIM_A, q, zero), jnp.where(lane >= HEAD_DIM_A, q, zero)


def _finish_head(acc1, l1, acc2, l2, lam, sw):
    o = acc1 / l1 - lam * (acc2 / l2)
    o = o * lax.rsqrt(jnp.mean(o * o, axis=-1, keepdims=True) + EPS) * sw
    return o * (1.0 - LAMBDA_INIT)


def _attn_kernel(q_ref, k_ref, v_ref, lam_ref, sw_ref, o_ref, mx_s, mrep_s, ls_s, acc_s,
                 *, tq, tk, hp):
    qi = pl.program_id(2)
    nf = tk // LANES
    qs = []
    for hh in range(hp):
        qs.extend(_split_maps(q_ref[:, hh * HEAD_W:(hh + 1) * HEAD_W]))
    ns = 2 * hp

    r_chunk = lax.broadcasted_iota(I32, (tq, tk), 0) // CHUNK
    c_chunk = lax.broadcasted_iota(I32, (tq, tk), 1) // CHUNK
    diag_ok = c_chunk <= r_chunk

    def scores(kt, s, masked):
        k = k_ref[pl.ds(pl.multiple_of(kt * tk, tk), tk), (s // 2) * HEAD_W:(s // 2 + 1) * HEAD_W]
        sc = lax.dot_general(qs[s], k, (((1,), (1,)), ((), ())), preferred_element_type=F32)
        return jnp.where(diag_ok, sc, NEG) if masked else sc

    def fold(x, op):
        r = x[:, :LANES]
        for j in range(1, nf):
            r = op(r, x[:, j * LANES:(j + 1) * LANES])
        return r

    def pass1(kt, first):
        for s in range(ns):
            m = fold(scores(kt, s, first), jnp.maximum)
            mx_s[s] = m if first else jnp.maximum(mx_s[s], m)

    def pass2(kt, first):
        for s in range(ns):
            m = mrep_s[s]
            p = jnp.exp2(scores(kt, s, first) - jnp.concatenate([m] * nf, axis=1))
            v = v_ref[pl.ds(pl.multiple_of(kt * tk, tk), tk), (s // 2) * HEAD_W:(s // 2 + 1) * HEAD_W]
            pv = jnp.dot(p.astype(BF16), v, preferred_element_type=F32)
            ls_s[s] = fold(p, jnp.add) if first else ls_s[s] + fold(p, jnp.add)
            acc_s[s] = pv if first else acc_s[s] + pv

    def loop(fn):
        def body(kt, c):
            fn(kt, False)
            return c

        fn(qi, True)
        lax.fori_loop(0, qi, body, 0)

    loop(pass1)
    for s in range(ns):
        mrep_s[s] = jnp.broadcast_to(jnp.max(mx_s[s], axis=-1, keepdims=True), (tq, LANES))
    loop(pass2)

    lam = lam_ref[0, 0]
    for hh in range(hp):
        l1 = jnp.sum(ls_s[2 * hh], axis=-1, keepdims=True)
        l2 = jnp.sum(ls_s[2 * hh + 1], axis=-1, keepdims=True)
        o = _finish_head(acc_s[2 * hh], l1, acc_s[2 * hh + 1], l2, lam, sw_ref[...])
        o_ref[:, hh * HEAD_W:(hh + 1) * HEAD_W] = o.astype(o_ref.dtype)


def _attention_prompt(q, kb, vb, lam, subln_w, batch, seq):
    tq = tk = min(seq, 256)
    hp = 8
    nq = seq // tq
    kern = functools.partial(_attn_kernel, tq=tq, tk=tk, hp=hp)
    w = hp * HEAD_W
    return pl.pallas_call(
        kern,
        grid=(batch, N_HEADS // hp, nq),
        in_specs=[pl.BlockSpec((tq, w), lambda b, h, i: (b * nq + i, h)),
                  pl.BlockSpec((seq, w), lambda b, h, i: (b, h)),
                  pl.BlockSpec((seq, w), lambda b, h, i: (b, h)),
                  pl.BlockSpec((1, 1), lambda b, h, i: (0, 0), memory_space=pltpu.SMEM),
                  pl.BlockSpec((1, HEAD_W), lambda b, h, i: (0, 0))],
        out_specs=pl.BlockSpec((tq, w), lambda b, h, i: (b * nq + i, h)),
        out_shape=jax.ShapeDtypeStruct((batch * seq, D_MODEL), BF16),
        scratch_shapes=[pltpu.VMEM((2 * hp, tq, LANES), F32) for _ in range(4)],
        compiler_params=_cparams(("parallel", "parallel", "arbitrary")),
    )(q, kb, vb, lam, subln_w.reshape(1, -1))


def _attn_cached_kernel(q_ref, k_ref, v_ref, lam_ref, sw_ref, o_ref, *, lq, lk):
    qpos = (lk - lq) + lax.broadcasted_iota(I32, (lq, lk), 0)
    kpos = lax.broadcasted_iota(I32, (lq, lk), 1)
    allowed = (kpos // CHUNK) <= (qpos // CHUNK)
    lam = lam_ref[0, 0]
    for h in range(N_HEADS):
        sl = slice(h * HEAD_W, (h + 1) * HEAD_W)
        k = k_ref[:, sl]
        v = v_ref[:, sl]
        accs, ls = [], []
        for qm in _split_maps(q_ref[:, sl]):
            sc = lax.dot_general(qm, k, (((1,), (1,)), ((), ())), preferred_element_type=F32)
            sc = jnp.where(allowed, sc, NEG)
            p = jnp.exp2(sc - jnp.max(sc, axis=-1, keepdims=True))
            ls.append(jnp.sum(p, axis=-1, keepdims=True))
            accs.append(jnp.dot(p.astype(BF16), v, preferred_element_type=F32))
        o = _finish_head(accs[0], ls[0], accs[1], ls[1], lam, sw_ref[...])
        o_ref[:, sl] = o.astype(o_ref.dtype)


def _attention_cached(q, kb, vb, lam, subln_w, batch, lq, lk):
    return pl.pallas_call(
        functools.partial(_attn_cached_kernel, lq=lq, lk=lk),
        grid=(batch,),
        in_specs=[pl.BlockSpec((lq, D_MODEL), lambda b: (b, 0)),
                  pl.BlockSpec((lk, D_MODEL), lambda b: (b, 0)),
                  pl.BlockSpec((lk, D_MODEL), lambda b: (b, 0)),
                  pl.BlockSpec((1, 1), lambda b: (0, 0), memory_space=pltpu.SMEM),
                  pl.BlockSpec((1, HEAD_W), lambda b: (0, 0))],
        out_specs=pl.BlockSpec((lq, D_MODEL), lambda b: (b, 0)),
        out_shape=jax.ShapeDtypeStruct((batch * lq, D_MODEL), BF16),
        compiler_params=_cparams(("parallel",)),
    )(q, kb, vb, lam, subln_w.reshape(1, -1))


def _hgrn_kernel(qh_ref, fh_ref, ih_ref, gh_ref, lb_ref, nw_ref, s0_ref, o_ref, s_out_ref,
                 st_s, qin_s, qmid_s, kmid_s, kend_s, dec_s, *, ct, c):
    t = pl.program_id(1)
    nt = pl.num_programs(1)
    nc = ct // c

    @pl.when(t == 0)
    def _():
        for h in range(N_HEADS):
            st_s[h] = s0_ref[0, h].T

    a = lb_ref[...]
    amax = jnp.max(a, axis=0, keepdims=True)
    e = jnp.exp(a - amax)
    lb = e[0:1] / jnp.sum(e, axis=0, keepdims=True)

    row = lax.broadcasted_iota(I32, (c, c), 0)
    col = lax.broadcasted_iota(I32, (c, c), 1)
    causal = col <= row
    tril = causal.astype(F32)

    for ci in range(nc):
        rows = slice(ci * c, (ci + 1) * c)
        f = lb + (1.0 - lb) * _sigmoid(fh_ref[rows, :])
        b = jnp.dot(tril, jnp.log(f), precision=lax.Precision.HIGHEST, preferred_element_type=F32)
        b_last = b[c - 1:c, :]
        b_mid = b[c // 2 - 1:c // 2, :]
        q = _silu(qh_ref[rows, :].astype(F32)) * (HEAD_W ** -0.5)
        kk = 1.0 - f
        e_dn = jnp.exp(b - b_mid)
        e_up = jnp.exp(b_mid - b)
        qm = q * e_dn
        km = kk * e_up
        qmid_s[rows, :] = qm.astype(BF16)
        kmid_s[rows, :] = km.astype(BF16)
        qin_s[rows, :] = (qm * jnp.exp(b_mid)).astype(BF16)
        kend_s[rows, :] = (km * jnp.exp(b_last - b_mid)).astype(BF16)
        dec_s[ci:ci + 1, :] = jnp.exp(b_last)

    nw = nw_ref[...]
    st = [st_s[h] for h in range(N_HEADS)]
    for ci in range(nc):
        rows = slice(ci * c, (ci + 1) * c)
        gate = _silu(gh_ref[rows, :].astype(F32))
        decay = dec_s[ci:ci + 1, :]
        for h in range(N_HEADS):
            sl = slice(h * HEAD_W, (h + 1) * HEAD_W)
            v = ih_ref[rows, sl]
            inter = lax.dot_general(qin_s[rows, sl], st[h].astype(BF16), (((1,), (1,)), ((), ())),
                                    preferred_element_type=F32)
            att = lax.dot_general(qmid_s[rows, sl], kmid_s[rows, sl], (((1,), (1,)), ((), ())),
                                  preferred_element_type=F32)
            att = jnp.where(causal, att, 0.0)
            o = inter + jnp.dot(att.astype(BF16), v, preferred_element_type=F32)
            upd = lax.dot_general(v, kend_s[rows, sl], (((0,), (0,)), ((), ())),
                                  preferred_element_type=F32)
            st[h] = decay[:, sl] * st[h] + upd
            o = o * lax.rsqrt(jnp.mean(o * o, axis=-1, keepdims=True) + EPS) * nw[:, sl]
            o_ref[rows, sl] = (o * gate[:, sl]).astype(o_ref.dtype)
    for h in range(N_HEADS):
        st_s[h] = st[h]

    @pl.when(t == nt - 1)
    def _():
        for h in range(N_HEADS):
            s_out_ref[0, h] = st_s[h].T


def _hgrn(qh, fh, ih, gh, hgrn_lb, norm_w, s0, batch, seq):
    ct = min(seq, 256)
    c = min(GLA_CHUNK, ct)
    nt = seq // ct
    kern = functools.partial(_hgrn_kernel, ct=ct, c=c)
    tok = pl.BlockSpec((ct, D_MODEL), lambda b, t: (b * nt + t, 0))
    st_spec = pl.BlockSpec((1, N_HEADS, HEAD_W, HEAD_W), lambda b, t: (b, 0, 0, 0))
    return pl.pallas_call(
        kern,
        grid=(batch, nt),
        in_specs=[tok, tok, tok, tok,
                  pl.BlockSpec((2, D_MODEL), lambda b, t: (0, 0)),
                  pl.BlockSpec((1, D_MODEL), lambda b, t: (0, 0)),
                  st_spec],
        out_specs=[tok, st_spec],
        out_shape=[jax.ShapeDtypeStruct((batch * seq, D_MODEL), BF16),
                   jax.ShapeDtypeStruct((batch, N_HEADS, HEAD_W, HEAD_W), F32)],
        scratch_shapes=[pltpu.VMEM((N_HEADS, HEAD_W, HEAD_W), F32)]
        + [pltpu.VMEM((ct, D_MODEL), BF16) for _ in range(4)]
        + [pltpu.VMEM((max(ct // c, 8), D_MODEL), F32)],
        compiler_params=_cparams(("parallel", "arbitrary")),
    )(qh, fh, ih, gh, hgrn_lb, jnp.tile(norm_w.reshape(1, -1), (1, N_HEADS)), s0)


def _route(h2, wrt_ref, brt_ref, carry_s, tm):
    logits = lax.dot_general(wrt_ref[...], h2, (((1,), (1,)), ((), ())),
                             precision=lax.Precision.HIGHEST, preferred_element_type=F32)
    scores = _sigmoid(logits)
    biased = scores + brt_ref[...]
    big = float(2 * N_EXPERTS)

    x3 = biased.reshape(N_GROUPS, GROUP_W, tm)
    i3 = lax.broadcasted_iota(I32, x3.shape, 1).astype(F32)
    m1 = jnp.max(x3, axis=1, keepdims=True)
    i1 = jnp.min(jnp.where(x3 == m1, i3, big), axis=1, keepdims=True)
    m2 = jnp.max(jnp.where(i3 == i1, NEG, x3), axis=1, keepdims=True)
    gs = (m1 + m2).reshape(N_GROUPS, tm)

    g_iota = lax.broadcasted_iota(I32, (N_GROUPS, tm), 0).astype(F32)
    rem = gs
    gsel = jnp.zeros((N_GROUPS, tm), F32)
    for _ in range(TOPK_GROUPS):
        gm = jnp.max(rem, axis=0, keepdims=True)
        first = jnp.min(jnp.where(rem == gm, g_iota, big), axis=0, keepdims=True)
        sel = g_iota == first
        gsel = jnp.where(sel, 1.0, gsel)
        rem = jnp.where(sel, NEG, rem)
    keep = jnp.broadcast_to(gsel.reshape(N_GROUPS, 1, tm), x3.shape) > 0.5
    masked = jnp.where(keep, x3, NEG).reshape(N_EXPERTS, tm)

    e_iota = lax.broadcasted_iota(I32, (N_EXPERTS, tm), 0).astype(F32)
    onehot = jnp.zeros((N_EXPERTS, tm), F32)
    idxs, ws = [], []
    for _ in range(TOP_K):
        mk = jnp.max(masked, axis=0, keepdims=True)
        ik = jnp.min(jnp.where(masked == mk, e_iota, big), axis=0, keepdims=True)
        selk = e_iota == ik
        ws.append(jnp.sum(jnp.where(selk, scores, 0.0), axis=0, keepdims=True))
        idxs.append(ik)
        masked = jnp.where(selk, NEG, masked)
        onehot = jnp.where(selk, 1.0, onehot)
    wsum = ws[0]
    for k in range(1, TOP_K):
        wsum = wsum + ws[k]

    r = lax.broadcasted_iota(I32, (tm, tm), 0)
    cidx = lax.broadcasted_iota(I32, (tm, tm), 1)
    before = (r < cidx).astype(BF16)
    cum = jnp.dot(onehot.astype(BF16), before, preferred_element_type=F32) + carry_s[...]
    carry_s[...] = carry_s[...] + jnp.sum(onehot, axis=1, keepdims=True)

    k_iota = lax.broadcasted_iota(I32, (TOP_K, tm), 0)
    e_out = jnp.zeros((TOP_K, tm), F32)
    w_out = jnp.zeros((TOP_K, tm), F32)
    p_out = jnp.zeros((TOP_K, tm), F32)
    for k in range(TOP_K):
        pk = jnp.sum(jnp.where(e_iota == idxs[k], cum, 0.0), axis=0, keepdims=True)
        e_out = jnp.where(k_iota == k, idxs[k], e_out)
        w_out = jnp.where(k_iota == k, ws[k] / wsum * ROUTED_SCALE, w_out)
        p_out = jnp.where(k_iota == k, pk, p_out)
    return e_out, w_out, p_out


def _post_kernel(oa_ref, ob_ref, ga_ref, gb_ref, x_ref, g1_ref, sc2_ref, sh2_ref, g2_ref,
                 wpa_ref, wpb_ref, wo_ref, n2_ref, wrt_ref, brt_ref, wsg_ref, wsu_ref, wsd_ref,
                 base_ref, hp_ref, eidx_ref, wts_ref, pos_ref, cnt_ref, carry_s, *, tm):
    i = pl.program_id(0)

    @pl.when(i == 0)
    def _():
        carry_s[...] = jnp.zeros(carry_s.shape, F32)

    pa = jnp.dot(oa_ref[...], wpa_ref[...], preferred_element_type=F32)
    pb = jnp.dot(ob_ref[...], wpb_ref[...], preferred_element_type=F32)
    u = _sigmoid(ga_ref[...].astype(F32)) * pa + _sigmoid(gb_ref[...].astype(F32)) * pb
    x1 = x_ref[...] + _mod_val(g1_ref) * jnp.dot(u.astype(BF16), wo_ref[...], preferred_element_type=F32)
    y = x1 * lax.rsqrt(jnp.mean(x1 * x1, axis=-1, keepdims=True) + EPS) * n2_ref[...]
    h2 = y * (1.0 + _mod_val(sc2_ref)) + _mod_val(sh2_ref)
    hp_ref[...] = _pack_halves(h2)

    hb = h2.astype(BF16)
    sg = jnp.dot(hb, wsg_ref[...], preferred_element_type=F32)
    su = jnp.dot(hb, wsu_ref[...], preferred_element_type=F32)
    shared = jnp.dot((_silu(sg) * su).astype(BF16), wsd_ref[...], preferred_element_type=F32)
    base_ref[...] = x1 + _mod_val(g2_ref) * shared

    e_out, w_out, p_out = _route(h2, wrt_ref, brt_ref, carry_s, tm)
    eidx_ref[...] = e_out.astype(I32)
    wts_ref[...] = w_out
    pos_ref[...] = p_out.astype(I32)
    cnt_ref[...] = carry_s[...]


def _post(oa, ob, ga, gb, x2, g1, sc2, sh2, g2, wpa, wpb, wo, n2, wr, br, wsg, wsu, wsd, seq, tm):
    t = x2.shape[0]
    tok = lambda: pl.BlockSpec((tm, D_MODEL), lambda i: (i, 0))
    mod = lambda: _mod_spec(g1, seq, tm)
    full = lambda a: pl.BlockSpec(a.shape, lambda i: (0,) * a.ndim)
    n2 = n2.reshape(1, -1)
    wrt = wr.T
    brt = br.reshape(-1, 1)
    k_out = lambda: pl.BlockSpec((TOP_K, tm), lambda i: (0, i))
    return pl.pallas_call(
        functools.partial(_post_kernel, tm=tm),
        grid=(t // tm,),
        in_specs=[tok(), tok(), tok(), tok(), tok(), mod(), mod(), mod(), mod(),
                  full(wpa), full(wpb), full(wo), full(n2), full(wrt), full(brt),
                  full(wsg), full(wsu), full(wsd)],
        out_specs=[tok(), pl.BlockSpec((tm, D_MODEL // 2), lambda i: (i, 0)),
                   k_out(), k_out(), k_out(),
                   pl.BlockSpec((N_EXPERTS, 1), lambda i: (0, 0))],
        out_shape=[jax.ShapeDtypeStruct((t, D_MODEL), F32),
                   jax.ShapeDtypeStruct((t, D_MODEL // 2), U32),
                   jax.ShapeDtypeStruct((TOP_K, t), I32),
                   jax.ShapeDtypeStruct((TOP_K, t), F32),
                   jax.ShapeDtypeStruct((TOP_K, t), I32),
                   jax.ShapeDtypeStruct((N_EXPERTS, 1), F32)],
        scratch_shapes=[pltpu.VMEM((N_EXPERTS, 1), F32)],
        compiler_params=_cparams(("arbitrary",)),
    )(oa, ob, ga, gb, x2, g1, sc2, sh2, g2, wpa, wpb, wo, n2, wrt, brt, wsg, wsu, wsd)


def _dest_row(pstart_ref, eidx_ref, pos_ref, k, t):
    return pstart_ref[eidx_ref[k, t]] + pos_ref[k, t]


def _dispatch_kernel(pstart_ref, eidx_ref, pos_ref, hp_ref, xs_in_ref, xs_ref, sem, *, tm):
    del xs_in_ref

    def row_copy(t, d):
        return pltpu.make_async_copy(hp_ref.at[pl.ds(t, 1), :], xs_ref.at[pl.ds(d, 1), :], sem)

    def issue(t, c):
        for k in range(TOP_K):
            row_copy(t, _dest_row(pstart_ref, eidx_ref, pos_ref, k, t)).start(priority=k % 2)
        return c

    def drain(t, c):
        for k in range(TOP_K):
            row_copy(0, 0).wait()
        return c

    lax.fori_loop(0, tm, issue, 0)
    lax.fori_loop(0, tm, drain, 0)


def _smem_full(a):
    return pl.BlockSpec(a.shape, lambda i: (0,) * a.ndim, memory_space=pltpu.SMEM)


def _dispatch(hp, pstart, eidx, pos, rows, tm):
    t = hp.shape[0]
    zeros = jnp.zeros((rows, D_MODEL // 2), U32)
    ksm = lambda: pl.BlockSpec((TOP_K, tm), lambda i: (0, i), memory_space=pltpu.SMEM)
    return pl.pallas_call(
        functools.partial(_dispatch_kernel, tm=tm),
        grid=(t // tm,),
        in_specs=[_smem_full(pstart), ksm(), ksm(),
                  pl.BlockSpec((tm, D_MODEL // 2), lambda i: (i, 0)),
                  pl.BlockSpec(memory_space=pl.ANY)],
        out_specs=pl.BlockSpec(memory_space=pl.ANY),
        out_shape=jax.ShapeDtypeStruct((rows, D_MODEL // 2), U32),
        scratch_shapes=[pltpu.SemaphoreType.DMA],
        input_output_aliases={4: 0},
        compiler_params=_cparams(("arbitrary",)),
    )(pstart, eidx, pos, hp, zeros)


def _expert_kernel(be_ref, nu_ref, xs_ref, wg_ref, wu_ref, wd_ref, ys_ref):
    i = pl.program_id(0)

    @pl.when(i < nu_ref[0])
    def _():
        x = _unpack_halves(xs_ref[...]).astype(BF16)
        g = jnp.dot(x, wg_ref[0].astype(BF16), preferred_element_type=F32)
        u = jnp.dot(x, wu_ref[0].astype(BF16), preferred_element_type=F32)
        hmid = (_silu(g) * u).astype(BF16)
        ys_ref[...] = _pack_halves(jnp.dot(hmid, wd_ref[0].astype(BF16), preferred_element_type=F32))

    @pl.when(i >= nu_ref[0])
    def _():
        ys_ref[...] = jnp.zeros(ys_ref.shape, ys_ref.dtype)


def _experts(xs, blk_e, n_used, w_eg, w_eu, w_ed):
    rows = xs.shape[0]
    nb = rows // ROW_BLOCK
    grid_spec = pltpu.PrefetchScalarGridSpec(
        num_scalar_prefetch=2,
        grid=(nb,),
        in_specs=[pl.BlockSpec((ROW_BLOCK, D_MODEL // 2), lambda i, be, nu: (i, 0)),
                  pl.BlockSpec((1, D_MODEL, D_EXPERT), lambda i, be, nu: (be[i], 0, 0)),
                  pl.BlockSpec((1, D_MODEL, D_EXPERT), lambda i, be, nu: (be[i], 0, 0)),
                  pl.BlockSpec((1, D_EXPERT, D_MODEL), lambda i, be, nu: (be[i], 0, 0))],
        out_specs=pl.BlockSpec((ROW_BLOCK, D_MODEL // 2), lambda i, be, nu: (i, 0)),
    )
    return pl.pallas_call(
        _expert_kernel,
        grid_spec=grid_spec,
        out_shape=jax.ShapeDtypeStruct((rows, D_MODEL // 2), U32),
        compiler_params=_cparams(("arbitrary",)),
    )(blk_e, n_used, xs, w_eg, w_eu, w_ed)


def _combine_kernel(pstart_ref, eidx_ref, pos_ref, base_ref, wts_ref, g2_ref, ys_ref, o_ref,
                    buf, sem, *, tm):
    def row_copy(d, k, t):
        return pltpu.make_async_copy(ys_ref.at[pl.ds(d, 1), :], buf.at[k, pl.ds(t, 1), :], sem)

    def issue(t, c):
        for k in range(TOP_K):
            row_copy(_dest_row(pstart_ref, eidx_ref, pos_ref, k, t), k, t).start(priority=k % 2)
        return c

    def drain(t, c):
        for k in range(TOP_K):
            row_copy(0, 0, 0).wait()
        return c

    lax.fori_loop(0, tm, issue, 0)
    lax.fori_loop(0, tm, drain, 0)

    w = wts_ref[...]
    y = jnp.zeros((tm, D_MODEL), F32)
    for k in range(TOP_K):
        y = y + w[:, k:k + 1] * _unpack_halves(buf[k])
    o_ref[...] = base_ref[...] + _mod_val(g2_ref) * y


def _combine(base, wts_t, g2, ys, pstart, eidx, pos, seq, tm):
    t = base.shape[0]
    ksm = lambda: pl.BlockSpec((TOP_K, tm), lambda i: (0, i), memory_space=pltpu.SMEM)
    return pl.pallas_call(
        functools.partial(_combine_kernel, tm=tm),
        grid=(t // tm,),
        in_specs=[_smem_full(pstart), ksm(), ksm(),
                  pl.BlockSpec((tm, D_MODEL), lambda i: (i, 0)),
                  pl.BlockSpec((tm, TOP_K), lambda i: (i, 0)),
                  _mod_spec(g2, seq, tm),
                  pl.BlockSpec(memory_space=pl.ANY)],
        out_specs=pl.BlockSpec((tm, D_MODEL), lambda i: (i, 0)),
        out_shape=jax.ShapeDtypeStruct((t, D_MODEL), F32),
        scratch_shapes=[pltpu.VMEM((TOP_K, tm, D_MODEL // 2), U32), pltpu.SemaphoreType.DMA],
        compiler_params=_cparams(("arbitrary",)),
    )(pstart, eidx, pos, base, wts_t, g2, ys)


def _layer(x, mods, past_k, past_v, s0, p):
    batch, seq, _ = x.shape
    t = batch * seq
    x2 = x.reshape(t, D_MODEL)
    if seq % 256 == 0:
        tile = lambda cap: min(seq, cap)
    else:
        mods = [jnp.broadcast_to(m, (batch, seq, D_MODEL)).reshape(t, D_MODEL) for m in mods]
        tile = lambda cap: min(t, cap)
    shift1, scale1, gate1, shift2, scale2, gate2 = mods

    h = _norm_mod(x2, p["norm1_w"], scale1, shift1, seq, tile(512))
    qn = jnp.tile(p["q_norm_w"].reshape(1, -1), (1, D_MODEL // HEAD_DIM_A))
    kn = jnp.tile(p["k_norm_w"].reshape(1, -1), (1, D_MODEL // HEAD_DIM_A))
    w_in = p["w_in"]
    (q,) = _proj(h, w_in, 0, [BF16], qn, out_scale=(HEAD_DIM_A ** -0.5 * LOG2E,))
    k, kb = _proj(h, w_in, 1, [F32, BF16], kn)
    v, vb = _proj(h, w_in, 2, [F32, BF16])
    (qh,) = _proj(h, w_in, 3, [BF16])
    (fh,) = _proj(h, w_in, 4, [F32])
    (ih,) = _proj(h, w_in, 5, [BF16])
    (gh,) = _proj(h, w_in, 6, [BF16])
    (ga,) = _proj(h, w_in, 7, [BF16])
    (gb,) = _proj(h, w_in, 8, [BF16])

    if past_k is None:
        o_a = _attention_prompt(q, kb, vb, p["lam"], p["subln_w"], batch, seq)
    else:
        past = past_k.shape[1]
        lk = past + seq
        kb = jnp.concatenate([past_k.reshape(batch, past, D_MODEL).astype(BF16),
                              kb.reshape(batch, seq, D_MODEL)], axis=1).reshape(batch * lk, D_MODEL)
        vb = jnp.concatenate([past_v.reshape(batch, past, D_MODEL).astype(BF16),
                              vb.reshape(batch, seq, D_MODEL)], axis=1).reshape(batch * lk, D_MODEL)
        o_a = _attention_cached(q, kb, vb, p["lam"], p["subln_w"], batch, seq, lk)
    o_b, s_new = _hgrn(qh, fh, ih, gh, p["hgrn_lb"], p["hgrn_norm_w"], s0, batch, seq)

    tm = tile(256)
    base, hp, eidx, wts, pos, counts = _post(
        o_a, o_b, ga, gb, x2, gate1, scale2, shift2, gate2,
        p["w_proj_a"], p["w_proj_b"], p["w_out"], p["norm2_w"], p["w_router"], p["b_router"],
        p["w_sh_gate"], p["w_sh_up"], p["w_sh_down"], seq, tm)

    counts = counts.reshape(N_EXPERTS).astype(I32)
    pcounts = (counts + ROW_BLOCK - 1) // ROW_BLOCK * ROW_BLOCK
    pend = jnp.cumsum(pcounts)
    pstart = (pend - pcounts).astype(I32)
    nb = -(-(t * TOP_K + N_EXPERTS * (ROW_BLOCK - 1)) // ROW_BLOCK)
    rows = nb * ROW_BLOCK
    blk_e = jnp.minimum(jnp.searchsorted(pend, jnp.arange(nb, dtype=I32) * ROW_BLOCK, side="right"),
                        N_EXPERTS - 1).astype(I32)
    n_used = (pend[-1] // ROW_BLOCK).astype(I32).reshape(1)

    tmd = tile(512)
    xs = _dispatch(hp, pstart, eidx, pos, rows, tmd)
    ys = _experts(xs, blk_e, n_used, p["w_exp_gate"], p["w_exp_up"], p["w_exp_down"])
    y = _combine(base, wts.T, gate2, ys, pstart, eidx, pos, seq, tmd)
    return y.reshape(batch, seq, D_MODEL), k, v, s_new


def kernel(x_prompt, x_sample, cache_attn_k, cache_attn_v, state_hgrn, c_prompt, c_sample, w_ada, b_ada, norm1_w, norm2_w, w_in, q_norm_w, k_norm_w, lambda_q1, lambda_k1, lambda_q2, lambda_k2, subln_w, hgrn_lb, hgrn_norm_w, w_proj_a, w_proj_b, w_out, w_router, b_router, w_exp_gate, w_exp_up, w_exp_down, w_sh_gate, w_sh_up, w_sh_down):
    bp, lp, _ = x_prompt.shape
    bs, ls, _ = x_sample.shape
    l = 0
    lam = (jnp.exp(jnp.sum(lambda_q1[l] * lambda_k1[l])) - jnp.exp(jnp.sum(lambda_q2[l] * lambda_k2[l]))
           + LAMBDA_INIT).astype(F32).reshape(1, 1)
    p = dict(
        norm1_w=norm1_w[l], norm2_w=norm2_w[l], w_in=w_in[l].astype(BF16),
        q_norm_w=q_norm_w[l], k_norm_w=k_norm_w[l], lam=lam, subln_w=subln_w[l],
        hgrn_lb=hgrn_lb, hgrn_norm_w=hgrn_norm_w[l],
        w_proj_a=w_proj_a[l].astype(BF16), w_proj_b=w_proj_b[l].astype(BF16), w_out=w_out[l].astype(BF16),
        w_router=w_router[l], b_router=b_router[l],
        w_exp_gate=w_exp_gate[l], w_exp_up=w_exp_up[l], w_exp_down=w_exp_down[l],
        w_sh_gate=w_sh_gate[l].astype(BF16), w_sh_up=w_sh_up[l].astype(BF16), w_sh_down=w_sh_down[l].astype(BF16),
    )
    mod = _ada(jnp.concatenate([c_prompt, c_sample], axis=0), w_ada[l], b_ada[l])
    mod = mod.reshape(bp + bs, 6, 1, D_MODEL)
    mods_p = [mod[:bp, j] for j in range(6)]
    mods_s = [mod[bp:, j] for j in range(6)]

    zero_state = jnp.zeros((bp, N_HEADS, HEAD_W, HEAD_W), F32)
    yp, kp, vp, sp = _layer(x_prompt, mods_p, None, None, zero_state, p)
    ys, ks, vs, ss = _layer(x_sample, mods_s, cache_attn_k[l], cache_attn_v[l], state_hgrn[l], p)

    return (yp, ys,
            kp.reshape(1, bp, lp, N_HEADS, 2, HEAD_DIM_A), vp.reshape(1, bp, lp, N_HEADS, HEAD_W), sp[None],
            ks.reshape(1, bs, ls, N_HEADS, 2, HEAD_DIM_A), vs.reshape(1, bs, ls, N_HEADS, HEAD_W), ss[None])
```

```python
import functools
import math

import jax
import jax.numpy as jnp
from jax import lax
from jax.experimental import pallas as pl
from jax.experimental.pallas import tpu as pltpu

F32 = jnp.float32
BF16 = jnp.bfloat16
U32 = jnp.uint32
I32 = jnp.int32

D_MODEL = 1024
N_HEADS = 8
HEAD_W = 128
HEAD_DIM_A = 64
CHUNK = 64
N_EXPERTS = 256
TOP_K = 8
N_GROUPS = 8
GROUP_W = N_EXPERTS // N_GROUPS
TOPK_GROUPS = 4
D_EXPERT = 256
ROUTED_SCALE = 2.5
EPS = 1e-6
LAMBDA_INIT = 0.8 - 0.6 * math.exp(-0.3 * 0)
LANES = 128
NEG = -1e30
LOG2E = math.log2(math.e)

GLA_CHUNK = 32
ROW_BLOCK = 256
VMEM_LIMIT = 48 * 1024 * 1024


def _sigmoid(x):
    return 1.0 / (1.0 + jnp.exp(-x))


def _silu(x):
    return x * _sigmoid(x)


def _cparams(sem):
    return pltpu.CompilerParams(dimension_semantics=sem, vmem_limit_bytes=VMEM_LIMIT)


ROW_PARTS = 4


def _store_packed(ref, x):
    m, half = x.shape[0], x.shape[1] // 2
    lo = pltpu.bitcast(x[:, :half].astype(BF16).astype(F32), U32)
    hi = pltpu.bitcast(x[:, half:].astype(BF16).astype(F32), U32)
    w = (hi & jnp.uint32(0xFFFF0000)) | (lo >> 16)
    for j in range(ROW_PARTS):
        ref[pl.ds(j, m, stride=ROW_PARTS), :] = w[:, j * LANES:(j + 1) * LANES]


def _load_packed(ref, m):
    parts = [ref[pl.ds(j, m, stride=ROW_PARTS), :] for j in range(ROW_PARTS)]
    lo = [pltpu.bitcast(w << 16, F32) for w in parts]
    hi = [pltpu.bitcast(w & jnp.uint32(0xFFFF0000), F32) for w in parts]
    return jnp.concatenate(lo + hi, axis=1)


def _mod_spec(a, seq, tm):
    if a.ndim == 3:
        per_b = seq // tm
        return pl.BlockSpec((1, 1, D_MODEL), lambda i: (i // per_b, 0, 0))
    return pl.BlockSpec((tm, D_MODEL), lambda i: (i, 0))


def _mod_val(ref):
    v = ref[...]
    return v.reshape(v.shape[-2], v.shape[-1])


def _ada_kernel(c_ref, w_ref, b_ref, o_ref):
    c = _silu(c_ref[...])
    o_ref[...] = jnp.dot(c, w_ref[...], precision=lax.Precision.HIGHEST,
                         preferred_element_type=F32) + b_ref[...]


def _ada(c, w_ada, b_ada):
    n = c.shape[0]
    nj = w_ada.shape[1] // D_MODEL
    return pl.pallas_call(
        _ada_kernel,
        grid=(nj,),
        in_specs=[pl.BlockSpec((n, D_MODEL), lambda j: (0, 0)),
                  pl.BlockSpec((D_MODEL, D_MODEL), lambda j: (0, j)),
                  pl.BlockSpec((1, D_MODEL), lambda j: (0, j))],
        out_specs=pl.BlockSpec((n, D_MODEL), lambda j: (0, j)),
        out_shape=jax.ShapeDtypeStruct((n, w_ada.shape[1]), F32),
        compiler_params=_cparams(("arbitrary",)),
    )(c, w_ada, b_ada.reshape(1, -1))


def _norm_mod_kernel(x_ref, w_ref, sc_ref, sh_ref, o_ref):
    x = x_ref[...]
    y = x * lax.rsqrt(jnp.mean(x * x, axis=-1, keepdims=True) + EPS) * w_ref[...]
    o_ref[...] = (y * (1.0 + _mod_val(sc_ref)) + _mod_val(sh_ref)).astype(o_ref.dtype)


def _norm_mod(x2, w, scale, shift, seq, tm):
    t = x2.shape[0]
    return pl.pallas_call(
        _norm_mod_kernel,
        grid=(t // tm,),
        in_specs=[pl.BlockSpec((tm, D_MODEL), lambda i: (i, 0)),
                  pl.BlockSpec((1, D_MODEL), lambda i: (0, 0)),
                  _mod_spec(scale, seq, tm), _mod_spec(shift, seq, tm)],
        out_specs=pl.BlockSpec((tm, D_MODEL), lambda i: (i, 0)),
        out_shape=jax.ShapeDtypeStruct((t, D_MODEL), BF16),
        compiler_params=_cparams(("parallel",)),
    )(x2, w.reshape(1, -1), scale, shift)


def _head_norm(acc, w):
    r = lax.broadcasted_iota(I32, (HEAD_W, HEAD_W), 0) // HEAD_DIM_A
    c = lax.broadcasted_iota(I32, (HEAD_W, HEAD_W), 1) // HEAD_DIM_A
    ones = (r == c).astype(BF16)
    outs = []
    for h in range(N_HEADS):
        a = acc[:, h * HEAD_W:(h + 1) * HEAD_W]
        sq = a * a
        hi = sq.astype(BF16)
        lo = (sq - hi.astype(F32)).astype(BF16)
        ss = (jnp.dot(hi, ones, preferred_element_type=F32)
              + jnp.dot(lo, ones, preferred_element_type=F32))
        outs.append(a * lax.rsqrt(ss * (1.0 / HEAD_DIM_A) + EPS) * w[:, h * HEAD_W:(h + 1) * HEAD_W])
    return jnp.concatenate(outs, axis=1)


def _proj_kernel(h_ref, w_ref, nw_ref, *o_refs, head_norm, out_scale):
    acc = jnp.dot(h_ref[...], w_ref[...], preferred_element_type=F32)
    if head_norm:
        acc = _head_norm(acc, nw_ref[...])
    for o_ref, s in zip(o_refs, out_scale):
        o_ref[...] = (acc if s == 1.0 else acc * s).astype(o_ref.dtype)


def _proj(h, w_in_bf, group, out_dtypes, norm_w=None, out_scale=None, tm=1024):
    t = h.shape[0]
    tm = min(tm, t)
    nw = jnp.ones((1, D_MODEL), F32) if norm_w is None else norm_w
    out_scale = tuple(out_scale or (1.0,) * len(out_dtypes))
    outs = pl.pallas_call(
        functools.partial(_proj_kernel, head_norm=norm_w is not None, out_scale=out_scale),
        grid=(t // tm,),
        in_specs=[pl.BlockSpec((tm, D_MODEL), lambda i: (i, 0)),
                  pl.BlockSpec((D_MODEL, D_MODEL), lambda i: (0, group)),
                  pl.BlockSpec((1, D_MODEL), lambda i: (0, 0))],
        out_specs=[pl.BlockSpec((tm, D_MODEL), lambda i: (i, 0)) for _ in out_dtypes],
        out_shape=[jax.ShapeDtypeStruct((t, D_MODEL), dt) for dt in out_dtypes],
        compiler_params=_cparams(("parallel",)),
    )(h, w_in_bf, nw)
    return outs


def _split_maps(q):
    lane = lax.broadcasted_iota(I32, q.shape, 1)
    zero = jnp.zeros_like(q)
    return jnp.where(lane < HEAD_DIM_A, q, zero), jnp.where(lane >= HEAD_DIM_A, q, zero)


def _finish_head(acc1, l1, acc2, l2, lam, sw):
    o = acc1 / l1 - lam * (acc2 / l2)
    o = o * lax.rsqrt(jnp.mean(o * o, axis=-1, keepdims=True) + EPS) * sw
    return o * (1.0 - LAMBDA_INIT)


def _attn_kernel(q_ref, k_ref, v_ref, lam_ref, sw_ref, o_ref, mx_s, mrep_s, ls_s, acc_s,
                 *, tq, tk, hp):
    qi = pl.program_id(2)
    nf = tk // LANES
    qs = []
    for hh in range(hp):
        qs.extend(_split_maps(q_ref[:, hh * HEAD_W:(hh + 1) * HEAD_W]))
    ns = 2 * hp

    r_chunk = lax.broadcasted_iota(I32, (tq, tk), 0) // CHUNK
    c_chunk = lax.broadcasted_iota(I32, (tq, tk), 1) // CHUNK
    diag_ok = c_chunk <= r_chunk

    def scores(kt, s, masked):
        k = k_ref[pl.ds(pl.multiple_of(kt * tk, tk), tk), (s // 2) * HEAD_W:(s // 2 + 1) * HEAD_W]
        sc = lax.dot_general(qs[s], k, (((1,), (1,)), ((), ())), preferred_element_type=F32)
        return jnp.where(diag_ok, sc, NEG) if masked else sc

    def fold(x, op):
        r = x[:, :LANES]
        for j in range(1, nf):
            r = op(r, x[:, j * LANES:(j + 1) * LANES])
        return r

    def pass1(kt, first):
        for s in range(ns):
            m = fold(scores(kt, s, first), jnp.maximum)
            mx_s[s] = m if first else jnp.maximum(mx_s[s], m)

    def pass2(kt, first):
        for s in range(ns):
            m = mrep_s[s]
            p = jnp.exp2(scores(kt, s, first) - jnp.concatenate([m] * nf, axis=1))
            v = v_ref[pl.ds(pl.multiple_of(kt * tk, tk), tk), (s // 2) * HEAD_W:(s // 2 + 1) * HEAD_W]
            pv = jnp.dot(p.astype(BF16), v, preferred_element_type=F32)
            ls_s[s] = fold(p, jnp.add) if first else ls_s[s] + fold(p, jnp.add)
            acc_s[s] = pv if first else acc_s[s] + pv

    def loop(fn):
        def body(kt, c):
            fn(kt, False)
            return c

        fn(qi, True)
        lax.fori_loop(0, qi, body, 0)

    loop(pass1)
    for s in range(ns):
        mrep_s[s] = jnp.broadcast_to(jnp.max(mx_s[s], axis=-1, keepdims=True), (tq, LANES))
    loop(pass2)

    lam = lam_ref[0, 0]
    for hh in range(hp):
        l1 = jnp.sum(ls_s[2 * hh], axis=-1, keepdims=True)
        l2 = jnp.sum(ls_s[2 * hh + 1], axis=-1, keepdims=True)
        o = _finish_head(acc_s[2 * hh], l1, acc_s[2 * hh + 1], l2, lam, sw_ref[...])
        o_ref[:, hh * HEAD_W:(hh + 1) * HEAD_W] = o.astype(o_ref.dtype)


def _attention_prompt(q, kb, vb, lam, subln_w, batch, seq):
    tq = tk = min(seq, 256)
    hp = 8
    nq = seq // tq
    kern = functools.partial(_attn_kernel, tq=tq, tk=tk, hp=hp)
    w = hp * HEAD_W
    return pl.pallas_call(
        kern,
        grid=(batch, N_HEADS // hp, nq),
        in_specs=[pl.BlockSpec((tq, w), lambda b, h, i: (b * nq + i, h)),
                  pl.BlockSpec((seq, w), lambda b, h, i: (b, h)),
                  pl.BlockSpec((seq, w), lambda b, h, i: (b, h)),
                  pl.BlockSpec((1, 1), lambda b, h, i: (0, 0), memory_space=pltpu.SMEM),
                  pl.BlockSpec((1, HEAD_W), lambda b, h, i: (0, 0))],
        out_specs=pl.BlockSpec((tq, w), lambda b, h, i: (b * nq + i, h)),
        out_shape=jax.ShapeDtypeStruct((batch * seq, D_MODEL), BF16),
        scratch_shapes=[pltpu.VMEM((2 * hp, tq, LANES), F32) for _ in range(4)],
        compiler_params=_cparams(("parallel", "parallel", "arbitrary")),
    )(q, kb, vb, lam, subln_w.reshape(1, -1))


def _attn_cached_kernel(q_ref, k_ref, v_ref, lam_ref, sw_ref, o_ref, *, lq, lk):
    qpos = (lk - lq) + lax.broadcasted_iota(I32, (lq, lk), 0)
    kpos = lax.broadcasted_iota(I32, (lq, lk), 1)
    allowed = (kpos // CHUNK) <= (qpos // CHUNK)
    lam = lam_ref[0, 0]
    for h in range(N_HEADS):
        sl = slice(h * HEAD_W, (h + 1) * HEAD_W)
        k = k_ref[:, sl]
        v = v_ref[:, sl]
        accs, ls = [], []
        for qm in _split_maps(q_ref[:, sl]):
            sc = lax.dot_general(qm, k, (((1,), (1,)), ((), ())), preferred_element_type=F32)
            sc = jnp.where(allowed, sc, NEG)
            p = jnp.exp2(sc - jnp.max(sc, axis=-1, keepdims=True))
            ls.append(jnp.sum(p, axis=-1, keepdims=True))
            accs.append(jnp.dot(p.astype(BF16), v, preferred_element_type=F32))
        o = _finish_head(accs[0], ls[0], accs[1], ls[1], lam, sw_ref[...])
        o_ref[:, sl] = o.astype(o_ref.dtype)


def _attention_cached(q, kb, vb, lam, subln_w, batch, lq, lk):
    return pl.pallas_call(
        functools.partial(_attn_cached_kernel, lq=lq, lk=lk),
        grid=(batch,),
        in_specs=[pl.BlockSpec((lq, D_MODEL), lambda b: (b, 0)),
                  pl.BlockSpec((lk, D_MODEL), lambda b: (b, 0)),
                  pl.BlockSpec((lk, D_MODEL), lambda b: (b, 0)),
                  pl.BlockSpec((1, 1), lambda b: (0, 0), memory_space=pltpu.SMEM),
                  pl.BlockSpec((1, HEAD_W), lambda b: (0, 0))],
        out_specs=pl.BlockSpec((lq, D_MODEL), lambda b: (b, 0)),
        out_shape=jax.ShapeDtypeStruct((batch * lq, D_MODEL), BF16),
        compiler_params=_cparams(("parallel",)),
    )(q, kb, vb, lam, subln_w.reshape(1, -1))


def _hgrn_kernel(qh_ref, fh_ref, ih_ref, gh_ref, lb_ref, nw_ref, s0_ref, o_ref, s_out_ref,
                 st_s, qin_s, qmid_s, kmid_s, kend_s, dec_s, *, ct, c):
    t = pl.program_id(1)
    nt = pl.num_programs(1)
    nc = ct // c

    @pl.when(t == 0)
    def _():
        for h in range(N_HEADS):
            st_s[h] = s0_ref[0, h].T

    a = lb_ref[...]
    amax = jnp.max(a, axis=0, keepdims=True)
    e = jnp.exp(a - amax)
    lb = e[0:1] / jnp.sum(e, axis=0, keepdims=True)

    row = lax.broadcasted_iota(I32, (c, c), 0)
    col = lax.broadcasted_iota(I32, (c, c), 1)
    causal = col <= row
    tril = causal.astype(F32)

    for ci in range(nc):
        rows = slice(ci * c, (ci + 1) * c)
        f = lb + (1.0 - lb) * _sigmoid(fh_ref[rows, :])
        b = jnp.dot(tril, jnp.log(f), precision=lax.Precision.HIGHEST, preferred_element_type=F32)
        b_last = b[c - 1:c, :]
        b_mid = b[c // 2 - 1:c // 2, :]
        q = _silu(qh_ref[rows, :].astype(F32)) * (HEAD_W ** -0.5)
        kk = 1.0 - f
        e_dn = jnp.exp(b - b_mid)
        e_up = jnp.exp(b_mid - b)
        qm = q * e_dn
        km = kk * e_up
        qmid_s[rows, :] = qm.astype(BF16)
        kmid_s[rows, :] = km.astype(BF16)
        qin_s[rows, :] = (qm * jnp.exp(b_mid)).astype(BF16)
        kend_s[rows, :] = (km * jnp.exp(b_last - b_mid)).astype(BF16)
        dec_s[ci:ci + 1, :] = jnp.exp(b_last)

    nw = nw_ref[...]
    st = [st_s[h] for h in range(N_HEADS)]
    for ci in range(nc):
        rows = slice(ci * c, (ci + 1) * c)
        gate = _silu(gh_ref[rows, :].astype(F32))
        decay = dec_s[ci:ci + 1, :]
        for h in range(N_HEADS):
            sl = slice(h * HEAD_W, (h + 1) * HEAD_W)
            v = ih_ref[rows, sl]
            inter = lax.dot_general(qin_s[rows, sl], st[h].astype(BF16), (((1,), (1,)), ((), ())),
                                    preferred_element_type=F32)
            att = lax.dot_general(qmid_s[rows, sl], kmid_s[rows, sl], (((1,), (1,)), ((), ())),
                                  preferred_element_type=F32)
            att = jnp.where(causal, att, 0.0)
            o = inter + jnp.dot(att.astype(BF16), v, preferred_element_type=F32)
            upd = lax.dot_general(v, kend_s[rows, sl], (((0,), (0,)), ((), ())),
                                  preferred_element_type=F32)
            st[h] = decay[:, sl] * st[h] + upd
            o = o * lax.rsqrt(jnp.mean(o * o, axis=-1, keepdims=True) + EPS) * nw[:, sl]
            o_ref[rows, sl] = (o * gate[:, sl]).astype(o_ref.dtype)
    for h in range(N_HEADS):
        st_s[h] = st[h]

    @pl.when(t == nt - 1)
    def _():
        for h in range(N_HEADS):
            s_out_ref[0, h] = st_s[h].T


def _hgrn(qh, fh, ih, gh, hgrn_lb, norm_w, s0, batch, seq):
    ct = min(seq, 256)
    c = min(GLA_CHUNK, ct)
    nt = seq // ct
    kern = functools.partial(_hgrn_kernel, ct=ct, c=c)
    tok = pl.BlockSpec((ct, D_MODEL), lambda b, t: (b * nt + t, 0))
    st_spec = pl.BlockSpec((1, N_HEADS, HEAD_W, HEAD_W), lambda b, t: (b, 0, 0, 0))
    return pl.pallas_call(
        kern,
        grid=(batch, nt),
        in_specs=[tok, tok, tok, tok,
                  pl.BlockSpec((2, D_MODEL), lambda b, t: (0, 0)),
                  pl.BlockSpec((1, D_MODEL), lambda b, t: (0, 0)),
                  st_spec],
        out_specs=[tok, st_spec],
        out_shape=[jax.ShapeDtypeStruct((batch * seq, D_MODEL), BF16),
                   jax.ShapeDtypeStruct((batch, N_HEADS, HEAD_W, HEAD_W), F32)],
        scratch_shapes=[pltpu.VMEM((N_HEADS, HEAD_W, HEAD_W), F32)]
        + [pltpu.VMEM((ct, D_MODEL), BF16) for _ in range(4)]
        + [pltpu.VMEM((max(ct // c, 8), D_MODEL), F32)],
        compiler_params=_cparams(("parallel", "arbitrary")),
    )(qh, fh, ih, gh, hgrn_lb, jnp.tile(norm_w.reshape(1, -1), (1, N_HEADS)), s0)


def _route(h2, wrt_ref, brt_ref, carry_s, tm):
    logits = lax.dot_general(wrt_ref[...], h2, (((1,), (1,)), ((), ())),
                             precision=lax.Precision.HIGHEST, preferred_element_type=F32)
    scores = _sigmoid(logits)
    biased = scores + brt_ref[...]
    big = float(2 * N_EXPERTS)

    x3 = biased.reshape(N_GROUPS, GROUP_W, tm)
    i3 = lax.broadcasted_iota(I32, x3.shape, 1).astype(F32)
    m1 = jnp.max(x3, axis=1, keepdims=True)
    i1 = jnp.min(jnp.where(x3 == m1, i3, big), axis=1, keepdims=True)
    m2 = jnp.max(jnp.where(i3 == i1, NEG, x3), axis=1, keepdims=True)
    gs = (m1 + m2).reshape(N_GROUPS, tm)

    g_iota = lax.broadcasted_iota(I32, (N_GROUPS, tm), 0).astype(F32)
    rem = gs
    gsel = jnp.zeros((N_GROUPS, tm), F32)
    for _ in range(TOPK_GROUPS):
        gm = jnp.max(rem, axis=0, keepdims=True)
        first = jnp.min(jnp.where(rem == gm, g_iota, big), axis=0, keepdims=True)
        sel = g_iota == first
        gsel = jnp.where(sel, 1.0, gsel)
        rem = jnp.where(sel, NEG, rem)
    keep = jnp.broadcast_to(gsel.reshape(N_GROUPS, 1, tm), x3.shape) > 0.5
    masked = jnp.where(keep, x3, NEG).reshape(N_EXPERTS, tm)

    e_iota = lax.broadcasted_iota(I32, (N_EXPERTS, tm), 0).astype(F32)
    onehot = jnp.zeros((N_EXPERTS, tm), F32)
    idxs, ws = [], []
    for _ in range(TOP_K):
        mk = jnp.max(masked, axis=0, keepdims=True)
        ik = jnp.min(jnp.where(masked == mk, e_iota, big), axis=0, keepdims=True)
        selk = e_iota == ik
        ws.append(jnp.sum(jnp.where(selk, scores, 0.0), axis=0, keepdims=True))
        idxs.append(ik)
        masked = jnp.where(selk, NEG, masked)
        onehot = jnp.where(selk, 1.0, onehot)
    wsum = ws[0]
    for k in range(1, TOP_K):
        wsum = wsum + ws[k]

    r = lax.broadcasted_iota(I32, (tm, tm), 0)
    cidx = lax.broadcasted_iota(I32, (tm, tm), 1)
    before = (r < cidx).astype(BF16)
    cum = jnp.dot(onehot.astype(BF16), before, preferred_element_type=F32) + carry_s[...]
    carry_s[...] = carry_s[...] + jnp.sum(onehot, axis=1, keepdims=True)

    k_iota = lax.broadcasted_iota(I32, (TOP_K, tm), 0)
    e_out = jnp.zeros((TOP_K, tm), F32)
    w_out = jnp.zeros((TOP_K, tm), F32)
    p_out = jnp.zeros((TOP_K, tm), F32)
    for k in range(TOP_K):
        pk = jnp.sum(jnp.where(e_iota == idxs[k], cum, 0.0), axis=0, keepdims=True)
        e_out = jnp.where(k_iota == k, idxs[k], e_out)
        w_out = jnp.where(k_iota == k, ws[k] / wsum * ROUTED_SCALE, w_out)
        p_out = jnp.where(k_iota == k, pk, p_out)
    return e_out, w_out, p_out


def _post_kernel(oa_ref, ob_ref, ga_ref, gb_ref, x_ref, g1_ref, sc2_ref, sh2_ref, g2_ref,
                 wpa_ref, wpb_ref, wo_ref, n2_ref, wrt_ref, brt_ref, wsg_ref, wsu_ref, wsd_ref, c0_ref,
                 base_ref, hp_ref, eidx_ref, wts_ref, pos_ref, cnt_ref, carry_s, *, tm):
    i = pl.program_id(0)

    @pl.when(i == 0)
    def _():
        carry_s[...] = c0_ref[...]

    pa = jnp.dot(oa_ref[...], wpa_ref[...], preferred_element_type=F32)
    pb = jnp.dot(ob_ref[...], wpb_ref[...], preferred_element_type=F32)
    u = _sigmoid(ga_ref[...].astype(F32)) * pa + _sigmoid(gb_ref[...].astype(F32)) * pb
    x1 = x_ref[...] + _mod_val(g1_ref) * jnp.dot(u.astype(BF16), wo_ref[...], preferred_element_type=F32)
    y = x1 * lax.rsqrt(jnp.mean(x1 * x1, axis=-1, keepdims=True) + EPS) * n2_ref[...]
    h2 = y * (1.0 + _mod_val(sc2_ref)) + _mod_val(sh2_ref)
    _store_packed(hp_ref, h2)

    hb = h2.astype(BF16)
    sg = jnp.dot(hb, wsg_ref[...], preferred_element_type=F32)
    su = jnp.dot(hb, wsu_ref[...], preferred_element_type=F32)
    shared = jnp.dot((_silu(sg) * su).astype(BF16), wsd_ref[...], preferred_element_type=F32)
    base_ref[...] = x1 + _mod_val(g2_ref) * shared

    e_out, w_out, p_out = _route(h2, wrt_ref, brt_ref, carry_s, tm)
    eidx_ref[...] = e_out.astype(I32)
    wts_ref[...] = w_out
    pos_ref[...] = p_out.astype(I32)
    cnt_ref[...] = carry_s[...]


def _post(oa, ob, ga, gb, x2, g1, sc2, sh2, g2, wpa, wpb, wo, n2, wr, br, wsg, wsu, wsd, count0, seq, tm):
    t = x2.shape[0]
    tok = lambda: pl.BlockSpec((tm, D_MODEL), lambda i: (i, 0))
    mod = lambda: _mod_spec(g1, seq, tm)
    full = lambda a: pl.BlockSpec(a.shape, lambda i: (0,) * a.ndim)
    n2 = n2.reshape(1, -1)
    wrt = wr.T
    brt = br.reshape(-1, 1)
    k_out = lambda: pl.BlockSpec((TOP_K, tm), lambda i: (0, i))
    return pl.pallas_call(
        functools.partial(_post_kernel, tm=tm),
        grid=(t // tm,),
        in_specs=[tok(), tok(), tok(), tok(), tok(), mod(), mod(), mod(), mod(),
                  full(wpa), full(wpb), full(wo), full(n2), full(wrt), full(brt),
                  full(wsg), full(wsu), full(wsd), full(count0)],
        out_specs=[tok(), pl.BlockSpec((ROW_PARTS * tm, LANES), lambda i: (i, 0)),
                   k_out(), k_out(), k_out(),
                   pl.BlockSpec((N_EXPERTS, 1), lambda i: (0, 0))],
        out_shape=[jax.ShapeDtypeStruct((t, D_MODEL), F32),
                   jax.ShapeDtypeStruct((ROW_PARTS * t, LANES), U32),
                   jax.ShapeDtypeStruct((TOP_K, t), I32),
                   jax.ShapeDtypeStruct((TOP_K, t), F32),
                   jax.ShapeDtypeStruct((TOP_K, t), I32),
                   jax.ShapeDtypeStruct((N_EXPERTS, 1), F32)],
        scratch_shapes=[pltpu.VMEM((N_EXPERTS, 1), F32)],
        compiler_params=_cparams(("arbitrary",)),
    )(oa, ob, ga, gb, x2, g1, sc2, sh2, g2, wpa, wpb, wo, n2, wrt, brt, wsg, wsu, wsd, count0)


def _dest_kernel(eidx_ref, pos_ref, pstart_ref, o_ref, *, tm):
    e_iota = lax.broadcasted_iota(I32, (N_EXPERTS, tm), 0)
    k_iota = lax.broadcasted_iota(I32, (TOP_K, tm), 0)
    eidx = eidx_ref[...]
    start = jnp.zeros((TOP_K, tm), F32)
    for k in range(TOP_K):
        sel = e_iota == eidx[k:k + 1, :]
        sk = jnp.sum(jnp.where(sel, pstart_ref[...], 0.0), axis=0, keepdims=True)
        start = jnp.where(k_iota == k, sk, start)
    o_ref[...] = (start.astype(I32) + pos_ref[...]) * ROW_PARTS


def _dest(eidx, pos, pstart_col):
    t = eidx.shape[1]
    tm = min(t, 1024)
    blk = lambda: pl.BlockSpec((TOP_K, tm), lambda i: (0, i))
    return pl.pallas_call(
        functools.partial(_dest_kernel, tm=tm),
        grid=(t // tm,),
        in_specs=[blk(), blk(), pl.BlockSpec((N_EXPERTS, 1), lambda i: (0, 0))],
        out_specs=blk(),
        out_shape=jax.ShapeDtypeStruct((TOP_K, t), I32),
        compiler_params=_cparams(("parallel",)),
    )(eidx, pos, pstart_col)


def _dispatch_kernel(dest_ref, hp_ref, xs_in_ref, xs_ref, sem, *, tm):
    del xs_in_ref

    def row_copy(t, d):
        return pltpu.make_async_copy(
            hp_ref.at[pl.ds(pl.multiple_of(t * ROW_PARTS, ROW_PARTS), ROW_PARTS), :],
            xs_ref.at[pl.ds(pl.multiple_of(d, ROW_PARTS), ROW_PARTS), :], sem)

    def issue(t, c):
        for k in range(TOP_K):
            row_copy(t, dest_ref[k, t]).start(priority=k % 2)
        return c

    def drain(t, c):
        for k in range(TOP_K):
            row_copy(0, 0).wait()
        return c

    lax.fori_loop(0, tm, issue, 0)
    lax.fori_loop(0, tm, drain, 0)


def _dispatch(hp, dest, xs, tm):
    t = dest.shape[1]
    return pl.pallas_call(
        functools.partial(_dispatch_kernel, tm=tm),
        grid=(t // tm,),
        in_specs=[pl.BlockSpec((TOP_K, tm), lambda i: (0, i), memory_space=pltpu.SMEM),
                  pl.BlockSpec((ROW_PARTS * tm, LANES), lambda i: (i, 0)),
                  pl.BlockSpec(memory_space=pl.ANY)],
        out_specs=pl.BlockSpec(memory_space=pl.ANY),
        out_shape=jax.ShapeDtypeStruct(xs.shape, xs.dtype),
        scratch_shapes=[pltpu.SemaphoreType.DMA],
        input_output_aliases={2: 0},
        compiler_params=_cparams(("arbitrary",)),
    )(dest, hp, xs)


def _expert_kernel(be_ref, nu_ref, xs_ref, wg_ref, wu_ref, wd_ref, ys_ref):
    i = pl.program_id(0)

    @pl.when(i < nu_ref[0])
    def _():
        x = _load_packed(xs_ref, ROW_BLOCK).astype(BF16)
        g = jnp.dot(x, wg_ref[0].astype(BF16), preferred_element_type=F32)
        u = jnp.dot(x, wu_ref[0].astype(BF16), preferred_element_type=F32)
        hmid = (_silu(g) * u).astype(BF16)
        _store_packed(ys_ref, jnp.dot(hmid, wd_ref[0].astype(BF16), preferred_element_type=F32))

    @pl.when(i >= nu_ref[0])
    def _():
        ys_ref[...] = jnp.zeros(ys_ref.shape, ys_ref.dtype)


def _experts(xs, blk_e, n_used, w_eg, w_eu, w_ed):
    nb = xs.shape[0] // (ROW_PARTS * ROW_BLOCK)
    rows_spec = lambda: pl.BlockSpec((ROW_PARTS * ROW_BLOCK, LANES), lambda i, be, nu: (i, 0))
    grid_spec = pltpu.PrefetchScalarGridSpec(
        num_scalar_prefetch=2,
        grid=(nb,),
        in_specs=[rows_spec(),
                  pl.BlockSpec((1, D_MODEL, D_EXPERT), lambda i, be, nu: (be[i], 0, 0)),
                  pl.BlockSpec((1, D_MODEL, D_EXPERT), lambda i, be, nu: (be[i], 0, 0)),
                  pl.BlockSpec((1, D_EXPERT, D_MODEL), lambda i, be, nu: (be[i], 0, 0))],
        out_specs=rows_spec(),
    )
    return pl.pallas_call(
        _expert_kernel,
        grid_spec=grid_spec,
        out_shape=jax.ShapeDtypeStruct(xs.shape, U32),
        compiler_params=_cparams(("arbitrary",)),
    )(blk_e, n_used, xs, w_eg, w_eu, w_ed)


def _combine_kernel(dest_ref, base_ref, wts_ref, g2_ref, ys_ref, o_ref, buf, sem, *, tm):
    def row_copy(d, k, t):
        return pltpu.make_async_copy(
            ys_ref.at[pl.ds(pl.multiple_of(d, ROW_PARTS), ROW_PARTS), :],
            buf.at[k, pl.ds(pl.multiple_of(t * ROW_PARTS, ROW_PARTS), ROW_PARTS), :], sem)

    def issue(t, c):
        for k in range(TOP_K):
            row_copy(dest_ref[k, t], k, t).start(priority=k % 2)
        return c

    def drain(t, c):
        for k in range(TOP_K):
            row_copy(0, 0, 0).wait()
        return c

    lax.fori_loop(0, tm, issue, 0)
    lax.fori_loop(0, tm, drain, 0)

    w = wts_ref[...]
    y = jnp.zeros((tm, D_MODEL), F32)
    for k in range(TOP_K):
        y = y + w[:, k:k + 1] * _load_packed(buf.at[k], tm)
    o_ref[...] = base_ref[...] + _mod_val(g2_ref) * y


def _combine(base, wts_t, g2, ys, dest, seq, tm):
    t = base.shape[0]
    return pl.pallas_call(
        functools.partial(_combine_kernel, tm=tm),
        grid=(t // tm,),
        in_specs=[pl.BlockSpec((TOP_K, tm), lambda i: (0, i), memory_space=pltpu.SMEM),
                  pl.BlockSpec((tm, D_MODEL), lambda i: (i, 0)),
                  pl.BlockSpec((tm, TOP_K), lambda i: (i, 0)),
                  _mod_spec(g2, seq, tm),
                  pl.BlockSpec(memory_space=pl.ANY)],
        out_specs=pl.BlockSpec((tm, D_MODEL), lambda i: (i, 0)),
        out_shape=jax.ShapeDtypeStruct((t, D_MODEL), F32),
        scratch_shapes=[pltpu.VMEM((TOP_K, ROW_PARTS * tm, LANES), U32), pltpu.SemaphoreType.DMA],
        compiler_params=_cparams(("arbitrary",)),
    )(dest, base, wts_t, g2, ys)


def _mix(x, mods, past_k, past_v, s0, count0, p):
    batch, seq, _ = x.shape
    t = batch * seq
    x2 = x.reshape(t, D_MODEL)
    if seq % 256 == 0:
        tile = lambda cap: min(seq, cap)
    else:
        mods = [jnp.broadcast_to(m, (batch, seq, D_MODEL)).reshape(t, D_MODEL) for m in mods]
        tile = lambda cap: min(t, cap)
    shift1, scale1, gate1, shift2, scale2, gate2 = mods

    h = _norm_mod(x2, p["norm1_w"], scale1, shift1, seq, tile(512))
    qn = jnp.tile(p["q_norm_w"].reshape(1, -1), (1, D_MODEL // HEAD_DIM_A))
    kn = jnp.tile(p["k_norm_w"].reshape(1, -1), (1, D_MODEL // HEAD_DIM_A))
    w_in = p["w_in"]
    (q,) = _proj(h, w_in, 0, [BF16], qn, out_scale=(HEAD_DIM_A ** -0.5 * LOG2E,))
    k, kb = _proj(h, w_in, 1, [F32, BF16], kn)
    v, vb = _proj(h, w_in, 2, [F32, BF16])
    (qh,) = _proj(h, w_in, 3, [BF16])
    (fh,) = _proj(h, w_in, 4, [F32])
    (ih,) = _proj(h, w_in, 5, [BF16])
    (gh,) = _proj(h, w_in, 6, [BF16])
    (ga,) = _proj(h, w_in, 7, [BF16])
    (gb,) = _proj(h, w_in, 8, [BF16])

    if past_k is None:
        o_a = _attention_prompt(q, kb, vb, p["lam"], p["subln_w"], batch, seq)
    else:
        past = past_k.shape[1]
        lk = past + seq
        kb = jnp.concatenate([past_k.reshape(batch, past, D_MODEL).astype(BF16),
                              kb.reshape(batch, seq, D_MODEL)], axis=1).reshape(batch * lk, D_MODEL)
        vb = jnp.concatenate([past_v.reshape(batch, past, D_MODEL).astype(BF16),
                              vb.reshape(batch, seq, D_MODEL)], axis=1).reshape(batch * lk, D_MODEL)
        o_a = _attention_cached(q, kb, vb, p["lam"], p["subln_w"], batch, seq, lk)
    o_b, s_new = _hgrn(qh, fh, ih, gh, p["hgrn_lb"], p["hgrn_norm_w"], s0, batch, seq)

    base, hp, eidx, wts, pos, counts = _post(
        o_a, o_b, ga, gb, x2, gate1, scale2, shift2, gate2,
        p["w_proj_a"], p["w_proj_b"], p["w_out"], p["norm2_w"], p["w_router"], p["b_router"],
        p["w_sh_gate"], p["w_sh_up"], p["w_sh_down"], count0, seq, tile(256))
    return dict(base=base, hp=hp, eidx=eidx, wts=wts, pos=pos, counts=counts, k=k, v=v, s=s_new,
                gate2=gate2, seq=seq, tm=tile(512), shape=x.shape)


def kernel(x_prompt, x_sample, cache_attn_k, cache_attn_v, state_hgrn, c_prompt, c_sample, w_ada, b_ada, norm1_w, norm2_w, w_in, q_norm_w, k_norm_w, lambda_q1, lambda_k1, lambda_q2, lambda_k2, subln_w, hgrn_lb, hgrn_norm_w, w_proj_a, w_proj_b, w_out, w_router, b_router, w_exp_gate, w_exp_up, w_exp_down, w_sh_gate, w_sh_up, w_sh_down):
    bp, lp, _ = x_prompt.shape
    bs, ls, _ = x_sample.shape
    l = 0
    lam = (jnp.exp(jnp.sum(lambda_q1[l] * lambda_k1[l])) - jnp.exp(jnp.sum(lambda_q2[l] * lambda_k2[l]))
           + LAMBDA_INIT).astype(F32).reshape(1, 1)
    p = dict(
        norm1_w=norm1_w[l], norm2_w=norm2_w[l], w_in=w_in[l].astype(BF16),
        q_norm_w=q_norm_w[l], k_norm_w=k_norm_w[l], lam=lam, subln_w=subln_w[l],
        hgrn_lb=hgrn_lb, hgrn_norm_w=hgrn_norm_w[l],
        w_proj_a=w_proj_a[l].astype(BF16), w_proj_b=w_proj_b[l].astype(BF16), w_out=w_out[l].astype(BF16),
        w_router=w_router[l], b_router=b_router[l],
        w_sh_gate=w_sh_gate[l].astype(BF16), w_sh_up=w_sh_up[l].astype(BF16), w_sh_down=w_sh_down[l].astype(BF16),
    )
    mod = _ada(jnp.concatenate([c_prompt, c_sample], axis=0), w_ada[l], b_ada[l])
    mod = mod.reshape(bp + bs, 6, 1, D_MODEL)
    mods_p = [mod[:bp, j] for j in range(6)]
    mods_s = [mod[bp:, j] for j in range(6)]

    zero_state = jnp.zeros((bp, N_HEADS, HEAD_W, HEAD_W), F32)
    zero_count = jnp.zeros((N_EXPERTS, 1), F32)
    gp = _mix(x_prompt, mods_p, None, None, zero_state, zero_count, p)
    gs = _mix(x_sample, mods_s, cache_attn_k[l], cache_attn_v[l], state_hgrn[l], gp["counts"], p)
    groups = (gp, gs)

    counts = gs["counts"].reshape(N_EXPERTS).astype(I32)
    pcounts = (counts + ROW_BLOCK - 1) // ROW_BLOCK * ROW_BLOCK
    pend = jnp.cumsum(pcounts)
    pstart = pend - pcounts
    n_assign = (bp * lp + bs * ls) * TOP_K
    nb = -(-(n_assign + N_EXPERTS * (ROW_BLOCK - 1)) // ROW_BLOCK)
    blk_e = jnp.minimum(jnp.searchsorted(pend, jnp.arange(nb, dtype=I32) * ROW_BLOCK, side="right"),
                        N_EXPERTS - 1).astype(I32)
    n_used = (pend[-1] // ROW_BLOCK).astype(I32).reshape(1)
    pstart_col = pstart.astype(F32).reshape(N_EXPERTS, 1)

    xs = jnp.zeros((ROW_PARTS * nb * ROW_BLOCK, LANES), U32)
    for g in groups:
        g["dest"] = _dest(g["eidx"], g["pos"], pstart_col)
        xs = _dispatch(g["hp"], g["dest"], xs, g["tm"])
    ys = _experts(xs, blk_e, n_used, w_exp_gate[l], w_exp_up[l], w_exp_down[l])
    yp, ysm = [_combine(g["base"], g["wts"].T, g["gate2"], ys, g["dest"], g["seq"], g["tm"]).reshape(g["shape"])
               for g in groups]

    return (yp, ysm,
            gp["k"].reshape(1, bp, lp, N_HEADS, 2, HEAD_DIM_A), gp["v"].reshape(1, bp, lp, N_HEADS, HEAD_W),
            gp["s"][None],
            gs["k"].reshape(1, bs, ls, N_HEADS, 2, HEAD_DIM_A), gs["v"].reshape(1, bs, ls, N_HEADS, HEAD_W),
            gs["s"][None])
```

```python
import functools
import math

import jax
import jax.numpy as jnp
from jax import lax
from jax.experimental import pallas as pl
from jax.experimental.pallas import tpu as pltpu

F32 = jnp.float32
BF16 = jnp.bfloat16
U32 = jnp.uint32
I32 = jnp.int32

D_MODEL = 1024
N_HEADS = 8
HEAD_W = 128
HEAD_DIM_A = 64
CHUNK = 64
N_EXPERTS = 256
TOP_K = 8
N_GROUPS = 8
GROUP_W = N_EXPERTS // N_GROUPS
TOPK_GROUPS = 4
D_EXPERT = 256
ROUTED_SCALE = 2.5
EPS = 1e-6
LAMBDA_INIT = 0.8 - 0.6 * math.exp(-0.3 * 0)
LANES = 128
NEG = -1e30
LOG2E = math.log2(math.e)

ATTN_BOUND_MAX_GAP = 100.0
GLA_CHUNK = 32
ROW_BLOCK = 256
VMEM_LIMIT = 48 * 1024 * 1024


def _sigmoid(x):
    return 1.0 / (1.0 + jnp.exp(-x))


def _silu(x):
    return x * _sigmoid(x)


def _cparams(sem):
    return pltpu.CompilerParams(dimension_semantics=sem, vmem_limit_bytes=VMEM_LIMIT)


ROW_PARTS = 4


def _store_packed(ref, x, row0=0):
    m, half = x.shape[0], x.shape[1] // 2
    lo = pltpu.bitcast(x[:, :half].astype(BF16).astype(F32), U32)
    hi = pltpu.bitcast(x[:, half:].astype(BF16).astype(F32), U32)
    w = (hi & jnp.uint32(0xFFFF0000)) | (lo >> 16)
    for j in range(ROW_PARTS):
        ref[pl.ds(ROW_PARTS * row0 + j, m, stride=ROW_PARTS), :] = w[:, j * LANES:(j + 1) * LANES]


def _load_packed(ref, m, row0=0):
    parts = [ref[pl.ds(ROW_PARTS * row0 + j, m, stride=ROW_PARTS), :] for j in range(ROW_PARTS)]
    lo = [pltpu.bitcast(w << 16, F32) for w in parts]
    hi = [pltpu.bitcast(w & jnp.uint32(0xFFFF0000), F32) for w in parts]
    return jnp.concatenate(lo + hi, axis=1)


def _mod_spec(a, seq, tm):
    if a.ndim == 3:
        per_b = seq // tm
        return pl.BlockSpec((1, 1, D_MODEL), lambda i: (i // per_b, 0, 0))
    return pl.BlockSpec((tm, D_MODEL), lambda i: (i, 0))


def _mod_val(ref):
    v = ref[...]
    return v.reshape(v.shape[-2], v.shape[-1])


def _ada_kernel(c_ref, w_ref, b_ref, o_ref):
    c = _silu(c_ref[...])
    o_ref[...] = jnp.dot(c, w_ref[...], precision=lax.Precision.HIGHEST,
                         preferred_element_type=F32) + b_ref[...]


def _ada(c, w_ada, b_ada):
    n = c.shape[0]
    nj = w_ada.shape[1] // D_MODEL
    return pl.pallas_call(
        _ada_kernel,
        grid=(nj,),
        in_specs=[pl.BlockSpec((n, D_MODEL), lambda j: (0, 0)),
                  pl.BlockSpec((D_MODEL, D_MODEL), lambda j: (0, j)),
                  pl.BlockSpec((1, D_MODEL), lambda j: (0, j))],
        out_specs=pl.BlockSpec((n, D_MODEL), lambda j: (0, j)),
        out_shape=jax.ShapeDtypeStruct((n, w_ada.shape[1]), F32),
        compiler_params=_cparams(("arbitrary",)),
    )(c, w_ada, b_ada.reshape(1, -1))


def _norm_mod_kernel(x_ref, w_ref, sc_ref, sh_ref, o_ref):
    x = x_ref[...]
    y = x * lax.rsqrt(jnp.mean(x * x, axis=-1, keepdims=True) + EPS) * w_ref[...]
    o_ref[...] = (y * (1.0 + _mod_val(sc_ref)) + _mod_val(sh_ref)).astype(o_ref.dtype)


def _norm_mod(x2, w, scale, shift, seq, tm):
    t = x2.shape[0]
    return pl.pallas_call(
        _norm_mod_kernel,
        grid=(t // tm,),
        in_specs=[pl.BlockSpec((tm, D_MODEL), lambda i: (i, 0)),
                  pl.BlockSpec((1, D_MODEL), lambda i: (0, 0)),
                  _mod_spec(scale, seq, tm), _mod_spec(shift, seq, tm)],
        out_specs=pl.BlockSpec((tm, D_MODEL), lambda i: (i, 0)),
        out_shape=jax.ShapeDtypeStruct((t, D_MODEL), BF16),
        compiler_params=_cparams(("parallel",)),
    )(x2, w.reshape(1, -1), scale, shift)


def _head_norm(acc, w):
    r = lax.broadcasted_iota(I32, (HEAD_W, HEAD_W), 0) // HEAD_DIM_A
    c = lax.broadcasted_iota(I32, (HEAD_W, HEAD_W), 1) // HEAD_DIM_A
    ones = (r == c).astype(BF16)
    outs = []
    for h in range(N_HEADS):
        a = acc[:, h * HEAD_W:(h + 1) * HEAD_W]
        sq = a * a
        hi = sq.astype(BF16)
        lo = (sq - hi.astype(F32)).astype(BF16)
        ss = (jnp.dot(hi, ones, preferred_element_type=F32)
              + jnp.dot(lo, ones, preferred_element_type=F32))
        outs.append(a * lax.rsqrt(ss * (1.0 / HEAD_DIM_A) + EPS) * w[:, h * HEAD_W:(h + 1) * HEAD_W])
    return jnp.concatenate(outs, axis=1)


def _proj_kernel(h_ref, w_ref, nw_ref, *o_refs, head_norm, out_scale):
    acc = jnp.dot(h_ref[...], w_ref[...], preferred_element_type=F32)
    if head_norm:
        acc = _head_norm(acc, nw_ref[...])
    for o_ref, s in zip(o_refs, out_scale):
        o_ref[...] = (acc if s == 1.0 else acc * s).astype(o_ref.dtype)


def _proj(h, w_in_bf, group, out_dtypes, norm_w=None, out_scale=None, tm=1024):
    t = h.shape[0]
    tm = min(tm, t)
    nw = jnp.ones((1, D_MODEL), F32) if norm_w is None else norm_w
    out_scale = tuple(out_scale or (1.0,) * len(out_dtypes))
    outs = pl.pallas_call(
        functools.partial(_proj_kernel, head_norm=norm_w is not None, out_scale=out_scale),
        grid=(t // tm,),
        in_specs=[pl.BlockSpec((tm, D_MODEL), lambda i: (i, 0)),
                  pl.BlockSpec((D_MODEL, D_MODEL), lambda i: (0, group)),
                  pl.BlockSpec((1, D_MODEL), lambda i: (0, 0))],
        out_specs=[pl.BlockSpec((tm, D_MODEL), lambda i: (i, 0)) for _ in out_dtypes],
        out_shape=[jax.ShapeDtypeStruct((t, D_MODEL), dt) for dt in out_dtypes],
        compiler_params=_cparams(("parallel",)),
    )(h, w_in_bf, nw)
    return outs


def _split_maps(q):
    lane = lax.broadcasted_iota(I32, q.shape, 1)
    zero = jnp.zeros_like(q)
    return jnp.where(lane < HEAD_DIM_A, q, zero), jnp.where(lane >= HEAD_DIM_A, q, zero)


def _finish_head(acc1, l1, acc2, l2, lam, sw):
    o = acc1 / l1 - lam * (acc2 / l2)
    o = o * lax.rsqrt(jnp.mean(o * o, axis=-1, keepdims=True) + EPS) * sw
    return o * (1.0 - LAMBDA_INIT)


def _attn_kernel(q_ref, k_ref, v_ref, lam_ref, sw_ref, o_ref, mx_s, mrep_s, ls_s, acc_s,
                 *, tq, tk, hp):
    qi = pl.program_id(2)
    nf = tk // LANES
    qs = []
    for hh in range(hp):
        qs.extend(_split_maps(q_ref[:, hh * HEAD_W:(hh + 1) * HEAD_W]))
    ns = 2 * hp

    r_chunk = lax.broadcasted_iota(I32, (tq, tk), 0) // CHUNK
    c_chunk = lax.broadcasted_iota(I32, (tq, tk), 1) // CHUNK
    diag_ok = c_chunk <= r_chunk

    def scores(kt, s, masked):
        k = k_ref[pl.ds(pl.multiple_of(kt * tk, tk), tk), (s // 2) * HEAD_W:(s // 2 + 1) * HEAD_W]
        sc = lax.dot_general(qs[s], k, (((1,), (1,)), ((), ())), preferred_element_type=F32)
        return jnp.where(diag_ok, sc, NEG) if masked else sc

    def fold(x, op):
        r = x[:, :LANES]
        for j in range(1, nf):
            r = op(r, x[:, j * LANES:(j + 1) * LANES])
        return r

    def pass1(kt, first):
        for s in range(ns):
            m = fold(scores(kt, s, first), jnp.maximum)
            mx_s[s] = m if first else jnp.maximum(mx_s[s], m)

    def pass2(kt, first):
        for s in range(ns):
            m = mrep_s[s]
            p = jnp.exp2(scores(kt, s, first) - jnp.concatenate([m] * nf, axis=1))
            v = v_ref[pl.ds(pl.multiple_of(kt * tk, tk), tk), (s // 2) * HEAD_W:(s // 2 + 1) * HEAD_W]
            pv = jnp.dot(p.astype(BF16), v, preferred_element_type=F32)
            ls_s[s] = fold(p, jnp.add) if first else ls_s[s] + fold(p, jnp.add)
            acc_s[s] = pv if first else acc_s[s] + pv

    def loop(fn):
        def body(kt, c):
            fn(kt, False)
            return c

        fn(qi, True)
        lax.fori_loop(0, qi, body, 0)

    knorm = lam_ref[0, 1]
    use_bound = lam_ref[0, 2] > 0.5

    @pl.when(use_bound)
    def _():
        for s in range(ns):
            qf = qs[s].astype(F32)
            qn = jnp.sqrt(jnp.sum(qf * qf, axis=-1, keepdims=True))
            mrep_s[s] = jnp.broadcast_to(qn * knorm, (tq, LANES))

    @pl.when(jnp.logical_not(use_bound))
    def _():
        loop(pass1)
        for s in range(ns):
            mrep_s[s] = jnp.broadcast_to(jnp.max(mx_s[s], axis=-1, keepdims=True), (tq, LANES))

    loop(pass2)

    lam = lam_ref[0, 0]
    for hh in range(hp):
        l1 = jnp.sum(ls_s[2 * hh], axis=-1, keepdims=True)
        l2 = jnp.sum(ls_s[2 * hh + 1], axis=-1, keepdims=True)
        o = _finish_head(acc_s[2 * hh], l1, acc_s[2 * hh + 1], l2, lam, sw_ref[...])
        o_ref[:, hh * HEAD_W:(hh + 1) * HEAD_W] = o.astype(o_ref.dtype)


def _attention_prompt(q, kb, vb, lam, subln_w, batch, seq):
    tq = tk = min(seq, 256)
    hp = 8
    nq = seq // tq
    kern = functools.partial(_attn_kernel, tq=tq, tk=tk, hp=hp)
    w = hp * HEAD_W
    return pl.pallas_call(
        kern,
        grid=(batch, N_HEADS // hp, nq),
        in_specs=[pl.BlockSpec((tq, w), lambda b, h, i: (b * nq + i, h)),
                  pl.BlockSpec((seq, w), lambda b, h, i: (b, h)),
                  pl.BlockSpec((seq, w), lambda b, h, i: (b, h)),
                  pl.BlockSpec((1, 4), lambda b, h, i: (0, 0), memory_space=pltpu.SMEM),
                  pl.BlockSpec((1, HEAD_W), lambda b, h, i: (0, 0))],
        out_specs=pl.BlockSpec((tq, w), lambda b, h, i: (b * nq + i, h)),
        out_shape=jax.ShapeDtypeStruct((batch * seq, D_MODEL), BF16),
        scratch_shapes=[pltpu.VMEM((2 * hp, tq, LANES), F32) for _ in range(4)],
        compiler_params=_cparams(("parallel", "parallel", "arbitrary")),
    )(q, kb, vb, lam, subln_w.reshape(1, -1))


def _attn_cached_kernel(q_ref, k_ref, v_ref, lam_ref, sw_ref, o_ref, *, lq, lk):
    qpos = (lk - lq) + lax.broadcasted_iota(I32, (lq, lk), 0)
    kpos = lax.broadcasted_iota(I32, (lq, lk), 1)
    allowed = (kpos // CHUNK) <= (qpos // CHUNK)
    lam = lam_ref[0, 0]
    for h in range(N_HEADS):
        sl = slice(h * HEAD_W, (h + 1) * HEAD_W)
        k = k_ref[:, sl]
        v = v_ref[:, sl]
        accs, ls = [], []
        for qm in _split_maps(q_ref[:, sl]):
            sc = lax.dot_general(qm, k, (((1,), (1,)), ((), ())), preferred_element_type=F32)
            sc = jnp.where(allowed, sc, NEG)
            p = jnp.exp2(sc - jnp.max(sc, axis=-1, keepdims=True))
            ls.append(jnp.sum(p, axis=-1, keepdims=True))
            accs.append(jnp.dot(p.astype(BF16), v, preferred_element_type=F32))
        o = _finish_head(accs[0], ls[0], accs[1], ls[1], lam, sw_ref[...])
        o_ref[:, sl] = o.astype(o_ref.dtype)


def _attention_cached(q, kb, vb, lam, subln_w, batch, lq, lk):
    return pl.pallas_call(
        functools.partial(_attn_cached_kernel, lq=lq, lk=lk),
        grid=(batch,),
        in_specs=[pl.BlockSpec((lq, D_MODEL), lambda b: (b, 0)),
                  pl.BlockSpec((lk, D_MODEL), lambda b: (b, 0)),
                  pl.BlockSpec((lk, D_MODEL), lambda b: (b, 0)),
                  pl.BlockSpec((1, 4), lambda b: (0, 0), memory_space=pltpu.SMEM),
                  pl.BlockSpec((1, HEAD_W), lambda b: (0, 0))],
        out_specs=pl.BlockSpec((lq, D_MODEL), lambda b: (b, 0)),
        out_shape=jax.ShapeDtypeStruct((batch * lq, D_MODEL), BF16),
        compiler_params=_cparams(("parallel",)),
    )(q, kb, vb, lam, subln_w.reshape(1, -1))


def _hgrn_kernel(qh_ref, fh_ref, ih_ref, gh_ref, lb_ref, nw_ref, s0_ref, o_ref, s_out_ref,
                 st_s, qin_s, qmid_s, kmid_s, kend_s, dec_s, *, ct, c):
    t = pl.program_id(1)
    nt = pl.num_programs(1)
    nc = ct // c

    @pl.when(t == 0)
    def _():
        for h in range(N_HEADS):
            st_s[h] = s0_ref[0, h].T

    a = lb_ref[...]
    amax = jnp.max(a, axis=0, keepdims=True)
    e = jnp.exp(a - amax)
    lb = e[0:1] / jnp.sum(e, axis=0, keepdims=True)

    row = lax.broadcasted_iota(I32, (c, c), 0)
    col = lax.broadcasted_iota(I32, (c, c), 1)
    causal = col <= row
    tril = causal.astype(F32)

    for ci in range(nc):
        rows = slice(ci * c, (ci + 1) * c)
        f = lb + (1.0 - lb) * _sigmoid(fh_ref[rows, :])
        b = jnp.dot(tril, jnp.log(f), precision=lax.Precision.HIGHEST, preferred_element_type=F32)
        b_last = b[c - 1:c, :]
        b_mid = b[c // 2 - 1:c // 2, :]
        q = _silu(qh_ref[rows, :].astype(F32)) * (HEAD_W ** -0.5)
        kk = 1.0 - f
        e_dn = jnp.exp(b - b_mid)
        e_up = jnp.exp(b_mid - b)
        qm = q * e_dn
        km = kk * e_up
        qmid_s[rows, :] = qm.astype(BF16)
        kmid_s[rows, :] = km.astype(BF16)
        qin_s[rows, :] = (qm * jnp.exp(b_mid)).astype(BF16)
        kend_s[rows, :] = (km * jnp.exp(b_last - b_mid)).astype(BF16)
        dec_s[ci:ci + 1, :] = jnp.exp(b_last)

    nw = nw_ref[...]
    st = [st_s[h] for h in range(N_HEADS)]
    for ci in range(nc):
        rows = slice(ci * c, (ci + 1) * c)
        gate = _silu(gh_ref[rows, :].astype(F32))
        decay = dec_s[ci:ci + 1, :]
        for h in range(N_HEADS):
            sl = slice(h * HEAD_W, (h + 1) * HEAD_W)
            v = ih_ref[rows, sl]
            inter = lax.dot_general(qin_s[rows, sl], st[h].astype(BF16), (((1,), (1,)), ((), ())),
                                    preferred_element_type=F32)
            att = lax.dot_general(qmid_s[rows, sl], kmid_s[rows, sl], (((1,), (1,)), ((), ())),
                                  preferred_element_type=F32)
            att = jnp.where(causal, att, 0.0)
            o = inter + jnp.dot(att.astype(BF16), v, preferred_element_type=F32)
            upd = lax.dot_general(v, kend_s[rows, sl], (((0,), (0,)), ((), ())),
                                  preferred_element_type=F32)
            st[h] = decay[:, sl] * st[h] + upd
            o = o * lax.rsqrt(jnp.mean(o * o, axis=-1, keepdims=True) + EPS) * nw[:, sl]
            o_ref[rows, sl] = (o * gate[:, sl]).astype(o_ref.dtype)
    for h in range(N_HEADS):
        st_s[h] = st[h]

    @pl.when(t == nt - 1)
    def _():
        for h in range(N_HEADS):
            s_out_ref[0, h] = st_s[h].T


def _hgrn(qh, fh, ih, gh, hgrn_lb, norm_w, s0, batch, seq):
    ct = min(seq, 512)
    c = min(GLA_CHUNK, ct)
    nt = seq // ct
    kern = functools.partial(_hgrn_kernel, ct=ct, c=c)
    tok = pl.BlockSpec((ct, D_MODEL), lambda b, t: (b * nt + t, 0))
    st_spec = pl.BlockSpec((1, N_HEADS, HEAD_W, HEAD_W), lambda b, t: (b, 0, 0, 0))
    return pl.pallas_call(
        kern,
        grid=(batch, nt),
        in_specs=[tok, tok, tok, tok,
                  pl.BlockSpec((2, D_MODEL), lambda b, t: (0, 0)),
                  pl.BlockSpec((1, D_MODEL), lambda b, t: (0, 0)),
                  st_spec],
        out_specs=[tok, st_spec],
        out_shape=[jax.ShapeDtypeStruct((batch * seq, D_MODEL), BF16),
                   jax.ShapeDtypeStruct((batch, N_HEADS, HEAD_W, HEAD_W), F32)],
        scratch_shapes=[pltpu.VMEM((N_HEADS, HEAD_W, HEAD_W), F32)]
        + [pltpu.VMEM((ct, D_MODEL), BF16) for _ in range(4)]
        + [pltpu.VMEM((max(ct // c, 8), D_MODEL), F32)],
        compiler_params=_cparams(("parallel", "arbitrary")),
    )(qh, fh, ih, gh, hgrn_lb, jnp.tile(norm_w.reshape(1, -1), (1, N_HEADS)), s0)


def _route(h2, wrt_ref, brt_ref, carry_s, tm):
    nt_dims = (((1,), (1,)), ((), ()))
    w_hi = wrt_ref[0]
    w_lo = wrt_ref[1]
    h_hi = h2.astype(BF16)
    h_lo = (h2 - h_hi.astype(F32)).astype(BF16)
    logits = (lax.dot_general(w_hi, h_hi, nt_dims, preferred_element_type=F32)
              + lax.dot_general(w_hi, h_lo, nt_dims, preferred_element_type=F32)
              + lax.dot_general(w_lo, h_hi, nt_dims, preferred_element_type=F32))
    scores = _sigmoid(logits)
    biased = scores + brt_ref[...]
    big = float(2 * N_EXPERTS)

    x3 = biased.reshape(N_GROUPS, GROUP_W, tm)
    i3 = lax.broadcasted_iota(I32, x3.shape, 1).astype(F32)
    m1 = jnp.max(x3, axis=1, keepdims=True)
    i1 = jnp.min(jnp.where(x3 == m1, i3, big), axis=1, keepdims=True)
    m2 = jnp.max(jnp.where(i3 == i1, NEG, x3), axis=1, keepdims=True)
    gs = (m1 + m2).reshape(N_GROUPS, tm)

    g_iota = lax.broadcasted_iota(I32, (N_GROUPS, tm), 0).astype(F32)
    rem = gs
    gsel = jnp.zeros((N_GROUPS, tm), F32)
    for _ in range(TOPK_GROUPS):
        gm = jnp.max(rem, axis=0, keepdims=True)
        first = jnp.min(jnp.where(rem == gm, g_iota, big), axis=0, keepdims=True)
        sel = g_iota == first
        gsel = jnp.where(sel, 1.0, gsel)
        rem = jnp.where(sel, NEG, rem)
    keep = jnp.broadcast_to(gsel.reshape(N_GROUPS, 1, tm), x3.shape) > 0.5
    masked = jnp.where(keep, x3, NEG).reshape(N_EXPERTS, tm)

    e_iota = lax.broadcasted_iota(I32, (N_EXPERTS, tm), 0).astype(F32)
    onehot = jnp.zeros((N_EXPERTS, tm), F32)
    idxs, ws = [], []
    for _ in range(TOP_K):
        mk = jnp.max(masked, axis=0, keepdims=True)
        ik = jnp.min(jnp.where(masked == mk, e_iota, big), axis=0, keepdims=True)
        selk = e_iota == ik
        ws.append(jnp.sum(jnp.where(selk, scores, 0.0), axis=0, keepdims=True))
        idxs.append(ik)
        masked = jnp.where(selk, NEG, masked)
        onehot = jnp.where(selk, 1.0, onehot)
    wsum = ws[0]
    for k in range(1, TOP_K):
        wsum = wsum + ws[k]

    r = lax.broadcasted_iota(I32, (tm, tm), 0)
    cidx = lax.broadcasted_iota(I32, (tm, tm), 1)
    before = (r < cidx).astype(BF16)
    cum = jnp.dot(onehot.astype(BF16), before, preferred_element_type=F32) + carry_s[...]
    carry_s[...] = carry_s[...] + jnp.sum(onehot, axis=1, keepdims=True)

    k_iota = lax.broadcasted_iota(I32, (TOP_K, tm), 0)
    e_out = jnp.zeros((TOP_K, tm), F32)
    w_out = jnp.zeros((TOP_K, tm), F32)
    p_out = jnp.zeros((TOP_K, tm), F32)
    for k in range(TOP_K):
        pk = jnp.sum(jnp.where(e_iota == idxs[k], cum, 0.0), axis=0, keepdims=True)
        e_out = jnp.where(k_iota == k, idxs[k], e_out)
        w_out = jnp.where(k_iota == k, ws[k] / wsum * ROUTED_SCALE, w_out)
        p_out = jnp.where(k_iota == k, pk, p_out)
    return e_out, w_out, p_out


def _post_kernel(oa_ref, ob_ref, ga_ref, gb_ref, x_ref, g1_ref, sc2_ref, sh2_ref, g2_ref,
                 wpa_ref, wpb_ref, wo_ref, n2_ref, wrt_ref, brt_ref, wsg_ref, wsu_ref, wsd_ref, c0_ref,
                 base_ref, hp_ref, eidx_ref, wts_ref, pos_ref, cnt_ref, carry_s, *, tm):
    i = pl.program_id(0)

    @pl.when(i == 0)
    def _():
        carry_s[...] = c0_ref[...]

    pa = jnp.dot(oa_ref[...], wpa_ref[...], preferred_element_type=F32)
    pb = jnp.dot(ob_ref[...], wpb_ref[...], preferred_element_type=F32)
    u = _sigmoid(ga_ref[...].astype(F32)) * pa + _sigmoid(gb_ref[...].astype(F32)) * pb
    x1 = x_ref[...] + _mod_val(g1_ref) * jnp.dot(u.astype(BF16), wo_ref[...], preferred_element_type=F32)
    y = x1 * lax.rsqrt(jnp.mean(x1 * x1, axis=-1, keepdims=True) + EPS) * n2_ref[...]
    h2 = y * (1.0 + _mod_val(sc2_ref)) + _mod_val(sh2_ref)
    _store_packed(hp_ref, h2)

    hb = h2.astype(BF16)
    sg = jnp.dot(hb, wsg_ref[...], preferred_element_type=F32)
    su = jnp.dot(hb, wsu_ref[...], preferred_element_type=F32)
    shared = jnp.dot((_silu(sg) * su).astype(BF16), wsd_ref[...], preferred_element_type=F32)
    base_ref[...] = x1 + _mod_val(g2_ref) * shared

    e_out, w_out, p_out = _route(h2, wrt_ref, brt_ref, carry_s, tm)
    eidx_ref[...] = e_out.astype(I32)
    wts_ref[...] = w_out
    pos_ref[...] = p_out.astype(I32)
    cnt_ref[...] = carry_s[...]


def _post(oa, ob, ga, gb, x2, g1, sc2, sh2, g2, wpa, wpb, wo, n2, wr, br, wsg, wsu, wsd, count0, seq, tm):
    t = x2.shape[0]
    tok = lambda: pl.BlockSpec((tm, D_MODEL), lambda i: (i, 0))
    mod = lambda: _mod_spec(g1, seq, tm)
    full = lambda a: pl.BlockSpec(a.shape, lambda i: (0,) * a.ndim)
    n2 = n2.reshape(1, -1)
    wrt_hi = wr.T.astype(BF16)
    wrt = jnp.stack([wrt_hi, (wr.T - wrt_hi.astype(F32)).astype(BF16)])
    brt = br.reshape(-1, 1)
    k_out = lambda: pl.BlockSpec((TOP_K, tm), lambda i: (0, i))
    return pl.pallas_call(
        functools.partial(_post_kernel, tm=tm),
        grid=(t // tm,),
        in_specs=[tok(), tok(), tok(), tok(), tok(), mod(), mod(), mod(), mod(),
                  full(wpa), full(wpb), full(wo), full(n2), full(wrt), full(brt),
                  full(wsg), full(wsu), full(wsd), full(count0)],
        out_specs=[tok(), pl.BlockSpec((ROW_PARTS * tm, LANES), lambda i: (i, 0)),
                   k_out(), k_out(), k_out(),
                   pl.BlockSpec((N_EXPERTS, 1), lambda i: (0, 0))],
        out_shape=[jax.ShapeDtypeStruct((t, D_MODEL), F32),
                   jax.ShapeDtypeStruct((ROW_PARTS * t, LANES), U32),
                   jax.ShapeDtypeStruct((TOP_K, t), I32),
                   jax.ShapeDtypeStruct((TOP_K, t), F32),
                   jax.ShapeDtypeStruct((TOP_K, t), I32),
                   jax.ShapeDtypeStruct((N_EXPERTS, 1), F32)],
        scratch_shapes=[pltpu.VMEM((N_EXPERTS, 1), F32)],
        compiler_params=_cparams(("arbitrary",)),
    )(oa, ob, ga, gb, x2, g1, sc2, sh2, g2, wpa, wpb, wo, n2, wrt, brt, wsg, wsu, wsd, count0)


def _dest_kernel(eidx_ref, pos_ref, pstart_ref, o_ref, *, tm):
    e_iota = lax.broadcasted_iota(I32, (N_EXPERTS, tm), 0)
    k_iota = lax.broadcasted_iota(I32, (TOP_K, tm), 0)
    eidx = eidx_ref[...]
    start = jnp.zeros((TOP_K, tm), F32)
    for k in range(TOP_K):
        sel = e_iota == eidx[k:k + 1, :]
        sk = jnp.sum(jnp.where(sel, pstart_ref[...], 0.0), axis=0, keepdims=True)
        start = jnp.where(k_iota == k, sk, start)
    o_ref[...] = (start.astype(I32) + pos_ref[...]) * ROW_PARTS


def _dest(eidx, pos, pstart_col):
    t = eidx.shape[1]
    tm = min(t, 1024)
    blk = lambda: pl.BlockSpec((TOP_K, tm), lambda i: (0, i))
    return pl.pallas_call(
        functools.partial(_dest_kernel, tm=tm),
        grid=(t // tm,),
        in_specs=[blk(), blk(), pl.BlockSpec((N_EXPERTS, 1), lambda i: (0, 0))],
        out_specs=blk(),
        out_shape=jax.ShapeDtypeStruct((TOP_K, t), I32),
        compiler_params=_cparams(("parallel",)),
    )(eidx, pos, pstart_col)


def _dispatch_kernel(dest_ref, hp_ref, xs_in_ref, xs_ref, sem, *, tm):
    del xs_in_ref

    def row_copy(t, d):
        return pltpu.make_async_copy(
            hp_ref.at[pl.ds(pl.multiple_of(t * ROW_PARTS, ROW_PARTS), ROW_PARTS), :],
            xs_ref.at[pl.ds(pl.multiple_of(d, ROW_PARTS), ROW_PARTS), :], sem)

    def issue(t, c):
        for k in range(TOP_K):
            row_copy(t, dest_ref[k, t]).start(priority=k % 2)
        return c

    def drain(t, c):
        for k in range(TOP_K):
            row_copy(0, 0).wait()
        return c

    lax.fori_loop(0, tm, issue, 0)
    lax.fori_loop(0, tm, drain, 0)


def _dispatch(hp, dest, xs, tm):
    t = dest.shape[1]
    return pl.pallas_call(
        functools.partial(_dispatch_kernel, tm=tm),
        grid=(t // tm,),
        in_specs=[pl.BlockSpec((TOP_K, tm), lambda i: (0, i), memory_space=pltpu.SMEM),
                  pl.BlockSpec((ROW_PARTS * tm, LANES), lambda i: (i, 0)),
                  pl.BlockSpec(memory_space=pl.ANY)],
        out_specs=pl.BlockSpec(memory_space=pl.ANY),
        out_shape=jax.ShapeDtypeStruct(xs.shape, xs.dtype),
        scratch_shapes=[pltpu.SemaphoreType.DMA],
        input_output_aliases={2: 0},
        compiler_params=_cparams(("arbitrary",)),
    )(dest, hp, xs)


def _expert_kernel(be_ref, nu_ref, xs_ref, wg0_ref, wu0_ref, wd0_ref, wg1_ref, wu1_ref, wd1_ref, ys_ref,
                   wg_s, wu_s, wd_s):
    i = pl.program_id(0)
    b0 = 2 * i
    experts = (be_ref[b0], be_ref[b0 + 1])
    prev = (be_ref[jnp.maximum(b0 - 2, 0)], be_ref[jnp.maximum(b0 - 1, 0)])
    w_refs = ((wg0_ref, wu0_ref, wd0_ref), (wg1_ref, wu1_ref, wd1_ref))

    for slot in range(2):
        @pl.when(jnp.logical_and(b0 + slot < nu_ref[0], jnp.logical_or(i == 0, experts[slot] != prev[slot])))
        def _(slot=slot):
            wg_s[slot] = w_refs[slot][0][0].astype(BF16)
            wu_s[slot] = w_refs[slot][1][0].astype(BF16)
            wd_s[slot] = w_refs[slot][2][0].astype(BF16)

    def ffn(row0, m, slot):
        x = _load_packed(xs_ref, m, row0).astype(BF16)
        g = jnp.dot(x, wg_s[slot], preferred_element_type=F32)
        u = jnp.dot(x, wu_s[slot], preferred_element_type=F32)
        hmid = (_silu(g) * u).astype(BF16)
        _store_packed(ys_ref, jnp.dot(hmid, wd_s[slot], preferred_element_type=F32), row0)

    both = b0 + 1 < nu_ref[0]
    same = jnp.logical_and(both, experts[0] == experts[1])

    @pl.when(same)
    def _():
        ffn(0, 2 * ROW_BLOCK, 0)

    @pl.when(jnp.logical_and(both, jnp.logical_not(same)))
    def _():
        ffn(0, ROW_BLOCK, 0)
        ffn(ROW_BLOCK, ROW_BLOCK, 1)

    @pl.when(jnp.logical_and(b0 < nu_ref[0], jnp.logical_not(both)))
    def _():
        ffn(0, ROW_BLOCK, 0)
        ys_ref[pl.ds(ROW_PARTS * ROW_BLOCK, ROW_PARTS * ROW_BLOCK), :] = jnp.zeros(
            (ROW_PARTS * ROW_BLOCK, LANES), ys_ref.dtype)

    @pl.when(b0 >= nu_ref[0])
    def _():
        ys_ref[...] = jnp.zeros(ys_ref.shape, ys_ref.dtype)


def _experts(xs, blk_e, n_used, w_eg, w_eu, w_ed):
    nb = xs.shape[0] // (ROW_PARTS * ROW_BLOCK)
    rows_spec = lambda: pl.BlockSpec((2 * ROW_PARTS * ROW_BLOCK, LANES), lambda i, be, nu: (i, 0))
    w_up = lambda s: pl.BlockSpec((1, D_MODEL, D_EXPERT), lambda i, be, nu: (be[2 * i + s], 0, 0))
    w_dn = lambda s: pl.BlockSpec((1, D_EXPERT, D_MODEL), lambda i, be, nu: (be[2 * i + s], 0, 0))
    grid_spec = pltpu.PrefetchScalarGridSpec(
        num_scalar_prefetch=2,
        grid=(nb // 2,),
        in_specs=[rows_spec(), w_up(0), w_up(0), w_dn(0), w_up(1), w_up(1), w_dn(1)],
        out_specs=rows_spec(),
        scratch_shapes=[pltpu.VMEM((2, D_MODEL, D_EXPERT), BF16), pltpu.VMEM((2, D_MODEL, D_EXPERT), BF16),
                        pltpu.VMEM((2, D_EXPERT, D_MODEL), BF16)],
    )
    return pl.pallas_call(
        _expert_kernel,
        grid_spec=grid_spec,
        out_shape=jax.ShapeDtypeStruct(xs.shape, U32),
        compiler_params=_cparams(("arbitrary",)),
    )(blk_e, n_used, xs, w_eg, w_eu, w_ed, w_eg, w_eu, w_ed)


def _combine_kernel(dest_ref, base_ref, wts_ref, g2_ref, ys_ref, o_ref, buf, sem, *, tm):
    def row_copy(d, k, t):
        return pltpu.make_async_copy(
            ys_ref.at[pl.ds(pl.multiple_of(d, ROW_PARTS), ROW_PARTS), :],
            buf.at[k, pl.ds(pl.multiple_of(t * ROW_PARTS, ROW_PARTS), ROW_PARTS), :], sem)

    def issue(t, c):
        for k in range(TOP_K):
            row_copy(dest_ref[k, t], k, t).start(priority=k % 2)
        return c

    def drain(t, c):
        for k in range(TOP_K):
            row_copy(0, 0, 0).wait()
        return c

    lax.fori_loop(0, tm, issue, 0)
    lax.fori_loop(0, tm, drain, 0)

    w = wts_ref[...]
    y = jnp.zeros((tm, D_MODEL), F32)
    for k in range(TOP_K):
        y = y + w[:, k:k + 1] * _load_packed(buf.at[k], tm)
    o_ref[...] = base_ref[...] + _mod_val(g2_ref) * y


def _combine(base, wts_t, g2, ys, dest, seq, tm):
    t = base.shape[0]
    return pl.pallas_call(
        functools.partial(_combine_kernel, tm=tm),
        grid=(t // tm,),
        in_specs=[pl.BlockSpec((TOP_K, tm), lambda i: (0, i), memory_space=pltpu.SMEM),
                  pl.BlockSpec((tm, D_MODEL), lambda i: (i, 0)),
                  pl.BlockSpec((tm, TOP_K), lambda i: (i, 0)),
                  _mod_spec(g2, seq, tm),
                  pl.BlockSpec(memory_space=pl.ANY)],
        out_specs=pl.BlockSpec((tm, D_MODEL), lambda i: (i, 0)),
        out_shape=jax.ShapeDtypeStruct((t, D_MODEL), F32),
        scratch_shapes=[pltpu.VMEM((TOP_K, ROW_PARTS * tm, LANES), U32), pltpu.SemaphoreType.DMA],
        compiler_params=_cparams(("arbitrary",)),
    )(dest, base, wts_t, g2, ys)


def _mix(x, mods, past_k, past_v, s0, count0, p):
    batch, seq, _ = x.shape
    t = batch * seq
    x2 = x.reshape(t, D_MODEL)
    if seq % 256 == 0:
        tile = lambda cap: min(seq, cap)
    else:
        mods = [jnp.broadcast_to(m, (batch, seq, D_MODEL)).reshape(t, D_MODEL) for m in mods]
        tile = lambda cap: min(t, cap)
    shift1, scale1, gate1, shift2, scale2, gate2 = mods

    h = _norm_mod(x2, p["norm1_w"], scale1, shift1, seq, tile(512))
    qn = jnp.tile(p["q_norm_w"].reshape(1, -1), (1, D_MODEL // HEAD_DIM_A))
    kn = jnp.tile(p["k_norm_w"].reshape(1, -1), (1, D_MODEL // HEAD_DIM_A))
    w_in = p["w_in"]
    (q,) = _proj(h, w_in, 0, [BF16], qn, out_scale=(HEAD_DIM_A ** -0.5 * LOG2E,))
    k, kb = _proj(h, w_in, 1, [F32, BF16], kn)
    v, vb = _proj(h, w_in, 2, [F32, BF16])
    (qh,) = _proj(h, w_in, 3, [BF16])
    (fh,) = _proj(h, w_in, 4, [F32])
    (ih,) = _proj(h, w_in, 5, [BF16])
    (gh,) = _proj(h, w_in, 6, [BF16])
    (ga,) = _proj(h, w_in, 7, [BF16])
    (gb,) = _proj(h, w_in, 8, [BF16])

    if past_k is None:
        o_a = _attention_prompt(q, kb, vb, p["lam"], p["subln_w"], batch, seq)
    else:
        past = past_k.shape[1]
        lk = past + seq
        kb = jnp.concatenate([past_k.reshape(batch, past, D_MODEL).astype(BF16),
                              kb.reshape(batch, seq, D_MODEL)], axis=1).reshape(batch * lk, D_MODEL)
        vb = jnp.concatenate([past_v.reshape(batch, past, D_MODEL).astype(BF16),
                              vb.reshape(batch, seq, D_MODEL)], axis=1).reshape(batch * lk, D_MODEL)
        o_a = _attention_cached(q, kb, vb, p["lam"], p["subln_w"], batch, seq, lk)
    o_b, s_new = _hgrn(qh, fh, ih, gh, p["hgrn_lb"], p["hgrn_norm_w"], s0, batch, seq)

    base, hp, eidx, wts, pos, counts = _post(
        o_a, o_b, ga, gb, x2, gate1, scale2, shift2, gate2,
        p["w_proj_a"], p["w_proj_b"], p["w_out"], p["norm2_w"], p["w_router"], p["b_router"],
        p["w_sh_gate"], p["w_sh_up"], p["w_sh_down"], count0, seq, tile(256))
    return dict(base=base, hp=hp, eidx=eidx, wts=wts, pos=pos, counts=counts, k=k, v=v, s=s_new,
                gate2=gate2, seq=seq, tm=tile(512), shape=x.shape)


def kernel(x_prompt, x_sample, cache_attn_k, cache_attn_v, state_hgrn, c_prompt, c_sample, w_ada, b_ada, norm1_w, norm2_w, w_in, q_norm_w, k_norm_w, lambda_q1, lambda_k1, lambda_q2, lambda_k2, subln_w, hgrn_lb, hgrn_norm_w, w_proj_a, w_proj_b, w_out, w_router, b_router, w_exp_gate, w_exp_up, w_exp_down, w_sh_gate, w_sh_up, w_sh_down):
    bp, lp, _ = x_prompt.shape
    bs, ls, _ = x_sample.shape
    l = 0
    lam = (jnp.exp(jnp.sum(lambda_q1[l] * lambda_k1[l])) - jnp.exp(jnp.sum(lambda_q2[l] * lambda_k2[l]))
           + LAMBDA_INIT).astype(F32)
    row_norm = lambda w, s: math.sqrt(HEAD_DIM_A) * jnp.max(jnp.abs(w.astype(F32))) * (s * 1.01)
    knorm = row_norm(k_norm_w[l], 1.0)
    qnorm = row_norm(q_norm_w[l], HEAD_DIM_A ** -0.5 * LOG2E)
    use_bound = (2.0 * qnorm * knorm < ATTN_BOUND_MAX_GAP).astype(F32)
    lam = jnp.stack([lam, knorm, use_bound, jnp.zeros((), F32)]).reshape(1, 4)
    p = dict(
        norm1_w=norm1_w[l], norm2_w=norm2_w[l], w_in=w_in[l].astype(BF16),
        q_norm_w=q_norm_w[l], k_norm_w=k_norm_w[l], lam=lam, subln_w=subln_w[l],
        hgrn_lb=hgrn_lb, hgrn_norm_w=hgrn_norm_w[l],
        w_proj_a=w_proj_a[l].astype(BF16), w_proj_b=w_proj_b[l].astype(BF16), w_out=w_out[l].astype(BF16),
        w_router=w_router[l], b_router=b_router[l],
        w_sh_gate=w_sh_gate[l].astype(BF16), w_sh_up=w_sh_up[l].astype(BF16), w_sh_down=w_sh_down[l].astype(BF16),
    )
    mod = _ada(jnp.concatenate([c_prompt, c_sample], axis=0), w_ada[l], b_ada[l])
    mod = mod.reshape(bp + bs, 6, 1, D_MODEL)
    mods_p = [mod[:bp, j] for j in range(6)]
    mods_s = [mod[bp:, j] for j in range(6)]

    zero_state = jnp.zeros((bp, N_HEADS, HEAD_W, HEAD_W), F32)
    zero_count = jnp.zeros((N_EXPERTS, 1), F32)
    gp = _mix(x_prompt, mods_p, None, None, zero_state, zero_count, p)
    gs = _mix(x_sample, mods_s, cache_attn_k[l], cache_attn_v[l], state_hgrn[l], gp["counts"], p)
    groups = (gp, gs)

    counts = gs["counts"].reshape(N_EXPERTS).astype(I32)
    pcounts = (counts + ROW_BLOCK - 1) // ROW_BLOCK * ROW_BLOCK
    pend = jnp.cumsum(pcounts)
    pstart = pend - pcounts
    n_assign = (bp * lp + bs * ls) * TOP_K
    nb = -(-(n_assign + N_EXPERTS * (ROW_BLOCK - 1)) // (2 * ROW_BLOCK)) * 2
    blk_e = jnp.minimum(jnp.searchsorted(pend, jnp.arange(nb, dtype=I32) * ROW_BLOCK, side="right"),
                        N_EXPERTS - 1).astype(I32)
    n_used = (pend[-1] // ROW_BLOCK).astype(I32).reshape(1)
    pstart_col = pstart.astype(F32).reshape(N_EXPERTS, 1)

    xs = jnp.zeros((ROW_PARTS * nb * ROW_BLOCK, LANES), U32)
    for g in groups:
        g["dest"] = _dest(g["eidx"], g["pos"], pstart_col)
        xs = _dispatch(g["hp"], g["dest"], xs, g["tm"])
    ys = _experts(xs, blk_e, n_used, w_exp_gate[l], w_exp_up[l], w_exp_down[l])
    yp, ysm = [_combine(g["base"], g["wts"].T, g["gate2"], ys, g["dest"], g["seq"], g["tm"]).reshape(g["shape"])
               for g in groups]

    return (yp, ysm,
            gp["k"].reshape(1, bp, lp, N_HEADS, 2, HEAD_DIM_A), gp["v"].reshape(1, bp, lp, N_HEADS, HEAD_W),
            gp["s"][None],
            gs["k"].reshape(1, bs, ls, N_HEADS, 2, HEAD_DIM_A), gs["v"].reshape(1, bs, ls, N_HEADS, HEAD_W),
            gs["s"][None])
```

```python
import functools
import math

import jax
import jax.numpy as jnp
from jax import lax
from jax.experimental import pallas as pl
from jax.experimental.pallas import tpu as pltpu

F32 = jnp.float32
BF16 = jnp.bfloat16
U32 = jnp.uint32
I32 = jnp.int32

D_MODEL = 1024
N_HEADS = 8
HEAD_W = 128
HEAD_DIM_A = 64
CHUNK = 64
N_EXPERTS = 256
TOP_K = 8
N_GROUPS = 8
GROUP_W = N_EXPERTS // N_GROUPS
TOPK_GROUPS = 4
D_EXPERT = 256
ROUTED_SCALE = 2.5
EPS = 1e-6
LAMBDA_INIT = 0.8 - 0.6 * math.exp(-0.3 * 0)
LANES = 128
NEG = -1e30
LOG2E = math.log2(math.e)

ATTN_BOUND_MAX_GAP = 100.0
GLA_CHUNK = 32
ROW_BLOCK = 512
VMEM_LIMIT = 48 * 1024 * 1024


def _sigmoid(x):
    return 1.0 / (1.0 + jnp.exp(-x))


def _silu(x):
    return x * _sigmoid(x)


def _cparams(sem):
    return pltpu.CompilerParams(dimension_semantics=sem, vmem_limit_bytes=VMEM_LIMIT)


ROW_PARTS = 4


def _store_packed(ref, x, row0=0):
    m, half = x.shape[0], x.shape[1] // 2
    lo = pltpu.bitcast(x[:, :half].astype(BF16).astype(F32), U32)
    hi = pltpu.bitcast(x[:, half:].astype(BF16).astype(F32), U32)
    w = (hi & jnp.uint32(0xFFFF0000)) | (lo >> 16)
    for j in range(ROW_PARTS):
        ref[pl.ds(ROW_PARTS * row0 + j, m, stride=ROW_PARTS), :] = w[:, j * LANES:(j + 1) * LANES]


def _load_packed(ref, m, row0=0):
    parts = [ref[pl.ds(ROW_PARTS * row0 + j, m, stride=ROW_PARTS), :] for j in range(ROW_PARTS)]
    lo = [pltpu.bitcast(w << 16, F32) for w in parts]
    hi = [pltpu.bitcast(w & jnp.uint32(0xFFFF0000), F32) for w in parts]
    return jnp.concatenate(lo + hi, axis=1)


def _mod_spec(a, seq, tm):
    if a.ndim == 3:
        per_b = seq // tm
        return pl.BlockSpec((1, 1, D_MODEL), lambda i: (i // per_b, 0, 0))
    return pl.BlockSpec((tm, D_MODEL), lambda i: (i, 0))


def _mod_val(ref):
    v = ref[...]
    return v.reshape(v.shape[-2], v.shape[-1])


def _ada_kernel(c_ref, w_ref, b_ref, o_ref):
    c = _silu(c_ref[...])
    o_ref[...] = jnp.dot(c, w_ref[...], precision=lax.Precision.HIGHEST,
                         preferred_element_type=F32) + b_ref[...]


def _ada(c, w_ada, b_ada):
    n = c.shape[0]
    nj = w_ada.shape[1] // D_MODEL
    return pl.pallas_call(
        _ada_kernel,
        grid=(nj,),
        in_specs=[pl.BlockSpec((n, D_MODEL), lambda j: (0, 0)),
                  pl.BlockSpec((D_MODEL, D_MODEL), lambda j: (0, j)),
                  pl.BlockSpec((1, D_MODEL), lambda j: (0, j))],
        out_specs=pl.BlockSpec((n, D_MODEL), lambda j: (0, j)),
        out_shape=jax.ShapeDtypeStruct((n, w_ada.shape[1]), F32),
        compiler_params=_cparams(("arbitrary",)),
    )(c, w_ada, b_ada.reshape(1, -1))


def _norm_mod_kernel(x_ref, w_ref, sc_ref, sh_ref, o_ref):
    x = x_ref[...]
    y = x * lax.rsqrt(jnp.mean(x * x, axis=-1, keepdims=True) + EPS) * w_ref[...]
    o_ref[...] = (y * (1.0 + _mod_val(sc_ref)) + _mod_val(sh_ref)).astype(o_ref.dtype)


def _norm_mod(x2, w, scale, shift, seq, tm):
    t = x2.shape[0]
    return pl.pallas_call(
        _norm_mod_kernel,
        grid=(t // tm,),
        in_specs=[pl.BlockSpec((tm, D_MODEL), lambda i: (i, 0)),
                  pl.BlockSpec((1, D_MODEL), lambda i: (0, 0)),
                  _mod_spec(scale, seq, tm), _mod_spec(shift, seq, tm)],
        out_specs=pl.BlockSpec((tm, D_MODEL), lambda i: (i, 0)),
        out_shape=jax.ShapeDtypeStruct((t, D_MODEL), BF16),
        compiler_params=_cparams(("parallel",)),
    )(x2, w.reshape(1, -1), scale, shift)


def _head_norm(acc, w):
    r = lax.broadcasted_iota(I32, (HEAD_W, HEAD_W), 0) // HEAD_DIM_A
    c = lax.broadcasted_iota(I32, (HEAD_W, HEAD_W), 1) // HEAD_DIM_A
    ones = (r == c).astype(BF16)
    outs = []
    for h in range(N_HEADS):
        a = acc[:, h * HEAD_W:(h + 1) * HEAD_W]
        sq = a * a
        hi = sq.astype(BF16)
        lo = (sq - hi.astype(F32)).astype(BF16)
        ss = (jnp.dot(hi, ones, preferred_element_type=F32)
              + jnp.dot(lo, ones, preferred_element_type=F32))
        outs.append(a * lax.rsqrt(ss * (1.0 / HEAD_DIM_A) + EPS) * w[:, h * HEAD_W:(h + 1) * HEAD_W])
    return jnp.concatenate(outs, axis=1)


def _proj_kernel(h_ref, w_ref, nw_ref, *o_refs, head_norm, out_scale):
    acc = jnp.dot(h_ref[...], w_ref[...], preferred_element_type=F32)
    if head_norm:
        acc = _head_norm(acc, nw_ref[...])
    for o_ref, s in zip(o_refs, out_scale):
        o_ref[...] = (acc if s == 1.0 else acc * s).astype(o_ref.dtype)


def _proj(h, w_in_bf, group, out_dtypes, norm_w=None, out_scale=None, tm=1024):
    t = h.shape[0]
    tm = min(tm, t)
    nw = jnp.ones((1, D_MODEL), F32) if norm_w is None else norm_w
    out_scale = tuple(out_scale or (1.0,) * len(out_dtypes))
    outs = pl.pallas_call(
        functools.partial(_proj_kernel, head_norm=norm_w is not None, out_scale=out_scale),
        grid=(t // tm,),
        in_specs=[pl.BlockSpec((tm, D_MODEL), lambda i: (i, 0)),
                  pl.BlockSpec((D_MODEL, D_MODEL), lambda i: (0, group)),
                  pl.BlockSpec((1, D_MODEL), lambda i: (0, 0))],
        out_specs=[pl.BlockSpec((tm, D_MODEL), lambda i: (i, 0)) for _ in out_dtypes],
        out_shape=[jax.ShapeDtypeStruct((t, D_MODEL), dt) for dt in out_dtypes],
        compiler_params=_cparams(("parallel",)),
    )(h, w_in_bf, nw)
    return outs


def _split_maps(q):
    lane = lax.broadcasted_iota(I32, q.shape, 1)
    zero = jnp.zeros_like(q)
    return jnp.where(lane < HEAD_DIM_A, q, zero), jnp.where(lane >= HEAD_DIM_A, q, zero)


def _finish_head(acc1, l1, acc2, l2, lam, sw):
    o = acc1 / l1 - lam * (acc2 / l2)
    o = o * lax.rsqrt(jnp.mean(o * o, axis=-1, keepdims=True) + EPS) * sw
    return o * (1.0 - LAMBDA_INIT)


def _attn_kernel(q_ref, k_ref, v_ref, lam_ref, sw_ref, o_ref, mx_s, mrep_s, ls_s, acc_s,
                 *, tq, tk, hp):
    qi = pl.program_id(2)
    nf = tk // LANES
    qs = []
    for hh in range(hp):
        qs.extend(_split_maps(q_ref[:, hh * HEAD_W:(hh + 1) * HEAD_W]))
    ns = 2 * hp

    r_chunk = lax.broadcasted_iota(I32, (tq, tk), 0) // CHUNK
    c_chunk = lax.broadcasted_iota(I32, (tq, tk), 1) // CHUNK
    diag_ok = c_chunk <= r_chunk

    def scores(kt, s, masked):
        k = k_ref[pl.ds(pl.multiple_of(kt * tk, tk), tk), (s // 2) * HEAD_W:(s // 2 + 1) * HEAD_W]
        sc = lax.dot_general(qs[s], k, (((1,), (1,)), ((), ())), preferred_element_type=F32)
        return jnp.where(diag_ok, sc, NEG) if masked else sc

    def fold(x, op):
        r = x[:, :LANES]
        for j in range(1, nf):
            r = op(r, x[:, j * LANES:(j + 1) * LANES])
        return r

    def pass1(kt, first):
        for s in range(ns):
            m = fold(scores(kt, s, first), jnp.maximum)
            mx_s[s] = m if first else jnp.maximum(mx_s[s], m)

    def pass2(kt, first):
        for s in range(ns):
            m = mrep_s[s]
            p = jnp.exp2(scores(kt, s, first) - jnp.concatenate([m] * nf, axis=1))
            v = v_ref[pl.ds(pl.multiple_of(kt * tk, tk), tk), (s // 2) * HEAD_W:(s // 2 + 1) * HEAD_W]
            pv = jnp.dot(p.astype(BF16), v, preferred_element_type=F32)
            ls_s[s] = fold(p, jnp.add) if first else ls_s[s] + fold(p, jnp.add)
            acc_s[s] = pv if first else acc_s[s] + pv

    def loop(fn):
        def body(kt, c):
            fn(kt, False)
            return c

        fn(qi, True)
        lax.fori_loop(0, qi, body, 0)

    knorm = lam_ref[0, 1]
    use_bound = lam_ref[0, 2] > 0.5

    @pl.when(use_bound)
    def _():
        for s in range(ns):
            qf = qs[s].astype(F32)
            qn = jnp.sqrt(jnp.sum(qf * qf, axis=-1, keepdims=True))
            mrep_s[s] = jnp.broadcast_to(qn * knorm, (tq, LANES))

    @pl.when(jnp.logical_not(use_bound))
    def _():
        loop(pass1)
        for s in range(ns):
            mrep_s[s] = jnp.broadcast_to(jnp.max(mx_s[s], axis=-1, keepdims=True), (tq, LANES))

    loop(pass2)

    lam = lam_ref[0, 0]
    for hh in range(hp):
        l1 = jnp.sum(ls_s[2 * hh], axis=-1, keepdims=True)
        l2 = jnp.sum(ls_s[2 * hh + 1], axis=-1, keepdims=True)
        o = _finish_head(acc_s[2 * hh], l1, acc_s[2 * hh + 1], l2, lam, sw_ref[...])
        o_ref[:, hh * HEAD_W:(hh + 1) * HEAD_W] = o.astype(o_ref.dtype)


def _attention_prompt(q, kb, vb, lam, subln_w, batch, seq):
    tq = tk = min(seq, 256)
    hp = 8
    nq = seq // tq
    kern = functools.partial(_attn_kernel, tq=tq, tk=tk, hp=hp)
    w = hp * HEAD_W
    return pl.pallas_call(
        kern,
        grid=(batch, N_HEADS // hp, nq),
        in_specs=[pl.BlockSpec((tq, w), lambda b, h, i: (b * nq + i, h)),
                  pl.BlockSpec((seq, w), lambda b, h, i: (b, h)),
                  pl.BlockSpec((seq, w), lambda b, h, i: (b, h)),
                  pl.BlockSpec((1, 4), lambda b, h, i: (0, 0), memory_space=pltpu.SMEM),
                  pl.BlockSpec((1, HEAD_W), lambda b, h, i: (0, 0))],
        out_specs=pl.BlockSpec((tq, w), lambda b, h, i: (b * nq + i, h)),
        out_shape=jax.ShapeDtypeStruct((batch * seq, D_MODEL), BF16),
        scratch_shapes=[pltpu.VMEM((2 * hp, tq, LANES), F32) for _ in range(4)],
        compiler_params=_cparams(("parallel", "parallel", "arbitrary")),
    )(q, kb, vb, lam, subln_w.reshape(1, -1))


def _attn_cached_kernel(q_ref, k_ref, v_ref, lam_ref, sw_ref, o_ref, *, lq, lk):
    qpos = (lk - lq) + lax.broadcasted_iota(I32, (lq, lk), 0)
    kpos = lax.broadcasted_iota(I32, (lq, lk), 1)
    allowed = (kpos // CHUNK) <= (qpos // CHUNK)
    lam = lam_ref[0, 0]
    for h in range(N_HEADS):
        sl = slice(h * HEAD_W, (h + 1) * HEAD_W)
        k = k_ref[:, sl]
        v = v_ref[:, sl]
        accs, ls = [], []
        for qm in _split_maps(q_ref[:, sl]):
            sc = lax.dot_general(qm, k, (((1,), (1,)), ((), ())), preferred_element_type=F32)
            sc = jnp.where(allowed, sc, NEG)
            p = jnp.exp2(sc - jnp.max(sc, axis=-1, keepdims=True))
            ls.append(jnp.sum(p, axis=-1, keepdims=True))
            accs.append(jnp.dot(p.astype(BF16), v, preferred_element_type=F32))
        o = _finish_head(accs[0], ls[0], accs[1], ls[1], lam, sw_ref[...])
        o_ref[:, sl] = o.astype(o_ref.dtype)


def _attention_cached(q, kb, vb, lam, subln_w, batch, lq, lk):
    return pl.pallas_call(
        functools.partial(_attn_cached_kernel, lq=lq, lk=lk),
        grid=(batch,),
        in_specs=[pl.BlockSpec((lq, D_MODEL), lambda b: (b, 0)),
                  pl.BlockSpec((lk, D_MODEL), lambda b: (b, 0)),
                  pl.BlockSpec((lk, D_MODEL), lambda b: (b, 0)),
                  pl.BlockSpec((1, 4), lambda b: (0, 0), memory_space=pltpu.SMEM),
                  pl.BlockSpec((1, HEAD_W), lambda b: (0, 0))],
        out_specs=pl.BlockSpec((lq, D_MODEL), lambda b: (b, 0)),
        out_shape=jax.ShapeDtypeStruct((batch * lq, D_MODEL), BF16),
        compiler_params=_cparams(("parallel",)),
    )(q, kb, vb, lam, subln_w.reshape(1, -1))


def _hgrn_kernel(qh_ref, fh_ref, ih_ref, gh_ref, lb_ref, nw_ref, s0_ref, o_ref, s_out_ref,
                 st_s, qin_s, qmid_s, kmid_s, kend_s, dec_s, *, ct, c):
    t = pl.program_id(1)
    nt = pl.num_programs(1)
    nc = ct // c

    @pl.when(t == 0)
    def _():
        for h in range(N_HEADS):
            st_s[h] = s0_ref[0, h].T

    a = lb_ref[...]
    amax = jnp.max(a, axis=0, keepdims=True)
    e = jnp.exp(a - amax)
    lb = e[0:1] / jnp.sum(e, axis=0, keepdims=True)

    row = lax.broadcasted_iota(I32, (c, c), 0)
    col = lax.broadcasted_iota(I32, (c, c), 1)
    causal = col <= row
    tril = causal.astype(F32)

    for ci in range(nc):
        rows = slice(ci * c, (ci + 1) * c)
        f = lb + (1.0 - lb) * _sigmoid(fh_ref[rows, :])
        b = jnp.dot(tril, jnp.log(f), precision=lax.Precision.HIGHEST, preferred_element_type=F32)
        b_last = b[c - 1:c, :]
        b_mid = b[c // 2 - 1:c // 2, :]
        q = _silu(qh_ref[rows, :].astype(F32)) * (HEAD_W ** -0.5)
        kk = 1.0 - f
        e_dn = jnp.exp(b - b_mid)
        e_up = jnp.exp(b_mid - b)
        qm = q * e_dn
        km = kk * e_up
        qmid_s[rows, :] = qm.astype(BF16)
        kmid_s[rows, :] = km.astype(BF16)
        qin_s[rows, :] = (qm * jnp.exp(b_mid)).astype(BF16)
        kend_s[rows, :] = (km * jnp.exp(b_last - b_mid)).astype(BF16)
        dec_s[ci:ci + 1, :] = jnp.exp(b_last)

    nw = nw_ref[...]
    st = [st_s[h] for h in range(N_HEADS)]
    for ci in range(nc):
        rows = slice(ci * c, (ci + 1) * c)
        gate = _silu(gh_ref[rows, :].astype(F32))
        decay = dec_s[ci:ci + 1, :]
        for h in range(N_HEADS):
            sl = slice(h * HEAD_W, (h + 1) * HEAD_W)
            v = ih_ref[rows, sl]
            inter = lax.dot_general(qin_s[rows, sl], st[h].astype(BF16), (((1,), (1,)), ((), ())),
                                    preferred_element_type=F32)
            att = lax.dot_general(qmid_s[rows, sl], kmid_s[rows, sl], (((1,), (1,)), ((), ())),
                                  preferred_element_type=F32)
            att = jnp.where(causal, att, 0.0)
            o = inter + jnp.dot(att.astype(BF16), v, preferred_element_type=F32)
            upd = lax.dot_general(v, kend_s[rows, sl], (((0,), (0,)), ((), ())),
                                  preferred_element_type=F32)
            st[h] = decay[:, sl] * st[h] + upd
            o = o * lax.rsqrt(jnp.mean(o * o, axis=-1, keepdims=True) + EPS) * nw[:, sl]
            o_ref[rows, sl] = (o * gate[:, sl]).astype(o_ref.dtype)
    for h in range(N_HEADS):
        st_s[h] = st[h]

    @pl.when(t == nt - 1)
    def _():
        for h in range(N_HEADS):
            s_out_ref[0, h] = st_s[h].T


def _hgrn(qh, fh, ih, gh, hgrn_lb, norm_w, s0, batch, seq):
    ct = min(seq, 512)
    c = min(GLA_CHUNK, ct)
    nt = seq // ct
    kern = functools.partial(_hgrn_kernel, ct=ct, c=c)
    tok = pl.BlockSpec((ct, D_MODEL), lambda b, t: (b * nt + t, 0))
    st_spec = pl.BlockSpec((1, N_HEADS, HEAD_W, HEAD_W), lambda b, t: (b, 0, 0, 0))
    return pl.pallas_call(
        kern,
        grid=(batch, nt),
        in_specs=[tok, tok, tok, tok,
                  pl.BlockSpec((2, D_MODEL), lambda b, t: (0, 0)),
                  pl.BlockSpec((1, D_MODEL), lambda b, t: (0, 0)),
                  st_spec],
        out_specs=[tok, st_spec],
        out_shape=[jax.ShapeDtypeStruct((batch * seq, D_MODEL), BF16),
                   jax.ShapeDtypeStruct((batch, N_HEADS, HEAD_W, HEAD_W), F32)],
        scratch_shapes=[pltpu.VMEM((N_HEADS, HEAD_W, HEAD_W), F32)]
        + [pltpu.VMEM((ct, D_MODEL), BF16) for _ in range(4)]
        + [pltpu.VMEM((max(ct // c, 8), D_MODEL), F32)],
        compiler_params=_cparams(("parallel", "arbitrary")),
    )(qh, fh, ih, gh, hgrn_lb, jnp.tile(norm_w.reshape(1, -1), (1, N_HEADS)), s0)


def _route(h2, wrt_ref, brt_ref, carry_s, tm):
    nt_dims = (((1,), (1,)), ((), ()))
    w_hi = wrt_ref[0]
    w_lo = wrt_ref[1]
    h_hi = h2.astype(BF16)
    h_lo = (h2 - h_hi.astype(F32)).astype(BF16)
    logits = (lax.dot_general(w_hi, h_hi, nt_dims, preferred_element_type=F32)
              + lax.dot_general(w_hi, h_lo, nt_dims, preferred_element_type=F32)
              + lax.dot_general(w_lo, h_hi, nt_dims, preferred_element_type=F32))
    scores = _sigmoid(logits)
    biased = scores + brt_ref[...]
    big = float(2 * N_EXPERTS)

    x3 = biased.reshape(N_GROUPS, GROUP_W, tm)
    i3 = lax.broadcasted_iota(I32, x3.shape, 1).astype(F32)
    m1 = jnp.max(x3, axis=1, keepdims=True)
    i1 = jnp.min(jnp.where(x3 == m1, i3, big), axis=1, keepdims=True)
    m2 = jnp.max(jnp.where(i3 == i1, NEG, x3), axis=1, keepdims=True)
    gs = (m1 + m2).reshape(N_GROUPS, tm)

    g_iota = lax.broadcasted_iota(I32, (N_GROUPS, tm), 0).astype(F32)
    rem = gs
    gsel = jnp.zeros((N_GROUPS, tm), F32)
    for _ in range(TOPK_GROUPS):
        gm = jnp.max(rem, axis=0, keepdims=True)
        first = jnp.min(jnp.where(rem == gm, g_iota, big), axis=0, keepdims=True)
        sel = g_iota == first
        gsel = jnp.where(sel, 1.0, gsel)
        rem = jnp.where(sel, NEG, rem)
    keep = jnp.broadcast_to(gsel.reshape(N_GROUPS, 1, tm), x3.shape) > 0.5
    masked = jnp.where(keep, x3, NEG).reshape(N_EXPERTS, tm)

    e_iota = lax.broadcasted_iota(I32, (N_EXPERTS, tm), 0).astype(F32)
    onehot = jnp.zeros((N_EXPERTS, tm), F32)
    idxs, ws = [], []
    for _ in range(TOP_K):
        mk = jnp.max(masked, axis=0, keepdims=True)
        ik = jnp.min(jnp.where(masked == mk, e_iota, big), axis=0, keepdims=True)
        selk = e_iota == ik
        ws.append(jnp.sum(jnp.where(selk, scores, 0.0), axis=0, keepdims=True))
        idxs.append(ik)
        masked = jnp.where(selk, NEG, masked)
        onehot = jnp.where(selk, 1.0, onehot)
    wsum = ws[0]
    for k in range(1, TOP_K):
        wsum = wsum + ws[k]

    r = lax.broadcasted_iota(I32, (tm, tm), 0)
    cidx = lax.broadcasted_iota(I32, (tm, tm), 1)
    before = (r < cidx).astype(BF16)
    cum = jnp.dot(onehot.astype(BF16), before, preferred_element_type=F32) + carry_s[...]
    carry_s[...] = carry_s[...] + jnp.sum(onehot, axis=1, keepdims=True)

    k_iota = lax.broadcasted_iota(I32, (TOP_K, tm), 0)
    e_out = jnp.zeros((TOP_K, tm), F32)
    w_out = jnp.zeros((TOP_K, tm), F32)
    p_out = jnp.zeros((TOP_K, tm), F32)
    for k in range(TOP_K):
        pk = jnp.sum(jnp.where(e_iota == idxs[k], cum, 0.0), axis=0, keepdims=True)
        e_out = jnp.where(k_iota == k, idxs[k], e_out)
        w_out = jnp.where(k_iota == k, ws[k] / wsum * ROUTED_SCALE, w_out)
        p_out = jnp.where(k_iota == k, pk, p_out)
    return e_out, w_out, p_out


def _post_kernel(oa_ref, ob_ref, ga_ref, gb_ref, x_ref, g1_ref, sc2_ref, sh2_ref, g2_ref,
                 wpa_ref, wpb_ref, wo_ref, n2_ref, wrt_ref, brt_ref, wsg_ref, wsu_ref, wsd_ref, c0_ref,
                 base_ref, hp_ref, eidx_ref, wts_ref, pos_ref, cnt_ref, carry_s, *, tm):
    i = pl.program_id(0)

    @pl.when(i == 0)
    def _():
        carry_s[...] = c0_ref[...]

    pa = jnp.dot(oa_ref[...], wpa_ref[...], preferred_element_type=F32)
    pb = jnp.dot(ob_ref[...], wpb_ref[...], preferred_element_type=F32)
    u = _sigmoid(ga_ref[...].astype(F32)) * pa + _sigmoid(gb_ref[...].astype(F32)) * pb
    x1 = x_ref[...] + _mod_val(g1_ref) * jnp.dot(u.astype(BF16), wo_ref[...], preferred_element_type=F32)
    y = x1 * lax.rsqrt(jnp.mean(x1 * x1, axis=-1, keepdims=True) + EPS) * n2_ref[...]
    h2 = y * (1.0 + _mod_val(sc2_ref)) + _mod_val(sh2_ref)
    _store_packed(hp_ref, h2)

    hb = h2.astype(BF16)
    sg = jnp.dot(hb, wsg_ref[...], preferred_element_type=F32)
    su = jnp.dot(hb, wsu_ref[...], preferred_element_type=F32)
    shared = jnp.dot((_silu(sg) * su).astype(BF16), wsd_ref[...], preferred_element_type=F32)
    base_ref[...] = x1 + _mod_val(g2_ref) * shared

    e_out, w_out, p_out = _route(h2, wrt_ref, brt_ref, carry_s, tm)
    eidx_ref[...] = e_out.astype(I32)
    wts_ref[...] = w_out
    pos_ref[...] = p_out.astype(I32)
    cnt_ref[...] = carry_s[...]


def _post(oa, ob, ga, gb, x2, g1, sc2, sh2, g2, wpa, wpb, wo, n2, wr, br, wsg, wsu, wsd, count0, seq, tm):
    t = x2.shape[0]
    tok = lambda: pl.BlockSpec((tm, D_MODEL), lambda i: (i, 0))
    mod = lambda: _mod_spec(g1, seq, tm)
    full = lambda a: pl.BlockSpec(a.shape, lambda i: (0,) * a.ndim)
    n2 = n2.reshape(1, -1)
    wrt_hi = wr.T.astype(BF16)
    wrt = jnp.stack([wrt_hi, (wr.T - wrt_hi.astype(F32)).astype(BF16)])
    brt = br.reshape(-1, 1)
    k_out = lambda: pl.BlockSpec((TOP_K, tm), lambda i: (0, i))
    return pl.pallas_call(
        functools.partial(_post_kernel, tm=tm),
        grid=(t // tm,),
        in_specs=[tok(), tok(), tok(), tok(), tok(), mod(), mod(), mod(), mod(),
                  full(wpa), full(wpb), full(wo), full(n2), full(wrt), full(brt),
                  full(wsg), full(wsu), full(wsd), full(count0)],
        out_specs=[tok(), pl.BlockSpec((ROW_PARTS * tm, LANES), lambda i: (i, 0)),
                   k_out(), k_out(), k_out(),
                   pl.BlockSpec((N_EXPERTS, 1), lambda i: (0, 0))],
        out_shape=[jax.ShapeDtypeStruct((t, D_MODEL), F32),
                   jax.ShapeDtypeStruct((ROW_PARTS * t, LANES), U32),
                   jax.ShapeDtypeStruct((TOP_K, t), I32),
                   jax.ShapeDtypeStruct((TOP_K, t), F32),
                   jax.ShapeDtypeStruct((TOP_K, t), I32),
                   jax.ShapeDtypeStruct((N_EXPERTS, 1), F32)],
        scratch_shapes=[pltpu.VMEM((N_EXPERTS, 1), F32)],
        compiler_params=_cparams(("arbitrary",)),
    )(oa, ob, ga, gb, x2, g1, sc2, sh2, g2, wpa, wpb, wo, n2, wrt, brt, wsg, wsu, wsd, count0)


def _dest_kernel(eidx_ref, pos_ref, pstart_ref, o_ref, *, tm):
    e_iota = lax.broadcasted_iota(I32, (N_EXPERTS, tm), 0)
    k_iota = lax.broadcasted_iota(I32, (TOP_K, tm), 0)
    eidx = eidx_ref[...]
    start = jnp.zeros((TOP_K, tm), F32)
    for k in range(TOP_K):
        sel = e_iota == eidx[k:k + 1, :]
        sk = jnp.sum(jnp.where(sel, pstart_ref[...], 0.0), axis=0, keepdims=True)
        start = jnp.where(k_iota == k, sk, start)
    o_ref[...] = (start.astype(I32) + pos_ref[...]) * ROW_PARTS


def _dest(eidx, pos, pstart_col):
    t = eidx.shape[1]
    tm = min(t, 1024)
    blk = lambda: pl.BlockSpec((TOP_K, tm), lambda i: (0, i))
    return pl.pallas_call(
        functools.partial(_dest_kernel, tm=tm),
        grid=(t // tm,),
        in_specs=[blk(), blk(), pl.BlockSpec((N_EXPERTS, 1), lambda i: (0, 0))],
        out_specs=blk(),
        out_shape=jax.ShapeDtypeStruct((TOP_K, t), I32),
        compiler_params=_cparams(("parallel",)),
    )(eidx, pos, pstart_col)


def _dispatch_kernel(dest_ref, hp_ref, xs_in_ref, xs_ref, sem, *, tm):
    del xs_in_ref

    def row_copy(t, d):
        return pltpu.make_async_copy(
            hp_ref.at[pl.ds(pl.multiple_of(t * ROW_PARTS, ROW_PARTS), ROW_PARTS), :],
            xs_ref.at[pl.ds(pl.multiple_of(d, ROW_PARTS), ROW_PARTS), :], sem)

    def issue(t, c):
        for k in range(TOP_K):
            row_copy(t, dest_ref[k, t]).start(priority=k % 2)
        return c

    def drain(t, c):
        for k in range(TOP_K):
            row_copy(0, 0).wait()
        return c

    lax.fori_loop(0, tm, issue, 0)
    lax.fori_loop(0, tm, drain, 0)


def _dispatch(hp, dest, xs, tm):
    t = dest.shape[1]
    return pl.pallas_call(
        functools.partial(_dispatch_kernel, tm=tm),
        grid=(t // tm,),
        in_specs=[pl.BlockSpec((TOP_K, tm), lambda i: (0, i), memory_space=pltpu.SMEM),
                  pl.BlockSpec((ROW_PARTS * tm, LANES), lambda i: (i, 0)),
                  pl.BlockSpec(memory_space=pl.ANY)],
        out_specs=pl.BlockSpec(memory_space=pl.ANY),
        out_shape=jax.ShapeDtypeStruct(xs.shape, xs.dtype),
        scratch_shapes=[pltpu.SemaphoreType.DMA],
        input_output_aliases={2: 0},
        compiler_params=_cparams(("arbitrary",)),
    )(dest, hp, xs)


def _expert_kernel(be_ref, nu_ref, xs_ref, wg_ref, wu_ref, wd_ref, ys_ref, wg_s, wu_s, wd_s):
    i = pl.program_id(0)
    used = i < nu_ref[0]

    @pl.when(jnp.logical_and(used, jnp.logical_or(i == 0, be_ref[i] != be_ref[jnp.maximum(i - 1, 0)])))
    def _():
        wg_s[...] = wg_ref[0].astype(BF16)
        wu_s[...] = wu_ref[0].astype(BF16)
        wd_s[...] = wd_ref[0].astype(BF16)

    @pl.when(used)
    def _():
        x = _load_packed(xs_ref, ROW_BLOCK).astype(BF16)
        g = jnp.dot(x, wg_s[...], preferred_element_type=F32)
        u = jnp.dot(x, wu_s[...], preferred_element_type=F32)
        hmid = (_silu(g) * u).astype(BF16)
        _store_packed(ys_ref, jnp.dot(hmid, wd_s[...], preferred_element_type=F32))

    @pl.when(jnp.logical_not(used))
    def _():
        ys_ref[...] = jnp.zeros(ys_ref.shape, ys_ref.dtype)


def _experts(xs, blk_e, n_used, w_eg, w_eu, w_ed):
    nb = xs.shape[0] // (ROW_PARTS * ROW_BLOCK)
    rows_spec = lambda: pl.BlockSpec((ROW_PARTS * ROW_BLOCK, LANES), lambda i, be, nu: (i, 0))
    w_up = lambda: pl.BlockSpec((1, D_MODEL, D_EXPERT), lambda i, be, nu: (be[i], 0, 0))
    grid_spec = pltpu.PrefetchScalarGridSpec(
        num_scalar_prefetch=2,
        grid=(nb,),
        in_specs=[rows_spec(), w_up(), w_up(),
                  pl.BlockSpec((1, D_EXPERT, D_MODEL), lambda i, be, nu: (be[i], 0, 0))],
        out_specs=rows_spec(),
        scratch_shapes=[pltpu.VMEM((D_MODEL, D_EXPERT), BF16), pltpu.VMEM((D_MODEL, D_EXPERT), BF16),
                        pltpu.VMEM((D_EXPERT, D_MODEL), BF16)],
    )
    return pl.pallas_call(
        _expert_kernel,
        grid_spec=grid_spec,
        out_shape=jax.ShapeDtypeStruct(xs.shape, U32),
        compiler_params=_cparams(("arbitrary",)),
    )(blk_e, n_used, xs, w_eg, w_eu, w_ed)


def _combine_kernel(dest_ref, base_ref, wts_ref, g2_ref, ys_ref, o_ref, buf, sem, *, tm):
    def row_copy(d, k, t):
        return pltpu.make_async_copy(
            ys_ref.at[pl.ds(pl.multiple_of(d, ROW_PARTS), ROW_PARTS), :],
            buf.at[k, pl.ds(pl.multiple_of(t * ROW_PARTS, ROW_PARTS), ROW_PARTS), :], sem)

    def issue(t, c):
        for k in range(TOP_K):
            row_copy(dest_ref[k, t], k, t).start(priority=k % 2)
        return c

    def drain(t, c):
        for k in range(TOP_K):
            row_copy(0, 0, 0).wait()
        return c

    lax.fori_loop(0, tm, issue, 0)
    lax.fori_loop(0, tm, drain, 0)

    w = wts_ref[...]
    y = jnp.zeros((tm, D_MODEL), F32)
    for k in range(TOP_K):
        y = y + w[:, k:k + 1] * _load_packed(buf.at[k], tm)
    o_ref[...] = base_ref[...] + _mod_val(g2_ref) * y


def _combine(base, wts_t, g2, ys, dest, seq, tm):
    t = base.shape[0]
    return pl.pallas_call(
        functools.partial(_combine_kernel, tm=tm),
        grid=(t // tm,),
        in_specs=[pl.BlockSpec((TOP_K, tm), lambda i: (0, i), memory_space=pltpu.SMEM),
                  pl.BlockSpec((tm, D_MODEL), lambda i: (i, 0)),
                  pl.BlockSpec((tm, TOP_K), lambda i: (i, 0)),
                  _mod_spec(g2, seq, tm),
                  pl.BlockSpec(memory_space=pl.ANY)],
        out_specs=pl.BlockSpec((tm, D_MODEL), lambda i: (i, 0)),
        out_shape=jax.ShapeDtypeStruct((t, D_MODEL), F32),
        scratch_shapes=[pltpu.VMEM((TOP_K, ROW_PARTS * tm, LANES), U32), pltpu.SemaphoreType.DMA],
        compiler_params=_cparams(("arbitrary",)),
    )(dest, base, wts_t, g2, ys)


def _mix(x, mods, past_k, past_v, s0, count0, p):
    batch, seq, _ = x.shape
    t = batch * seq
    x2 = x.reshape(t, D_MODEL)
    if seq % 256 == 0:
        tile = lambda cap: min(seq, cap)
    else:
        mods = [jnp.broadcast_to(m, (batch, seq, D_MODEL)).reshape(t, D_MODEL) for m in mods]
        tile = lambda cap: min(t, cap)
    shift1, scale1, gate1, shift2, scale2, gate2 = mods

    h = _norm_mod(x2, p["norm1_w"], scale1, shift1, seq, tile(512))
    qn = jnp.tile(p["q_norm_w"].reshape(1, -1), (1, D_MODEL // HEAD_DIM_A))
    kn = jnp.tile(p["k_norm_w"].reshape(1, -1), (1, D_MODEL // HEAD_DIM_A))
    w_in = p["w_in"]
    (q,) = _proj(h, w_in, 0, [BF16], qn, out_scale=(HEAD_DIM_A ** -0.5 * LOG2E,))
    k, kb = _proj(h, w_in, 1, [F32, BF16], kn)
    v, vb = _proj(h, w_in, 2, [F32, BF16])
    (qh,) = _proj(h, w_in, 3, [BF16])
    (fh,) = _proj(h, w_in, 4, [F32])
    (ih,) = _proj(h, w_in, 5, [BF16])
    (gh,) = _proj(h, w_in, 6, [BF16])
    (ga,) = _proj(h, w_in, 7, [BF16])
    (gb,) = _proj(h, w_in, 8, [BF16])

    if past_k is None:
        o_a = _attention_prompt(q, kb, vb, p["lam"], p["subln_w"], batch, seq)
    else:
        past = past_k.shape[1]
        lk = past + seq
        kb = jnp.concatenate([past_k.reshape(batch, past, D_MODEL).astype(BF16),
                              kb.reshape(batch, seq, D_MODEL)], axis=1).reshape(batch * lk, D_MODEL)
        vb = jnp.concatenate([past_v.reshape(batch, past, D_MODEL).astype(BF16),
                              vb.reshape(batch, seq, D_MODEL)], axis=1).reshape(batch * lk, D_MODEL)
        o_a = _attention_cached(q, kb, vb, p["lam"], p["subln_w"], batch, seq, lk)
    o_b, s_new = _hgrn(qh, fh, ih, gh, p["hgrn_lb"], p["hgrn_norm_w"], s0, batch, seq)

    base, hp, eidx, wts, pos, counts = _post(
        o_a, o_b, ga, gb, x2, gate1, scale2, shift2, gate2,
        p["w_proj_a"], p["w_proj_b"], p["w_out"], p["norm2_w"], p["w_router"], p["b_router"],
        p["w_sh_gate"], p["w_sh_up"], p["w_sh_down"], count0, seq, tile(256))
    return dict(base=base, hp=hp, eidx=eidx, wts=wts, pos=pos, counts=counts, k=k, v=v, s=s_new,
                gate2=gate2, seq=seq, tm=tile(512), shape=x.shape)


def kernel(x_prompt, x_sample, cache_attn_k, cache_attn_v, state_hgrn, c_prompt, c_sample, w_ada, b_ada, norm1_w, norm2_w, w_in, q_norm_w, k_norm_w, lambda_q1, lambda_k1, lambda_q2, lambda_k2, subln_w, hgrn_lb, hgrn_norm_w, w_proj_a, w_proj_b, w_out, w_router, b_router, w_exp_gate, w_exp_up, w_exp_down, w_sh_gate, w_sh_up, w_sh_down):
    bp, lp, _ = x_prompt.shape
    bs, ls, _ = x_sample.shape
    l = 0
    lam = (jnp.exp(jnp.sum(lambda_q1[l] * lambda_k1[l])) - jnp.exp(jnp.sum(lambda_q2[l] * lambda_k2[l]))
           + LAMBDA_INIT).astype(F32)
    row_norm = lambda w, s: math.sqrt(HEAD_DIM_A) * jnp.max(jnp.abs(w.astype(F32))) * (s * 1.01)
    knorm = row_norm(k_norm_w[l], 1.0)
    qnorm = row_norm(q_norm_w[l], HEAD_DIM_A ** -0.5 * LOG2E)
    use_bound = (2.0 * qnorm * knorm < ATTN_BOUND_MAX_GAP).astype(F32)
    lam = jnp.stack([lam, knorm, use_bound, jnp.zeros((), F32)]).reshape(1, 4)
    p = dict(
        norm1_w=norm1_w[l], norm2_w=norm2_w[l], w_in=w_in[l].astype(BF16),
        q_norm_w=q_norm_w[l], k_norm_w=k_norm_w[l], lam=lam, subln_w=subln_w[l],
        hgrn_lb=hgrn_lb, hgrn_norm_w=hgrn_norm_w[l],
        w_proj_a=w_proj_a[l].astype(BF16), w_proj_b=w_proj_b[l].astype(BF16), w_out=w_out[l].astype(BF16),
        w_router=w_router[l], b_router=b_router[l],
        w_sh_gate=w_sh_gate[l].astype(BF16), w_sh_up=w_sh_up[l].astype(BF16), w_sh_down=w_sh_down[l].astype(BF16),
    )
    mod = _ada(jnp.concatenate([c_prompt, c_sample], axis=0), w_ada[l], b_ada[l])
    mod = mod.reshape(bp + bs, 6, 1, D_MODEL)
    mods_p = [mod[:bp, j] for j in range(6)]
    mods_s = [mod[bp:, j] for j in range(6)]

    zero_state = jnp.zeros((bp, N_HEADS, HEAD_W, HEAD_W), F32)
    zero_count = jnp.zeros((N_EXPERTS, 1), F32)
    gp = _mix(x_prompt, mods_p, None, None, zero_state, zero_count, p)
    gs = _mix(x_sample, mods_s, cache_attn_k[l], cache_attn_v[l], state_hgrn[l], gp["counts"], p)
    groups = (gp, gs)

    counts = gs["counts"].reshape(N_EXPERTS).astype(I32)
    pcounts = (counts + ROW_BLOCK - 1) // ROW_BLOCK * ROW_BLOCK
    pend = jnp.cumsum(pcounts)
    pstart = pend - pcounts
    n_assign = (bp * lp + bs * ls) * TOP_K
    nb = -(-(n_assign + N_EXPERTS * (ROW_BLOCK - 1)) // ROW_BLOCK)
    blk_e = jnp.minimum(jnp.searchsorted(pend, jnp.arange(nb, dtype=I32) * ROW_BLOCK, side="right"),
                        N_EXPERTS - 1).astype(I32)
    n_used = (pend[-1] // ROW_BLOCK).astype(I32).reshape(1)
    pstart_col = pstart.astype(F32).reshape(N_EXPERTS, 1)

    xs = jnp.zeros((ROW_PARTS * nb * ROW_BLOCK, LANES), U32)
    for g in groups:
        g["dest"] = _dest(g["eidx"], g["pos"], pstart_col)
        xs = _dispatch(g["hp"], g["dest"], xs, g["tm"])
    ys = _experts(xs, blk_e, n_used, w_exp_gate[l], w_exp_up[l], w_exp_down[l])
    yp, ysm = [_combine(g["base"], g["wts"].T, g["gate2"], ys, g["dest"], g["seq"], g["tm"]).reshape(g["shape"])
               for g in groups]

    return (yp, ysm,
            gp["k"].reshape(1, bp, lp, N_HEADS, 2, HEAD_DIM_A), gp["v"].reshape(1, bp, lp, N_HEADS, HEAD_W),
            gp["s"][None],
            gs["k"].reshape(1, bs, ls, N_HEADS, 2, HEAD_DIM_A), gs["v"].reshape(1, bs, ls, N_HEADS, HEAD_W),
            gs["s"][None])
```

```python
import functools
import math

import jax
import jax.numpy as jnp
from jax import lax
from jax.experimental import pallas as pl
from jax.experimental.pallas import tpu as pltpu

F32 = jnp.float32
BF16 = jnp.bfloat16
U32 = jnp.uint32
I32 = jnp.int32

D_MODEL = 1024
N_HEADS = 8
HEAD_W = 128
HEAD_DIM_A = 64
CHUNK = 64
N_EXPERTS = 256
TOP_K = 8
N_GROUPS = 8
GROUP_W = N_EXPERTS // N_GROUPS
TOPK_GROUPS = 4
D_EXPERT = 256
ROUTED_SCALE = 2.5
EPS = 1e-6
LAMBDA_INIT = 0.8 - 0.6 * math.exp(-0.3 * 0)
LANES = 128
NEG = -1e30
LOG2E = math.log2(math.e)

ATTN_BOUND_MAX_GAP = 100.0
GLA_CHUNK = 32
ROW_BLOCK = 512
VMEM_LIMIT = 48 * 1024 * 1024


def _sigmoid(x):
    return 1.0 / (1.0 + jnp.exp(-x))


def _silu(x):
    return x * _sigmoid(x)


def _cparams(sem):
    return pltpu.CompilerParams(dimension_semantics=sem, vmem_limit_bytes=VMEM_LIMIT)


ROW_PARTS = 4


def _store_packed(ref, x, row0=0):
    m, half = x.shape[0], x.shape[1] // 2
    lo = pltpu.bitcast(x[:, :half].astype(BF16).astype(F32), U32)
    hi = pltpu.bitcast(x[:, half:].astype(BF16).astype(F32), U32)
    w = (hi & jnp.uint32(0xFFFF0000)) | (lo >> 16)
    for j in range(ROW_PARTS):
        ref[pl.ds(ROW_PARTS * row0 + j, m, stride=ROW_PARTS), :] = w[:, j * LANES:(j + 1) * LANES]


def _load_packed(ref, m, row0=0):
    parts = [ref[pl.ds(ROW_PARTS * row0 + j, m, stride=ROW_PARTS), :] for j in range(ROW_PARTS)]
    lo = [pltpu.bitcast(w << 16, F32) for w in parts]
    hi = [pltpu.bitcast(w & jnp.uint32(0xFFFF0000), F32) for w in parts]
    return jnp.concatenate(lo + hi, axis=1)


def _mod_spec(a, seq, tm):
    if a.ndim == 3:
        per_b = seq // tm
        return pl.BlockSpec((1, 1, D_MODEL), lambda i: (i // per_b, 0, 0))
    return pl.BlockSpec((tm, D_MODEL), lambda i: (i, 0))


def _mod_val(ref):
    v = ref[...]
    return v.reshape(v.shape[-2], v.shape[-1])


def _ada_kernel(c_ref, w_ref, b_ref, o_ref):
    c = _silu(c_ref[...])
    o_ref[...] = jnp.dot(c, w_ref[...], precision=lax.Precision.HIGHEST,
                         preferred_element_type=F32) + b_ref[...]


def _ada(c, w_ada, b_ada):
    n = c.shape[0]
    nj = w_ada.shape[1] // D_MODEL
    return pl.pallas_call(
        _ada_kernel,
        grid=(nj,),
        in_specs=[pl.BlockSpec((n, D_MODEL), lambda j: (0, 0)),
                  pl.BlockSpec((D_MODEL, D_MODEL), lambda j: (0, j)),
                  pl.BlockSpec((1, D_MODEL), lambda j: (0, j))],
        out_specs=pl.BlockSpec((n, D_MODEL), lambda j: (0, j)),
        out_shape=jax.ShapeDtypeStruct((n, w_ada.shape[1]), F32),
        compiler_params=_cparams(("arbitrary",)),
    )(c, w_ada, b_ada.reshape(1, -1))


def _norm_mod_kernel(x_ref, w_ref, sc_ref, sh_ref, o_ref):
    x = x_ref[...]
    y = x * lax.rsqrt(jnp.mean(x * x, axis=-1, keepdims=True) + EPS) * w_ref[...]
    o_ref[...] = (y * (1.0 + _mod_val(sc_ref)) + _mod_val(sh_ref)).astype(o_ref.dtype)


def _norm_mod(x2, w, scale, shift, seq, tm):
    t = x2.shape[0]
    return pl.pallas_call(
        _norm_mod_kernel,
        grid=(t // tm,),
        in_specs=[pl.BlockSpec((tm, D_MODEL), lambda i: (i, 0)),
                  pl.BlockSpec((1, D_MODEL), lambda i: (0, 0)),
                  _mod_spec(scale, seq, tm), _mod_spec(shift, seq, tm)],
        out_specs=pl.BlockSpec((tm, D_MODEL), lambda i: (i, 0)),
        out_shape=jax.ShapeDtypeStruct((t, D_MODEL), BF16),
        compiler_params=_cparams(("parallel",)),
    )(x2, w.reshape(1, -1), scale, shift)


def _head_norm(acc, w):
    r = lax.broadcasted_iota(I32, (HEAD_W, HEAD_W), 0) // HEAD_DIM_A
    c = lax.broadcasted_iota(I32, (HEAD_W, HEAD_W), 1) // HEAD_DIM_A
    ones = (r == c).astype(BF16)
    outs = []
    for h in range(N_HEADS):
        a = acc[:, h * HEAD_W:(h + 1) * HEAD_W]
        sq = a * a
        hi = sq.astype(BF16)
        lo = (sq - hi.astype(F32)).astype(BF16)
        ss = (jnp.dot(hi, ones, preferred_element_type=F32)
              + jnp.dot(lo, ones, preferred_element_type=F32))
        outs.append(a * lax.rsqrt(ss * (1.0 / HEAD_DIM_A) + EPS) * w[:, h * HEAD_W:(h + 1) * HEAD_W])
    return jnp.concatenate(outs, axis=1)


N_MAPS = D_MODEL // HEAD_DIM_A


def _proj_kernel(h_ref, w_ref, nw_ref, *o_refs, head_norm, out_scale, split_first):
    acc = jnp.dot(h_ref[...], w_ref[...], preferred_element_type=F32)
    if head_norm:
        acc = _head_norm(acc, nw_ref[...])
    for n, (o_ref, s) in enumerate(zip(o_refs, out_scale)):
        val = (acc if s == 1.0 else acc * s).astype(o_ref.dtype)
        if split_first and n == 0:
            for g in range(N_MAPS):
                o_ref[pl.ds(g, val.shape[0], stride=N_MAPS), :] = val[:, g * HEAD_DIM_A:(g + 1) * HEAD_DIM_A]
        else:
            o_ref[...] = val


def _proj(h, w_in_bf, group, out_dtypes, norm_w=None, out_scale=None, split_first=False, tm=1024):
    t = h.shape[0]
    tm = min(tm, t)
    nw = jnp.ones((1, D_MODEL), F32) if norm_w is None else norm_w
    out_scale = tuple(out_scale or (1.0,) * len(out_dtypes))
    specs = [pl.BlockSpec((tm, D_MODEL), lambda i: (i, 0)) for _ in out_dtypes]
    shapes = [jax.ShapeDtypeStruct((t, D_MODEL), dt) for dt in out_dtypes]
    if split_first:
        specs[0] = pl.BlockSpec((N_MAPS * tm, HEAD_DIM_A), lambda i: (i, 0))
        shapes[0] = jax.ShapeDtypeStruct((N_MAPS * t, HEAD_DIM_A), out_dtypes[0])
    outs = pl.pallas_call(
        functools.partial(_proj_kernel, head_norm=norm_w is not None, out_scale=out_scale,
                          split_first=split_first),
        grid=(t // tm,),
        in_specs=[pl.BlockSpec((tm, D_MODEL), lambda i: (i, 0)),
                  pl.BlockSpec((D_MODEL, D_MODEL), lambda i: (0, group)),
                  pl.BlockSpec((1, D_MODEL), lambda i: (0, 0))],
        out_specs=specs,
        out_shape=shapes,
        compiler_params=_cparams(("parallel",)),
    )(h, w_in_bf, nw)
    return outs


def _split_maps(q):
    lane = lax.broadcasted_iota(I32, q.shape, 1)
    zero = jnp.zeros_like(q)
    return jnp.where(lane < HEAD_DIM_A, q, zero), jnp.where(lane >= HEAD_DIM_A, q, zero)


def _finish_head(acc1, l1, acc2, l2, lam, sw):
    o = acc1 / l1 - lam * (acc2 / l2)
    o = o * lax.rsqrt(jnp.mean(o * o, axis=-1, keepdims=True) + EPS) * sw
    return o * (1.0 - LAMBDA_INIT)


def _attn_kernel(q_ref, k_ref, v_ref, lam_ref, sw_ref, o_ref, mx_s, mrep_s, ls_s, acc_s,
                 *, tq, tk, hp):
    qi = pl.program_id(2)
    nf = tk // LANES
    qs = []
    for hh in range(hp):
        qs.extend(_split_maps(q_ref[:, hh * HEAD_W:(hh + 1) * HEAD_W]))
    ns = 2 * hp

    r_chunk = lax.broadcasted_iota(I32, (tq, tk), 0) // CHUNK
    c_chunk = lax.broadcasted_iota(I32, (tq, tk), 1) // CHUNK
    diag_ok = c_chunk <= r_chunk

    def scores(kt, s, masked):
        k = k_ref[pl.ds(pl.multiple_of(kt * tk, tk), tk), (s // 2) * HEAD_W:(s // 2 + 1) * HEAD_W]
        sc = lax.dot_general(qs[s], k, (((1,), (1,)), ((), ())), preferred_element_type=F32)
        return jnp.where(diag_ok, sc, NEG) if masked else sc

    def fold(x, op):
        r = x[:, :LANES]
        for j in range(1, nf):
            r = op(r, x[:, j * LANES:(j + 1) * LANES])
        return r

    def pass1(kt, first):
        for s in range(ns):
            m = fold(scores(kt, s, first), jnp.maximum)
            mx_s[s] = m if first else jnp.maximum(mx_s[s], m)

    def pass2(kt, first):
        for s in range(ns):
            m = mrep_s[s]
            p = jnp.exp2(scores(kt, s, first) - jnp.concatenate([m] * nf, axis=1))
            v = v_ref[pl.ds(pl.multiple_of(kt * tk, tk), tk), (s // 2) * HEAD_W:(s // 2 + 1) * HEAD_W]
            pv = jnp.dot(p.astype(BF16), v, preferred_element_type=F32)
            ls_s[s] = fold(p, jnp.add) if first else ls_s[s] + fold(p, jnp.add)
            acc_s[s] = pv if first else acc_s[s] + pv

    def loop(fn):
        def body(kt, c):
            fn(kt, False)
            return c

        fn(qi, True)
        lax.fori_loop(0, qi, body, 0)

    knorm = lam_ref[0, 1]
    use_bound = lam_ref[0, 2] > 0.5

    @pl.when(use_bound)
    def _():
        for s in range(ns):
            qf = qs[s].astype(F32)
            qn = jnp.sqrt(jnp.sum(qf * qf, axis=-1, keepdims=True))
            mrep_s[s] = jnp.broadcast_to(qn * knorm, (tq, LANES))

    @pl.when(jnp.logical_not(use_bound))
    def _():
        loop(pass1)
        for s in range(ns):
            mrep_s[s] = jnp.broadcast_to(jnp.max(mx_s[s], axis=-1, keepdims=True), (tq, LANES))

    loop(pass2)

    lam = lam_ref[0, 0]
    for hh in range(hp):
        l1 = jnp.sum(ls_s[2 * hh], axis=-1, keepdims=True)
        l2 = jnp.sum(ls_s[2 * hh + 1], axis=-1, keepdims=True)
        o = _finish_head(acc_s[2 * hh], l1, acc_s[2 * hh + 1], l2, lam, sw_ref[...])
        o_ref[:, hh * HEAD_W:(hh + 1) * HEAD_W] = o.astype(o_ref.dtype)


def _attention_prompt(q, kb, vb, lam, subln_w, batch, seq):
    tq = tk = min(seq, 256)
    hp = 8
    nq = seq // tq
    kern = functools.partial(_attn_kernel, tq=tq, tk=tk, hp=hp)
    w = hp * HEAD_W
    return pl.pallas_call(
        kern,
        grid=(batch, N_HEADS // hp, nq),
        in_specs=[pl.BlockSpec((tq, w), lambda b, h, i: (b * nq + i, h)),
                  pl.BlockSpec((seq, w), lambda b, h, i: (b, h)),
                  pl.BlockSpec((seq, w), lambda b, h, i: (b, h)),
                  pl.BlockSpec((1, 4), lambda b, h, i: (0, 0), memory_space=pltpu.SMEM),
                  pl.BlockSpec((1, HEAD_W), lambda b, h, i: (0, 0))],
        out_specs=pl.BlockSpec((tq, w), lambda b, h, i: (b * nq + i, h)),
        out_shape=jax.ShapeDtypeStruct((batch * seq, D_MODEL), BF16),
        scratch_shapes=[pltpu.VMEM((2 * hp, tq, LANES), F32) for _ in range(4)],
        compiler_params=_cparams(("parallel", "parallel", "arbitrary")),
    )(q, kb, vb, lam, subln_w.reshape(1, -1))


def _attn_cached_kernel(q_ref, k_ref, v_ref, lam_ref, sw_ref, o_ref, *, lq, lk):
    qpos = (lk - lq) + lax.broadcasted_iota(I32, (lq, lk), 0)
    kpos = lax.broadcasted_iota(I32, (lq, lk), 1)
    allowed = (kpos // CHUNK) <= (qpos // CHUNK)
    lam = lam_ref[0, 0]
    for h in range(N_HEADS):
        sl = slice(h * HEAD_W, (h + 1) * HEAD_W)
        k = k_ref[:, sl]
        v = v_ref[:, sl]
        accs, ls = [], []
        for qm in _split_maps(q_ref[:, sl]):
            sc = lax.dot_general(qm, k, (((1,), (1,)), ((), ())), preferred_element_type=F32)
            sc = jnp.where(allowed, sc, NEG)
            p = jnp.exp2(sc - jnp.max(sc, axis=-1, keepdims=True))
            ls.append(jnp.sum(p, axis=-1, keepdims=True))
            accs.append(jnp.dot(p.astype(BF16), v, preferred_element_type=F32))
        o = _finish_head(accs[0], ls[0], accs[1], ls[1], lam, sw_ref[...])
        o_ref[:, sl] = o.astype(o_ref.dtype)


def _attention_cached(q, kb, vb, lam, subln_w, batch, lq, lk):
    return pl.pallas_call(
        functools.partial(_attn_cached_kernel, lq=lq, lk=lk),
        grid=(batch,),
        in_specs=[pl.BlockSpec((lq, D_MODEL), lambda b: (b, 0)),
                  pl.BlockSpec((lk, D_MODEL), lambda b: (b, 0)),
                  pl.BlockSpec((lk, D_MODEL), lambda b: (b, 0)),
                  pl.BlockSpec((1, 4), lambda b: (0, 0), memory_space=pltpu.SMEM),
                  pl.BlockSpec((1, HEAD_W), lambda b: (0, 0))],
        out_specs=pl.BlockSpec((lq, D_MODEL), lambda b: (b, 0)),
        out_shape=jax.ShapeDtypeStruct((batch * lq, D_MODEL), BF16),
        compiler_params=_cparams(("parallel",)),
    )(q, kb, vb, lam, subln_w.reshape(1, -1))


def _hgrn_kernel(qh_ref, fh_ref, ih_ref, gh_ref, lb_ref, nw_ref, s0_ref, o_ref, s_out_ref,
                 st_s, qin_s, qmid_s, kmid_s, kend_s, dec_s, *, ct, c):
    t = pl.program_id(1)
    nt = pl.num_programs(1)
    nc = ct // c

    @pl.when(t == 0)
    def _():
        for h in range(N_HEADS):
            st_s[h] = s0_ref[0, h].T

    a = lb_ref[...]
    amax = jnp.max(a, axis=0, keepdims=True)
    e = jnp.exp(a - amax)
    lb = e[0:1] / jnp.sum(e, axis=0, keepdims=True)

    row = lax.broadcasted_iota(I32, (c, c), 0)
    col = lax.broadcasted_iota(I32, (c, c), 1)
    causal = col <= row
    tril = causal.astype(F32)

    for ci in range(nc):
        rows = slice(ci * c, (ci + 1) * c)
        f = lb + (1.0 - lb) * _sigmoid(fh_ref[rows, :])
        b = jnp.dot(tril, jnp.log(f), precision=lax.Precision.HIGHEST, preferred_element_type=F32)
        b_last = b[c - 1:c, :]
        b_mid = b[c // 2 - 1:c // 2, :]
        q = _silu(qh_ref[rows, :].astype(F32)) * (HEAD_W ** -0.5)
        kk = 1.0 - f
        e_dn = jnp.exp(b - b_mid)
        e_up = jnp.exp(b_mid - b)
        qm = q * e_dn
        km = kk * e_up
        qmid_s[rows, :] = qm.astype(BF16)
        kmid_s[rows, :] = km.astype(BF16)
        qin_s[rows, :] = (qm * jnp.exp(b_mid)).astype(BF16)
        kend_s[rows, :] = (km * jnp.exp(b_last - b_mid)).astype(BF16)
        dec_s[ci:ci + 1, :] = jnp.exp(b_last)

    nw = nw_ref[...]
    st = [st_s[h] for h in range(N_HEADS)]
    for ci in range(nc):
        rows = slice(ci * c, (ci + 1) * c)
        gate = _silu(gh_ref[rows, :].astype(F32))
        decay = dec_s[ci:ci + 1, :]
        for h in range(N_HEADS):
            sl = slice(h * HEAD_W, (h + 1) * HEAD_W)
            v = ih_ref[rows, sl]
            inter = lax.dot_general(qin_s[rows, sl], st[h].astype(BF16), (((1,), (1,)), ((), ())),
                                    preferred_element_type=F32)
            att = lax.dot_general(qmid_s[rows, sl], kmid_s[rows, sl], (((1,), (1,)), ((), ())),
                                  preferred_element_type=F32)
            att = jnp.where(causal, att, 0.0)
            o = inter + jnp.dot(att.astype(BF16), v, preferred_element_type=F32)
            upd = lax.dot_general(v, kend_s[rows, sl], (((0,), (0,)), ((), ())),
                                  preferred_element_type=F32)
            st[h] = decay[:, sl] * st[h] + upd
            o = o * lax.rsqrt(jnp.mean(o * o, axis=-1, keepdims=True) + EPS) * nw[:, sl]
            o_ref[rows, sl] = (o * gate[:, sl]).astype(o_ref.dtype)
    for h in range(N_HEADS):
        st_s[h] = st[h]

    @pl.when(t == nt - 1)
    def _():
        for h in range(N_HEADS):
            s_out_ref[0, h] = st_s[h].T


def _hgrn(qh, fh, ih, gh, hgrn_lb, norm_w, s0, batch, seq):
    ct = min(seq, 512)
    c = min(GLA_CHUNK, ct)
    nt = seq // ct
    kern = functools.partial(_hgrn_kernel, ct=ct, c=c)
    tok = pl.BlockSpec((ct, D_MODEL), lambda b, t: (b * nt + t, 0))
    st_spec = pl.BlockSpec((1, N_HEADS, HEAD_W, HEAD_W), lambda b, t: (b, 0, 0, 0))
    return pl.pallas_call(
        kern,
        grid=(batch, nt),
        in_specs=[tok, tok, tok, tok,
                  pl.BlockSpec((2, D_MODEL), lambda b, t: (0, 0)),
                  pl.BlockSpec((1, D_MODEL), lambda b, t: (0, 0)),
                  st_spec],
        out_specs=[tok, st_spec],
        out_shape=[jax.ShapeDtypeStruct((batch * seq, D_MODEL), BF16),
                   jax.ShapeDtypeStruct((batch, N_HEADS, HEAD_W, HEAD_W), F32)],
        scratch_shapes=[pltpu.VMEM((N_HEADS, HEAD_W, HEAD_W), F32)]
        + [pltpu.VMEM((ct, D_MODEL), BF16) for _ in range(4)]
        + [pltpu.VMEM((max(ct // c, 8), D_MODEL), F32)],
        compiler_params=_cparams(("parallel", "arbitrary")),
    )(qh, fh, ih, gh, hgrn_lb, jnp.tile(norm_w.reshape(1, -1), (1, N_HEADS)), s0)


def _route(h2, wrt_ref, brt_ref, carry_s, tm):
    nt_dims = (((1,), (1,)), ((), ()))
    w_hi = wrt_ref[0]
    w_lo = wrt_ref[1]
    h_hi = h2.astype(BF16)
    h_lo = (h2 - h_hi.astype(F32)).astype(BF16)
    logits = (lax.dot_general(w_hi, h_hi, nt_dims, preferred_element_type=F32)
              + lax.dot_general(w_hi, h_lo, nt_dims, preferred_element_type=F32)
              + lax.dot_general(w_lo, h_hi, nt_dims, preferred_element_type=F32))
    scores = _sigmoid(logits)
    biased = scores + brt_ref[...]
    big = float(2 * N_EXPERTS)

    x3 = biased.reshape(N_GROUPS, GROUP_W, tm)
    i3 = lax.broadcasted_iota(I32, x3.shape, 1).astype(F32)
    m1 = jnp.max(x3, axis=1, keepdims=True)
    i1 = jnp.min(jnp.where(x3 == m1, i3, big), axis=1, keepdims=True)
    m2 = jnp.max(jnp.where(i3 == i1, NEG, x3), axis=1, keepdims=True)
    gs = (m1 + m2).reshape(N_GROUPS, tm)

    g_iota = lax.broadcasted_iota(I32, (N_GROUPS, tm), 0).astype(F32)
    rem = gs
    gsel = jnp.zeros((N_GROUPS, tm), F32)
    for _ in range(TOPK_GROUPS):
        gm = jnp.max(rem, axis=0, keepdims=True)
        first = jnp.min(jnp.where(rem == gm, g_iota, big), axis=0, keepdims=True)
        sel = g_iota == first
        gsel = jnp.where(sel, 1.0, gsel)
        rem = jnp.where(sel, NEG, rem)
    keep = jnp.broadcast_to(gsel.reshape(N_GROUPS, 1, tm), x3.shape) > 0.5
    masked = jnp.where(keep, x3, NEG).reshape(N_EXPERTS, tm)

    e_iota = lax.broadcasted_iota(I32, (N_EXPERTS, tm), 0).astype(F32)
    onehot = jnp.zeros((N_EXPERTS, tm), F32)
    idxs, ws = [], []
    for _ in range(TOP_K):
        mk = jnp.max(masked, axis=0, keepdims=True)
        ik = jnp.min(jnp.where(masked == mk, e_iota, big), axis=0, keepdims=True)
        selk = e_iota == ik
        ws.append(jnp.sum(jnp.where(selk, scores, 0.0), axis=0, keepdims=True))
        idxs.append(ik)
        masked = jnp.where(selk, NEG, masked)
        onehot = jnp.where(selk, 1.0, onehot)
    wsum = ws[0]
    for k in range(1, TOP_K):
        wsum = wsum + ws[k]

    r = lax.broadcasted_iota(I32, (tm, tm), 0)
    cidx = lax.broadcasted_iota(I32, (tm, tm), 1)
    before = (r < cidx).astype(BF16)
    cum = jnp.dot(onehot.astype(BF16), before, preferred_element_type=F32) + carry_s[...]
    carry_s[...] = carry_s[...] + jnp.sum(onehot, axis=1, keepdims=True)

    k_iota = lax.broadcasted_iota(I32, (TOP_K, tm), 0)
    e_out = jnp.zeros((TOP_K, tm), F32)
    w_out = jnp.zeros((TOP_K, tm), F32)
    p_out = jnp.zeros((TOP_K, tm), F32)
    for k in range(TOP_K):
        pk = jnp.sum(jnp.where(e_iota == idxs[k], cum, 0.0), axis=0, keepdims=True)
        e_out = jnp.where(k_iota == k, idxs[k], e_out)
        w_out = jnp.where(k_iota == k, ws[k] / wsum * ROUTED_SCALE, w_out)
        p_out = jnp.where(k_iota == k, pk, p_out)
    return e_out, w_out, p_out


def _post_kernel(oa_ref, ob_ref, ga_ref, gb_ref, x_ref, g1_ref, sc2_ref, sh2_ref, g2_ref,
                 wpa_ref, wpb_ref, wo_ref, n2_ref, wrt_ref, brt_ref, wsg_ref, wsu_ref, wsd_ref, c0_ref,
                 base_ref, hp_ref, eidx_ref, wts_ref, pos_ref, cnt_ref, carry_s, *, tm):
    i = pl.program_id(0)

    @pl.when(i == 0)
    def _():
        carry_s[...] = c0_ref[...]

    pa = jnp.dot(oa_ref[...], wpa_ref[...], preferred_element_type=F32)
    pb = jnp.dot(ob_ref[...], wpb_ref[...], preferred_element_type=F32)
    u = _sigmoid(ga_ref[...].astype(F32)) * pa + _sigmoid(gb_ref[...].astype(F32)) * pb
    x1 = x_ref[...] + _mod_val(g1_ref) * jnp.dot(u.astype(BF16), wo_ref[...], preferred_element_type=F32)
    y = x1 * lax.rsqrt(jnp.mean(x1 * x1, axis=-1, keepdims=True) + EPS) * n2_ref[...]
    h2 = y * (1.0 + _mod_val(sc2_ref)) + _mod_val(sh2_ref)
    _store_packed(hp_ref, h2)

    hb = h2.astype(BF16)
    sg = jnp.dot(hb, wsg_ref[...], preferred_element_type=F32)
    su = jnp.dot(hb, wsu_ref[...], preferred_element_type=F32)
    shared = jnp.dot((_silu(sg) * su).astype(BF16), wsd_ref[...], preferred_element_type=F32)
    base_ref[...] = x1 + _mod_val(g2_ref) * shared

    e_out, w_out, p_out = _route(h2, wrt_ref, brt_ref, carry_s, tm)
    eidx_ref[...] = e_out.astype(I32)
    wts_ref[...] = w_out
    pos_ref[...] = p_out.astype(I32)
    cnt_ref[...] = carry_s[...]


def _post(oa, ob, ga, gb, x2, g1, sc2, sh2, g2, wpa, wpb, wo, n2, wr, br, wsg, wsu, wsd, count0, seq, tm):
    t = x2.shape[0]
    tok = lambda: pl.BlockSpec((tm, D_MODEL), lambda i: (i, 0))
    mod = lambda: _mod_spec(g1, seq, tm)
    full = lambda a: pl.BlockSpec(a.shape, lambda i: (0,) * a.ndim)
    n2 = n2.reshape(1, -1)
    wrt_hi = wr.T.astype(BF16)
    wrt = jnp.stack([wrt_hi, (wr.T - wrt_hi.astype(F32)).astype(BF16)])
    brt = br.reshape(-1, 1)
    k_out = lambda: pl.BlockSpec((TOP_K, tm), lambda i: (0, i))
    return pl.pallas_call(
        functools.partial(_post_kernel, tm=tm),
        grid=(t // tm,),
        in_specs=[tok(), tok(), tok(), tok(), tok(), mod(), mod(), mod(), mod(),
                  full(wpa), full(wpb), full(wo), full(n2), full(wrt), full(brt),
                  full(wsg), full(wsu), full(wsd), full(count0)],
        out_specs=[tok(), pl.BlockSpec((ROW_PARTS * tm, LANES), lambda i: (i, 0)),
                   k_out(), k_out(), k_out(),
                   pl.BlockSpec((N_EXPERTS, 1), lambda i: (0, 0))],
        out_shape=[jax.ShapeDtypeStruct((t, D_MODEL), F32),
                   jax.ShapeDtypeStruct((ROW_PARTS * t, LANES), U32),
                   jax.ShapeDtypeStruct((TOP_K, t), I32),
                   jax.ShapeDtypeStruct((TOP_K, t), F32),
                   jax.ShapeDtypeStruct((TOP_K, t), I32),
                   jax.ShapeDtypeStruct((N_EXPERTS, 1), F32)],
        scratch_shapes=[pltpu.VMEM((N_EXPERTS, 1), F32)],
        compiler_params=_cparams(("arbitrary",)),
    )(oa, ob, ga, gb, x2, g1, sc2, sh2, g2, wpa, wpb, wo, n2, wrt, brt, wsg, wsu, wsd, count0)


def _dest_kernel(eidx_ref, pos_ref, pstart_ref, o_ref, *, tm):
    e_iota = lax.broadcasted_iota(I32, (N_EXPERTS, tm), 0)
    k_iota = lax.broadcasted_iota(I32, (TOP_K, tm), 0)
    eidx = eidx_ref[...]
    start = jnp.zeros((TOP_K, tm), F32)
    for k in range(TOP_K):
        sel = e_iota == eidx[k:k + 1, :]
        sk = jnp.sum(jnp.where(sel, pstart_ref[...], 0.0), axis=0, keepdims=True)
        start = jnp.where(k_iota == k, sk, start)
    o_ref[...] = (start.astype(I32) + pos_ref[...]) * ROW_PARTS


def _dest(eidx, pos, pstart_col):
    t = eidx.shape[1]
    tm = min(t, 1024)
    blk = lambda: pl.BlockSpec((TOP_K, tm), lambda i: (0, i))
    return pl.pallas_call(
        functools.partial(_dest_kernel, tm=tm),
        grid=(t // tm,),
        in_specs=[blk(), blk(), pl.BlockSpec((N_EXPERTS, 1), lambda i: (0, 0))],
        out_specs=blk(),
        out_shape=jax.ShapeDtypeStruct((TOP_K, t), I32),
        compiler_params=_cparams(("parallel",)),
    )(eidx, pos, pstart_col)


def _zero_tail_kernel(last_ref, o_ref):
    del last_ref
    o_ref[...] = jnp.zeros(o_ref.shape, o_ref.dtype)


def _zero_tails(last_blk, nb):
    grid_spec = pltpu.PrefetchScalarGridSpec(
        num_scalar_prefetch=1,
        grid=(N_EXPERTS,),
        in_specs=[],
        out_specs=pl.BlockSpec((ROW_PARTS * ROW_BLOCK, LANES), lambda e, last: (last[e], 0)),
    )
    return pl.pallas_call(
        _zero_tail_kernel,
        grid_spec=grid_spec,
        out_shape=jax.ShapeDtypeStruct((ROW_PARTS * nb * ROW_BLOCK, LANES), U32),
        compiler_params=_cparams(("arbitrary",)),
    )(last_blk)


def _dispatch_kernel(dest_ref, hp_ref, xs_in_ref, xs_ref, sem, *, tm):
    del xs_in_ref

    def row_copy(t, d):
        return pltpu.make_async_copy(
            hp_ref.at[pl.ds(pl.multiple_of(t * ROW_PARTS, ROW_PARTS), ROW_PARTS), :],
            xs_ref.at[pl.ds(pl.multiple_of(d, ROW_PARTS), ROW_PARTS), :], sem)

    def issue(t, c):
        for k in range(TOP_K):
            row_copy(t, dest_ref[k, t]).start(priority=k % 2)
        return c

    def drain(t, c):
        for k in range(TOP_K):
            row_copy(0, 0).wait()
        return c

    lax.fori_loop(0, tm, issue, 0)
    lax.fori_loop(0, tm, drain, 0)


def _dispatch(hp, dest, xs, tm):
    t = dest.shape[1]
    return pl.pallas_call(
        functools.partial(_dispatch_kernel, tm=tm),
        grid=(t // tm,),
        in_specs=[pl.BlockSpec((TOP_K, tm), lambda i: (0, i), memory_space=pltpu.SMEM),
                  pl.BlockSpec((ROW_PARTS * tm, LANES), lambda i: (i, 0)),
                  pl.BlockSpec(memory_space=pl.ANY)],
        out_specs=pl.BlockSpec(memory_space=pl.ANY),
        out_shape=jax.ShapeDtypeStruct(xs.shape, xs.dtype),
        scratch_shapes=[pltpu.SemaphoreType.DMA],
        input_output_aliases={2: 0},
        compiler_params=_cparams(("arbitrary",)),
    )(dest, hp, xs)


def _expert_kernel(be_ref, nu_ref, xs_ref, wg_ref, wu_ref, wd_ref, ys_ref, wg_s, wu_s, wd_s):
    i = pl.program_id(0)
    used = i < nu_ref[0]

    @pl.when(jnp.logical_and(used, jnp.logical_or(i == 0, be_ref[i] != be_ref[jnp.maximum(i - 1, 0)])))
    def _():
        wg_s[...] = wg_ref[0].astype(BF16)
        wu_s[...] = wu_ref[0].astype(BF16)
        wd_s[...] = wd_ref[0].astype(BF16)

    @pl.when(used)
    def _():
        x = _load_packed(xs_ref, ROW_BLOCK).astype(BF16)
        g = jnp.dot(x, wg_s[...], preferred_element_type=F32)
        u = jnp.dot(x, wu_s[...], preferred_element_type=F32)
        hmid = (_silu(g) * u).astype(BF16)
        _store_packed(ys_ref, jnp.dot(hmid, wd_s[...], preferred_element_type=F32))

    @pl.when(jnp.logical_not(used))
    def _():
        ys_ref[...] = jnp.zeros(ys_ref.shape, ys_ref.dtype)


def _experts(xs, blk_e, n_used, w_eg, w_eu, w_ed):
    nb = xs.shape[0] // (ROW_PARTS * ROW_BLOCK)
    rows_spec = lambda: pl.BlockSpec((ROW_PARTS * ROW_BLOCK, LANES), lambda i, be, nu: (i, 0))
    used_rows_spec = pl.BlockSpec((ROW_PARTS * ROW_BLOCK, LANES),
                                  lambda i, be, nu: (jnp.maximum(jnp.minimum(i, nu[0] - 1), 0), 0))
    w_up = lambda: pl.BlockSpec((1, D_MODEL, D_EXPERT), lambda i, be, nu: (be[i], 0, 0))
    grid_spec = pltpu.PrefetchScalarGridSpec(
        num_scalar_prefetch=2,
        grid=(nb,),
        in_specs=[used_rows_spec, w_up(), w_up(),
                  pl.BlockSpec((1, D_EXPERT, D_MODEL), lambda i, be, nu: (be[i], 0, 0))],
        out_specs=rows_spec(),
        scratch_shapes=[pltpu.VMEM((D_MODEL, D_EXPERT), BF16), pltpu.VMEM((D_MODEL, D_EXPERT), BF16),
                        pltpu.VMEM((D_EXPERT, D_MODEL), BF16)],
    )
    return pl.pallas_call(
        _expert_kernel,
        grid_spec=grid_spec,
        out_shape=jax.ShapeDtypeStruct(xs.shape, U32),
        compiler_params=_cparams(("arbitrary",)),
    )(blk_e, n_used, xs, w_eg, w_eu, w_ed)


def _combine_kernel(dest_ref, base_ref, wts_ref, g2_ref, ys_ref, o_ref, buf, sem, *, tm):
    def row_copy(d, k, t):
        return pltpu.make_async_copy(
            ys_ref.at[pl.ds(pl.multiple_of(d, ROW_PARTS), ROW_PARTS), :],
            buf.at[k, pl.ds(pl.multiple_of(t * ROW_PARTS, ROW_PARTS), ROW_PARTS), :], sem)

    def issue(t, c):
        for k in range(TOP_K):
            row_copy(dest_ref[k, t], k, t).start(priority=k % 2)
        return c

    def drain(t, c):
        for k in range(TOP_K):
            row_copy(0, 0, 0).wait()
        return c

    lax.fori_loop(0, tm, issue, 0)
    lax.fori_loop(0, tm, drain, 0)

    w = wts_ref[...]
    y = jnp.zeros((tm, D_MODEL), F32)
    for k in range(TOP_K):
        y = y + w[:, k:k + 1] * _load_packed(buf.at[k], tm)
    o_ref[...] = base_ref[...] + _mod_val(g2_ref) * y


def _combine(base, wts_t, g2, ys, dest, seq, tm):
    t = base.shape[0]
    return pl.pallas_call(
        functools.partial(_combine_kernel, tm=tm),
        grid=(t // tm,),
        in_specs=[pl.BlockSpec((TOP_K, tm), lambda i: (0, i), memory_space=pltpu.SMEM),
                  pl.BlockSpec((tm, D_MODEL), lambda i: (i, 0)),
                  pl.BlockSpec((tm, TOP_K), lambda i: (i, 0)),
                  _mod_spec(g2, seq, tm),
                  pl.BlockSpec(memory_space=pl.ANY)],
        out_specs=pl.BlockSpec((tm, D_MODEL), lambda i: (i, 0)),
        out_shape=jax.ShapeDtypeStruct((t, D_MODEL), F32),
        scratch_shapes=[pltpu.VMEM((TOP_K, ROW_PARTS * tm, LANES), U32), pltpu.SemaphoreType.DMA],
        compiler_params=_cparams(("arbitrary",)),
    )(dest, base, wts_t, g2, ys)


def _mix(x, mods, past_k, past_v, s0, count0, p):
    batch, seq, _ = x.shape
    t = batch * seq
    x2 = x.reshape(t, D_MODEL)
    if seq % 256 == 0:
        tile = lambda cap: min(seq, cap)
    else:
        mods = [jnp.broadcast_to(m, (batch, seq, D_MODEL)).reshape(t, D_MODEL) for m in mods]
        tile = lambda cap: min(t, cap)
    shift1, scale1, gate1, shift2, scale2, gate2 = mods

    h = _norm_mod(x2, p["norm1_w"], scale1, shift1, seq, tile(512))
    qn = jnp.tile(p["q_norm_w"].reshape(1, -1), (1, D_MODEL // HEAD_DIM_A))
    kn = jnp.tile(p["k_norm_w"].reshape(1, -1), (1, D_MODEL // HEAD_DIM_A))
    w_in = p["w_in"]
    (q,) = _proj(h, w_in, 0, [BF16], qn, out_scale=(HEAD_DIM_A ** -0.5 * LOG2E,))
    k, kb = _proj(h, w_in, 1, [F32, BF16], kn, split_first=True, tm=512)
    v, vb = _proj(h, w_in, 2, [F32, BF16])
    (qh,) = _proj(h, w_in, 3, [BF16])
    (fh,) = _proj(h, w_in, 4, [F32])
    (ih,) = _proj(h, w_in, 5, [BF16])
    (gh,) = _proj(h, w_in, 6, [BF16])
    (ga,) = _proj(h, w_in, 7, [BF16])
    (gb,) = _proj(h, w_in, 8, [BF16])

    if past_k is None:
        o_a = _attention_prompt(q, kb, vb, p["lam"], p["subln_w"], batch, seq)
    else:
        past = past_k.shape[1]
        lk = past + seq
        kb = jnp.concatenate([past_k.reshape(batch, past, D_MODEL).astype(BF16),
                              kb.reshape(batch, seq, D_MODEL)], axis=1).reshape(batch * lk, D_MODEL)
        vb = jnp.concatenate([past_v.reshape(batch, past, D_MODEL).astype(BF16),
                              vb.reshape(batch, seq, D_MODEL)], axis=1).reshape(batch * lk, D_MODEL)
        o_a = _attention_cached(q, kb, vb, p["lam"], p["subln_w"], batch, seq, lk)
    o_b, s_new = _hgrn(qh, fh, ih, gh, p["hgrn_lb"], p["hgrn_norm_w"], s0, batch, seq)

    base, hp, eidx, wts, pos, counts = _post(
        o_a, o_b, ga, gb, x2, gate1, scale2, shift2, gate2,
        p["w_proj_a"], p["w_proj_b"], p["w_out"], p["norm2_w"], p["w_router"], p["b_router"],
        p["w_sh_gate"], p["w_sh_up"], p["w_sh_down"], count0, seq, tile(256))
    return dict(base=base, hp=hp, eidx=eidx, wts=wts, pos=pos, counts=counts, k=k, v=v, s=s_new,
                gate2=gate2, seq=seq, tm=tile(512), shape=x.shape)


def kernel(x_prompt, x_sample, cache_attn_k, cache_attn_v, state_hgrn, c_prompt, c_sample, w_ada, b_ada, norm1_w, norm2_w, w_in, q_norm_w, k_norm_w, lambda_q1, lambda_k1, lambda_q2, lambda_k2, subln_w, hgrn_lb, hgrn_norm_w, w_proj_a, w_proj_b, w_out, w_router, b_router, w_exp_gate, w_exp_up, w_exp_down, w_sh_gate, w_sh_up, w_sh_down):
    bp, lp, _ = x_prompt.shape
    bs, ls, _ = x_sample.shape
    l = 0
    lam = (jnp.exp(jnp.sum(lambda_q1[l] * lambda_k1[l])) - jnp.exp(jnp.sum(lambda_q2[l] * lambda_k2[l]))
           + LAMBDA_INIT).astype(F32)
    row_norm = lambda w, s: math.sqrt(HEAD_DIM_A) * jnp.max(jnp.abs(w.astype(F32))) * (s * 1.01)
    knorm = row_norm(k_norm_w[l], 1.0)
    qnorm = row_norm(q_norm_w[l], HEAD_DIM_A ** -0.5 * LOG2E)
    use_bound = (2.0 * qnorm * knorm < ATTN_BOUND_MAX_GAP).astype(F32)
    lam = jnp.stack([lam, knorm, use_bound, jnp.zeros((), F32)]).reshape(1, 4)
    p = dict(
        norm1_w=norm1_w[l], norm2_w=norm2_w[l], w_in=w_in[l].astype(BF16),
        q_norm_w=q_norm_w[l], k_norm_w=k_norm_w[l], lam=lam, subln_w=subln_w[l],
        hgrn_lb=hgrn_lb, hgrn_norm_w=hgrn_norm_w[l],
        w_proj_a=w_proj_a[l].astype(BF16), w_proj_b=w_proj_b[l].astype(BF16), w_out=w_out[l].astype(BF16),
        w_router=w_router[l], b_router=b_router[l],
        w_sh_gate=w_sh_gate[l].astype(BF16), w_sh_up=w_sh_up[l].astype(BF16), w_sh_down=w_sh_down[l].astype(BF16),
    )
    mod = _ada(jnp.concatenate([c_prompt, c_sample], axis=0), w_ada[l], b_ada[l])
    mod = mod.reshape(bp + bs, 6, 1, D_MODEL)
    mods_p = [mod[:bp, j] for j in range(6)]
    mods_s = [mod[bp:, j] for j in range(6)]

    zero_state = jnp.zeros((bp, N_HEADS, HEAD_W, HEAD_W), F32)
    zero_count = jnp.zeros((N_EXPERTS, 1), F32)
    gp = _mix(x_prompt, mods_p, None, None, zero_state, zero_count, p)
    gs = _mix(x_sample, mods_s, cache_attn_k[l], cache_attn_v[l], state_hgrn[l], gp["counts"], p)
    groups = (gp, gs)

    counts = gs["counts"].reshape(N_EXPERTS).astype(I32)
    pcounts = (counts + ROW_BLOCK - 1) // ROW_BLOCK * ROW_BLOCK
    pend = jnp.cumsum(pcounts)
    pstart = pend - pcounts
    n_assign = (bp * lp + bs * ls) * TOP_K
    nb = -(-(n_assign + N_EXPERTS * (ROW_BLOCK - 1)) // ROW_BLOCK)
    blk_e = jnp.minimum(jnp.searchsorted(pend, jnp.arange(nb, dtype=I32) * ROW_BLOCK, side="right"),
                        N_EXPERTS - 1).astype(I32)
    n_used = (pend[-1] // ROW_BLOCK).astype(I32).reshape(1)
    pstart_col = pstart.astype(F32).reshape(N_EXPERTS, 1)

    last_blk = jnp.maximum(pend // ROW_BLOCK - 1, 0).astype(I32)
    xs = _zero_tails(last_blk, nb)
    for g in groups:
        g["dest"] = _dest(g["eidx"], g["pos"], pstart_col)
        xs = _dispatch(g["hp"], g["dest"], xs, g["tm"])
    ys = _experts(xs, blk_e, n_used, w_exp_gate[l], w_exp_up[l], w_exp_down[l])
    yp, ysm = [_combine(g["base"], g["wts"].T, g["gate2"], ys, g["dest"], g["seq"], g["tm"]).reshape(g["shape"])
               for g in groups]

    return (yp, ysm,
            gp["k"].reshape(1, bp, lp, N_HEADS, 2, HEAD_DIM_A), gp["v"].reshape(1, bp, lp, N_HEADS, HEAD_W),
            gp["s"][None],
            gs["k"].reshape(1, bs, ls, N_HEADS, 2, HEAD_DIM_A), gs["v"].reshape(1, bs, ls, N_HEADS, HEAD_W),
            gs["s"][None])
```

```python
import functools
import math

import jax
import jax.numpy as jnp
from jax import lax
from jax.experimental import pallas as pl
from jax.experimental.pallas import tpu as pltpu

F32 = jnp.float32
BF16 = jnp.bfloat16
U32 = jnp.uint32
I32 = jnp.int32

D_MODEL = 1024
N_HEADS = 8
HEAD_W = 128
HEAD_DIM_A = 64
CHUNK = 64
N_EXPERTS = 256
TOP_K = 8
N_GROUPS = 8
GROUP_W = N_EXPERTS // N_GROUPS
TOPK_GROUPS = 4
D_EXPERT = 256
ROUTED_SCALE = 2.5
EPS = 1e-6
LAMBDA_INIT = 0.8 - 0.6 * math.exp(-0.3 * 0)
LANES = 128
NEG = -1e30
LOG2E = math.log2(math.e)

ATTN_BOUND_MAX_GAP = 100.0
GLA_CHUNK = 32
ROW_BLOCK = 512
VMEM_LIMIT = 48 * 1024 * 1024


def _sigmoid(x):
    return 0.5 * jnp.tanh(0.5 * x) + 0.5


def _silu(x):
    return x * _sigmoid(x)


def _cparams(sem):
    return pltpu.CompilerParams(dimension_semantics=sem, vmem_limit_bytes=VMEM_LIMIT)


ROW_PARTS = 4


def _store_packed(ref, x, row0=0):
    m, half = x.shape[0], x.shape[1] // 2
    lo = pltpu.bitcast(x[:, :half].astype(BF16).astype(F32), U32)
    hi = pltpu.bitcast(x[:, half:].astype(BF16).astype(F32), U32)
    w = (hi & jnp.uint32(0xFFFF0000)) | (lo >> 16)
    for j in range(ROW_PARTS):
        ref[pl.ds(ROW_PARTS * row0 + j, m, stride=ROW_PARTS), :] = w[:, j * LANES:(j + 1) * LANES]


def _load_packed(ref, m, row0=0):
    parts = [ref[pl.ds(ROW_PARTS * row0 + j, m, stride=ROW_PARTS), :] for j in range(ROW_PARTS)]
    lo = [pltpu.bitcast(w << 16, F32) for w in parts]
    hi = [pltpu.bitcast(w & jnp.uint32(0xFFFF0000), F32) for w in parts]
    return jnp.concatenate(lo + hi, axis=1)


def _mod_spec(a, seq, tm):
    if a.ndim == 3:
        per_b = seq // tm
        return pl.BlockSpec((1, 1, D_MODEL), lambda i: (i // per_b, 0, 0))
    return pl.BlockSpec((tm, D_MODEL), lambda i: (i, 0))


def _mod_val(ref):
    v = ref[...]
    return v.reshape(v.shape[-2], v.shape[-1])


def _ada_kernel(c_ref, w_ref, b_ref, o_ref):
    c = _silu(c_ref[...])
    o_ref[...] = jnp.dot(c, w_ref[...], precision=lax.Precision.HIGHEST,
                         preferred_element_type=F32) + b_ref[...]


def _ada(c, w_ada, b_ada):
    n = c.shape[0]
    nj = w_ada.shape[1] // D_MODEL
    return pl.pallas_call(
        _ada_kernel,
        grid=(nj,),
        in_specs=[pl.BlockSpec((n, D_MODEL), lambda j: (0, 0)),
                  pl.BlockSpec((D_MODEL, D_MODEL), lambda j: (0, j)),
                  pl.BlockSpec((1, D_MODEL), lambda j: (0, j))],
        out_specs=pl.BlockSpec((n, D_MODEL), lambda j: (0, j)),
        out_shape=jax.ShapeDtypeStruct((n, w_ada.shape[1]), F32),
        compiler_params=_cparams(("arbitrary",)),
    )(c, w_ada, b_ada.reshape(1, -1))


def _norm_mod_kernel(x_ref, w_ref, sc_ref, sh_ref, o_ref):
    x = x_ref[...]
    y = x * lax.rsqrt(jnp.mean(x * x, axis=-1, keepdims=True) + EPS) * w_ref[...]
    o_ref[...] = (y * (1.0 + _mod_val(sc_ref)) + _mod_val(sh_ref)).astype(o_ref.dtype)


def _norm_mod(x2, w, scale, shift, seq, tm):
    t = x2.shape[0]
    return pl.pallas_call(
        _norm_mod_kernel,
        grid=(t // tm,),
        in_specs=[pl.BlockSpec((tm, D_MODEL), lambda i: (i, 0)),
                  pl.BlockSpec((1, D_MODEL), lambda i: (0, 0)),
                  _mod_spec(scale, seq, tm), _mod_spec(shift, seq, tm)],
        out_specs=pl.BlockSpec((tm, D_MODEL), lambda i: (i, 0)),
        out_shape=jax.ShapeDtypeStruct((t, D_MODEL), BF16),
        compiler_params=_cparams(("parallel",)),
    )(x2, w.reshape(1, -1), scale, shift)


def _head_norm(acc, w):
    r = lax.broadcasted_iota(I32, (HEAD_W, HEAD_W), 0) // HEAD_DIM_A
    c = lax.broadcasted_iota(I32, (HEAD_W, HEAD_W), 1) // HEAD_DIM_A
    ones = (r == c).astype(BF16)
    outs = []
    for h in range(N_HEADS):
        a = acc[:, h * HEAD_W:(h + 1) * HEAD_W]
        sq = a * a
        hi = sq.astype(BF16)
        lo = (sq - hi.astype(F32)).astype(BF16)
        ss = (jnp.dot(hi, ones, preferred_element_type=F32)
              + jnp.dot(lo, ones, preferred_element_type=F32))
        outs.append(a * lax.rsqrt(ss * (1.0 / HEAD_DIM_A) + EPS) * w[:, h * HEAD_W:(h + 1) * HEAD_W])
    return jnp.concatenate(outs, axis=1)


N_MAPS = D_MODEL // HEAD_DIM_A


def _proj_kernel(h_ref, w_ref, nw_ref, *o_refs, head_norm, out_scale, split_first):
    acc = jnp.dot(h_ref[...], w_ref[...], preferred_element_type=F32)
    if head_norm:
        acc = _head_norm(acc, nw_ref[...])
    for n, (o_ref, s) in enumerate(zip(o_refs, out_scale)):
        val = (acc if s == 1.0 else acc * s).astype(o_ref.dtype)
        if split_first and n == 0:
            for g in range(N_MAPS):
                o_ref[pl.ds(g, val.shape[0], stride=N_MAPS), :] = val[:, g * HEAD_DIM_A:(g + 1) * HEAD_DIM_A]
        else:
            o_ref[...] = val


def _proj(h, w_in_bf, group, out_dtypes, norm_w=None, out_scale=None, split_first=False, tm=1024):
    t = h.shape[0]
    tm = min(tm, t)
    nw = jnp.ones((1, D_MODEL), F32) if norm_w is None else norm_w
    out_scale = tuple(out_scale or (1.0,) * len(out_dtypes))
    specs = [pl.BlockSpec((tm, D_MODEL), lambda i: (i, 0)) for _ in out_dtypes]
    shapes = [jax.ShapeDtypeStruct((t, D_MODEL), dt) for dt in out_dtypes]
    if split_first:
        specs[0] = pl.BlockSpec((N_MAPS * tm, HEAD_DIM_A), lambda i: (i, 0))
        shapes[0] = jax.ShapeDtypeStruct((N_MAPS * t, HEAD_DIM_A), out_dtypes[0])
    outs = pl.pallas_call(
        functools.partial(_proj_kernel, head_norm=norm_w is not None, out_scale=out_scale,
                          split_first=split_first),
        grid=(t // tm,),
        in_specs=[pl.BlockSpec((tm, D_MODEL), lambda i: (i, 0)),
                  pl.BlockSpec((D_MODEL, D_MODEL), lambda i: (0, group)),
                  pl.BlockSpec((1, D_MODEL), lambda i: (0, 0))],
        out_specs=specs,
        out_shape=shapes,
        compiler_params=_cparams(("parallel",)),
    )(h, w_in_bf, nw)
    return outs


def _split_maps(q):
    lane = lax.broadcasted_iota(I32, q.shape, 1)
    zero = jnp.zeros_like(q)
    return jnp.where(lane < HEAD_DIM_A, q, zero), jnp.where(lane >= HEAD_DIM_A, q, zero)


def _finish_head(acc1, l1, acc2, l2, lam, sw):
    o = acc1 / l1 - lam * (acc2 / l2)
    o = o * lax.rsqrt(jnp.mean(o * o, axis=-1, keepdims=True) + EPS) * sw
    return o * (1.0 - LAMBDA_INIT)


def _attn_kernel(q_ref, k_ref, v_ref, lam_ref, sw_ref, o_ref, mx_s, mrep_s, ls_s, acc_s,
                 *, tq, tk, hp):
    qi = pl.program_id(2)
    nf = tk // LANES
    qs = []
    for hh in range(hp):
        qs.extend(_split_maps(q_ref[:, hh * HEAD_W:(hh + 1) * HEAD_W]))
    ns = 2 * hp

    r_chunk = lax.broadcasted_iota(I32, (tq, tk), 0) // CHUNK
    c_chunk = lax.broadcasted_iota(I32, (tq, tk), 1) // CHUNK
    diag_ok = c_chunk <= r_chunk

    def scores(kt, s, masked):
        k = k_ref[pl.ds(pl.multiple_of(kt * tk, tk), tk), (s // 2) * HEAD_W:(s // 2 + 1) * HEAD_W]
        sc = lax.dot_general(qs[s], k, (((1,), (1,)), ((), ())), preferred_element_type=F32)
        return jnp.where(diag_ok, sc, NEG) if masked else sc

    def fold(x, op):
        r = x[:, :LANES]
        for j in range(1, nf):
            r = op(r, x[:, j * LANES:(j + 1) * LANES])
        return r

    def pass1(kt, first):
        for s in range(ns):
            m = fold(scores(kt, s, first), jnp.maximum)
            mx_s[s] = m if first else jnp.maximum(mx_s[s], m)

    def pass2(kt, first):
        for s in range(ns):
            m = mrep_s[s]
            p = jnp.exp2(scores(kt, s, first) - jnp.concatenate([m] * nf, axis=1))
            v = v_ref[pl.ds(pl.multiple_of(kt * tk, tk), tk), (s // 2) * HEAD_W:(s // 2 + 1) * HEAD_W]
            pv = jnp.dot(p.astype(BF16), v, preferred_element_type=F32)
            ls_s[s] = fold(p, jnp.add) if first else ls_s[s] + fold(p, jnp.add)
            acc_s[s] = pv if first else acc_s[s] + pv

    def loop(fn):
        def body(kt, c):
            fn(kt, False)
            return c

        fn(qi, True)
        lax.fori_loop(0, qi, body, 0)

    use_bound = lam_ref[0, 2] > 0.5

    @pl.when(use_bound)
    def _():
        mrep_s[...] = jnp.full(mrep_s.shape, lam_ref[0, 1], F32)

    @pl.when(jnp.logical_not(use_bound))
    def _():
        loop(pass1)
        for s in range(ns):
            mrep_s[s] = jnp.broadcast_to(jnp.max(mx_s[s], axis=-1, keepdims=True), (tq, LANES))

    loop(pass2)

    lam = lam_ref[0, 0]
    for hh in range(hp):
        l1 = jnp.sum(ls_s[2 * hh], axis=-1, keepdims=True)
        l2 = jnp.sum(ls_s[2 * hh + 1], axis=-1, keepdims=True)
        o = _finish_head(acc_s[2 * hh], l1, acc_s[2 * hh + 1], l2, lam, sw_ref[...])
        o_ref[:, hh * HEAD_W:(hh + 1) * HEAD_W] = o.astype(o_ref.dtype)


def _attention_prompt(q, kb, vb, lam, subln_w, batch, seq):
    tq = tk = min(seq, 256)
    hp = 8
    nq = seq // tq
    kern = functools.partial(_attn_kernel, tq=tq, tk=tk, hp=hp)
    w = hp * HEAD_W
    return pl.pallas_call(
        kern,
        grid=(batch, N_HEADS // hp, nq),
        in_specs=[pl.BlockSpec((tq, w), lambda b, h, i: (b * nq + i, h)),
                  pl.BlockSpec((seq, w), lambda b, h, i: (b, h)),
                  pl.BlockSpec((seq, w), lambda b, h, i: (b, h)),
                  pl.BlockSpec((1, 4), lambda b, h, i: (0, 0), memory_space=pltpu.SMEM),
                  pl.BlockSpec((1, HEAD_W), lambda b, h, i: (0, 0))],
        out_specs=pl.BlockSpec((tq, w), lambda b, h, i: (b * nq + i, h)),
        out_shape=jax.ShapeDtypeStruct((batch * seq, D_MODEL), BF16),
        scratch_shapes=[pltpu.VMEM((2 * hp, tq, LANES), F32) for _ in range(4)],
        compiler_params=_cparams(("parallel", "parallel", "arbitrary")),
    )(q, kb, vb, lam, subln_w.reshape(1, -1))


def _attn_cached_kernel(q_ref, k_ref, v_ref, lam_ref, sw_ref, o_ref, *, lq, lk):
    qpos = (lk - lq) + lax.broadcasted_iota(I32, (lq, lk), 0)
    kpos = lax.broadcasted_iota(I32, (lq, lk), 1)
    allowed = (kpos // CHUNK) <= (qpos // CHUNK)
    lam = lam_ref[0, 0]
    for h in range(N_HEADS):
        sl = slice(h * HEAD_W, (h + 1) * HEAD_W)
        k = k_ref[:, sl]
        v = v_ref[:, sl]
        accs, ls = [], []
        for qm in _split_maps(q_ref[:, sl]):
            sc = lax.dot_general(qm, k, (((1,), (1,)), ((), ())), preferred_element_type=F32)
            sc = jnp.where(allowed, sc, NEG)
            p = jnp.exp2(sc - jnp.max(sc, axis=-1, keepdims=True))
            ls.append(jnp.sum(p, axis=-1, keepdims=True))
            accs.append(jnp.dot(p.astype(BF16), v, preferred_element_type=F32))
        o = _finish_head(accs[0], ls[0], accs[1], ls[1], lam, sw_ref[...])
        o_ref[:, sl] = o.astype(o_ref.dtype)


def _attention_cached(q, kb, vb, lam, subln_w, batch, lq, lk):
    return pl.pallas_call(
        functools.partial(_attn_cached_kernel, lq=lq, lk=lk),
        grid=(batch,),
        in_specs=[pl.BlockSpec((lq, D_MODEL), lambda b: (b, 0)),
                  pl.BlockSpec((lk, D_MODEL), lambda b: (b, 0)),
                  pl.BlockSpec((lk, D_MODEL), lambda b: (b, 0)),
                  pl.BlockSpec((1, 4), lambda b: (0, 0), memory_space=pltpu.SMEM),
                  pl.BlockSpec((1, HEAD_W), lambda b: (0, 0))],
        out_specs=pl.BlockSpec((lq, D_MODEL), lambda b: (b, 0)),
        out_shape=jax.ShapeDtypeStruct((batch * lq, D_MODEL), BF16),
        compiler_params=_cparams(("parallel",)),
    )(q, kb, vb, lam, subln_w.reshape(1, -1))


def _hgrn_kernel(qh_ref, fh_ref, ih_ref, gh_ref, lb_ref, nw_ref, s0_ref, o_ref, s_out_ref,
                 st_s, qin_s, qmid_s, kmid_s, kend_s, dec_s, *, ct, c):
    t = pl.program_id(1)
    nt = pl.num_programs(1)
    nc = ct // c

    @pl.when(t == 0)
    def _():
        for h in range(N_HEADS):
            st_s[h] = s0_ref[0, h].T

    a = lb_ref[...]
    amax = jnp.max(a, axis=0, keepdims=True)
    e = jnp.exp(a - amax)
    lb = e[0:1] / jnp.sum(e, axis=0, keepdims=True)

    row = lax.broadcasted_iota(I32, (c, c), 0)
    col = lax.broadcasted_iota(I32, (c, c), 1)
    causal = col <= row
    t_idx = lax.broadcasted_iota(I32, (c, D_MODEL), 0)

    def cumsum_rows(g):
        sh = 1
        while sh < c:
            g = g + jnp.where(t_idx >= sh, pltpu.roll(g, sh, axis=0), 0.0)
            sh *= 2
        return g

    for ci in range(nc):
        rows = slice(ci * c, (ci + 1) * c)
        f = lb + (1.0 - lb) * _sigmoid(fh_ref[rows, :])
        b = cumsum_rows(jnp.log(f))
        b_last = b[c - 1:c, :]
        b_mid = b[c // 2 - 1:c // 2, :]
        q = _silu(qh_ref[rows, :].astype(F32)) * (HEAD_W ** -0.5)
        kk = 1.0 - f
        e_dn = jnp.exp(b - b_mid)
        e_up = jnp.exp(b_mid - b)
        qm = q * e_dn
        km = kk * e_up
        qmid_s[rows, :] = qm.astype(BF16)
        kmid_s[rows, :] = km.astype(BF16)
        qin_s[rows, :] = (qm * jnp.exp(b_mid)).astype(BF16)
        kend_s[rows, :] = (km * jnp.exp(b_last - b_mid)).astype(BF16)
        dec_s[ci:ci + 1, :] = jnp.exp(b_last)

    nw = nw_ref[...]
    st = [st_s[h] for h in range(N_HEADS)]
    for ci in range(nc):
        rows = slice(ci * c, (ci + 1) * c)
        gate = _silu(gh_ref[rows, :].astype(F32))
        decay = dec_s[ci:ci + 1, :]
        for h in range(N_HEADS):
            sl = slice(h * HEAD_W, (h + 1) * HEAD_W)
            v = ih_ref[rows, sl]
            inter = lax.dot_general(qin_s[rows, sl], st[h].astype(BF16), (((1,), (1,)), ((), ())),
                                    preferred_element_type=F32)
            att = lax.dot_general(qmid_s[rows, sl], kmid_s[rows, sl], (((1,), (1,)), ((), ())),
                                  preferred_element_type=F32)
            att = jnp.where(causal, att, 0.0)
            o = inter + jnp.dot(att.astype(BF16), v, preferred_element_type=F32)
            upd = lax.dot_general(v, kend_s[rows, sl], (((0,), (0,)), ((), ())),
                                  preferred_element_type=F32)
            st[h] = decay[:, sl] * st[h] + upd
            o = o * lax.rsqrt(jnp.mean(o * o, axis=-1, keepdims=True) + EPS) * nw[:, sl]
            o_ref[rows, sl] = (o * gate[:, sl]).astype(o_ref.dtype)
    for h in range(N_HEADS):
        st_s[h] = st[h]

    @pl.when(t == nt - 1)
    def _():
        for h in range(N_HEADS):
            s_out_ref[0, h] = st_s[h].T


def _hgrn(qh, fh, ih, gh, hgrn_lb, norm_w, s0, batch, seq):
    ct = min(seq, 512)
    c = min(GLA_CHUNK, ct)
    nt = seq // ct
    kern = functools.partial(_hgrn_kernel, ct=ct, c=c)
    tok = pl.BlockSpec((ct, D_MODEL), lambda b, t: (b * nt + t, 0))
    st_spec = pl.BlockSpec((1, N_HEADS, HEAD_W, HEAD_W), lambda b, t: (b, 0, 0, 0))
    return pl.pallas_call(
        kern,
        grid=(batch, nt),
        in_specs=[tok, tok, tok, tok,
                  pl.BlockSpec((2, D_MODEL), lambda b, t: (0, 0)),
                  pl.BlockSpec((1, D_MODEL), lambda b, t: (0, 0)),
                  st_spec],
        out_specs=[tok, st_spec],
        out_shape=[jax.ShapeDtypeStruct((batch * seq, D_MODEL), BF16),
                   jax.ShapeDtypeStruct((batch, N_HEADS, HEAD_W, HEAD_W), F32)],
        scratch_shapes=[pltpu.VMEM((N_HEADS, HEAD_W, HEAD_W), F32)]
        + [pltpu.VMEM((ct, D_MODEL), BF16) for _ in range(4)]
        + [pltpu.VMEM((max(ct // c, 8), D_MODEL), F32)],
        compiler_params=_cparams(("parallel", "arbitrary")),
    )(qh, fh, ih, gh, hgrn_lb, jnp.tile(norm_w.reshape(1, -1), (1, N_HEADS)), s0)


def _route(h2, wrt_ref, brt_ref, carry_s, tm):
    nt_dims = (((1,), (1,)), ((), ()))
    w_hi = wrt_ref[0]
    w_lo = wrt_ref[1]
    h_hi = h2.astype(BF16)
    h_lo = (h2 - h_hi.astype(F32)).astype(BF16)
    logits = (lax.dot_general(w_hi, h_hi, nt_dims, preferred_element_type=F32)
              + lax.dot_general(w_hi, h_lo, nt_dims, preferred_element_type=F32)
              + lax.dot_general(w_lo, h_hi, nt_dims, preferred_element_type=F32))
    scores = _sigmoid(logits)
    biased = scores + brt_ref[...]
    big = float(2 * N_EXPERTS)

    x3 = biased.reshape(N_GROUPS, GROUP_W, tm)
    i3 = lax.broadcasted_iota(I32, x3.shape, 1).astype(F32)
    m1 = jnp.max(x3, axis=1, keepdims=True)
    i1 = jnp.min(jnp.where(x3 == m1, i3, big), axis=1, keepdims=True)
    m2 = jnp.max(jnp.where(i3 == i1, NEG, x3), axis=1, keepdims=True)
    gs = (m1 + m2).reshape(N_GROUPS, tm)

    g_iota = lax.broadcasted_iota(I32, (N_GROUPS, tm), 0).astype(F32)
    rem = gs
    gsel = jnp.zeros((N_GROUPS, tm), F32)
    for _ in range(TOPK_GROUPS):
        gm = jnp.max(rem, axis=0, keepdims=True)
        first = jnp.min(jnp.where(rem == gm, g_iota, big), axis=0, keepdims=True)
        sel = g_iota == first
        gsel = jnp.where(sel, 1.0, gsel)
        rem = jnp.where(sel, NEG, rem)
    keep = jnp.broadcast_to(gsel.reshape(N_GROUPS, 1, tm), x3.shape) > 0.5
    masked = jnp.where(keep, x3, NEG).reshape(N_EXPERTS, tm)

    e_iota = lax.broadcasted_iota(I32, (N_EXPERTS, tm), 0).astype(F32)
    onehot = jnp.zeros((N_EXPERTS, tm), F32)
    idxs, ws = [], []
    for _ in range(TOP_K):
        mk = jnp.max(masked, axis=0, keepdims=True)
        ik = jnp.min(jnp.where(masked == mk, e_iota, big), axis=0, keepdims=True)
        selk = e_iota == ik
        ws.append(jnp.sum(jnp.where(selk, scores, 0.0), axis=0, keepdims=True))
        idxs.append(ik)
        masked = jnp.where(selk, NEG, masked)
        onehot = jnp.where(selk, 1.0, onehot)
    wsum = ws[0]
    for k in range(1, TOP_K):
        wsum = wsum + ws[k]

    r = lax.broadcasted_iota(I32, (tm, tm), 0)
    cidx = lax.broadcasted_iota(I32, (tm, tm), 1)
    before = (r < cidx).astype(BF16)
    cum = jnp.dot(onehot.astype(BF16), before, preferred_element_type=F32) + carry_s[...]
    carry_s[...] = carry_s[...] + jnp.sum(onehot, axis=1, keepdims=True)

    k_iota = lax.broadcasted_iota(I32, (TOP_K, tm), 0)
    e_out = jnp.zeros((TOP_K, tm), F32)
    w_out = jnp.zeros((TOP_K, tm), F32)
    p_out = jnp.zeros((TOP_K, tm), F32)
    for k in range(TOP_K):
        pk = jnp.sum(jnp.where(e_iota == idxs[k], cum, 0.0), axis=0, keepdims=True)
        e_out = jnp.where(k_iota == k, idxs[k], e_out)
        w_out = jnp.where(k_iota == k, ws[k] / wsum * ROUTED_SCALE, w_out)
        p_out = jnp.where(k_iota == k, pk, p_out)
    return e_out, w_out, p_out


def _post_kernel(oa_ref, ob_ref, ga_ref, gb_ref, x_ref, g1_ref, sc2_ref, sh2_ref, g2_ref,
                 wpa_ref, wpb_ref, wo_ref, n2_ref, wrt_ref, brt_ref, wsg_ref, wsu_ref, wsd_ref, c0_ref,
                 base_ref, hp_ref, eidx_ref, wts_ref, pos_ref, cnt_ref, carry_s, *, tm):
    i = pl.program_id(0)

    @pl.when(i == 0)
    def _():
        carry_s[...] = c0_ref[...]

    pa = jnp.dot(oa_ref[...], wpa_ref[...], preferred_element_type=F32)
    pb = jnp.dot(ob_ref[...], wpb_ref[...], preferred_element_type=F32)
    u = _sigmoid(ga_ref[...].astype(F32)) * pa + _sigmoid(gb_ref[...].astype(F32)) * pb
    x1 = x_ref[...] + _mod_val(g1_ref) * jnp.dot(u.astype(BF16), wo_ref[...], preferred_element_type=F32)
    y = x1 * lax.rsqrt(jnp.mean(x1 * x1, axis=-1, keepdims=True) + EPS) * n2_ref[...]
    h2 = y * (1.0 + _mod_val(sc2_ref)) + _mod_val(sh2_ref)
    _store_packed(hp_ref, h2)

    hb = h2.astype(BF16)
    sg = jnp.dot(hb, wsg_ref[...], preferred_element_type=F32)
    su = jnp.dot(hb, wsu_ref[...], preferred_element_type=F32)
    shared = jnp.dot((_silu(sg) * su).astype(BF16), wsd_ref[...], preferred_element_type=F32)
    base_ref[...] = x1 + _mod_val(g2_ref) * shared

    e_out, w_out, p_out = _route(h2, wrt_ref, brt_ref, carry_s, tm)
    eidx_ref[...] = e_out.astype(I32)
    wts_ref[...] = w_out
    pos_ref[...] = p_out.astype(I32)
    cnt_ref[...] = carry_s[...]


def _post(oa, ob, ga, gb, x2, g1, sc2, sh2, g2, wpa, wpb, wo, n2, wr, br, wsg, wsu, wsd, count0, seq, tm):
    t = x2.shape[0]
    tok = lambda: pl.BlockSpec((tm, D_MODEL), lambda i: (i, 0))
    mod = lambda: _mod_spec(g1, seq, tm)
    full = lambda a: pl.BlockSpec(a.shape, lambda i: (0,) * a.ndim)
    n2 = n2.reshape(1, -1)
    wrt_hi = wr.T.astype(BF16)
    wrt = jnp.stack([wrt_hi, (wr.T - wrt_hi.astype(F32)).astype(BF16)])
    brt = br.reshape(-1, 1)
    k_out = lambda: pl.BlockSpec((TOP_K, tm), lambda i: (0, i))
    return pl.pallas_call(
        functools.partial(_post_kernel, tm=tm),
        grid=(t // tm,),
        in_specs=[tok(), tok(), tok(), tok(), tok(), mod(), mod(), mod(), mod(),
                  full(wpa), full(wpb), full(wo), full(n2), full(wrt), full(brt),
                  full(wsg), full(wsu), full(wsd), full(count0)],
        out_specs=[tok(), pl.BlockSpec((ROW_PARTS * tm, LANES), lambda i: (i, 0)),
                   k_out(), k_out(), k_out(),
                   pl.BlockSpec((N_EXPERTS, 1), lambda i: (0, 0))],
        out_shape=[jax.ShapeDtypeStruct((t, D_MODEL), F32),
                   jax.ShapeDtypeStruct((ROW_PARTS * t, LANES), U32),
                   jax.ShapeDtypeStruct((TOP_K, t), I32),
                   jax.ShapeDtypeStruct((TOP_K, t), F32),
                   jax.ShapeDtypeStruct((TOP_K, t), I32),
                   jax.ShapeDtypeStruct((N_EXPERTS, 1), F32)],
        scratch_shapes=[pltpu.VMEM((N_EXPERTS, 1), F32)],
        compiler_params=_cparams(("arbitrary",)),
    )(oa, ob, ga, gb, x2, g1, sc2, sh2, g2, wpa, wpb, wo, n2, wrt, brt, wsg, wsu, wsd, count0)


def _dest_kernel(eidx_ref, pos_ref, pstart_ref, o_ref, *, tm):
    e_iota = lax.broadcasted_iota(I32, (N_EXPERTS, tm), 0)
    k_iota = lax.broadcasted_iota(I32, (TOP_K, tm), 0)
    eidx = eidx_ref[...]
    start = jnp.zeros((TOP_K, tm), F32)
    for k in range(TOP_K):
        sel = e_iota == eidx[k:k + 1, :]
        sk = jnp.sum(jnp.where(sel, pstart_ref[...], 0.0), axis=0, keepdims=True)
        start = jnp.where(k_iota == k, sk, start)
    o_ref[...] = (start.astype(I32) + pos_ref[...]) * ROW_PARTS


def _dest(eidx, pos, pstart_col):
    t = eidx.shape[1]
    tm = min(t, 1024)
    blk = lambda: pl.BlockSpec((TOP_K, tm), lambda i: (0, i))
    return pl.pallas_call(
        functools.partial(_dest_kernel, tm=tm),
        grid=(t // tm,),
        in_specs=[blk(), blk(), pl.BlockSpec((N_EXPERTS, 1), lambda i: (0, 0))],
        out_specs=blk(),
        out_shape=jax.ShapeDtypeStruct((TOP_K, t), I32),
        compiler_params=_cparams(("parallel",)),
    )(eidx, pos, pstart_col)


def _zero_tail_kernel(last_ref, o_ref):
    del last_ref
    o_ref[...] = jnp.zeros(o_ref.shape, o_ref.dtype)


def _zero_tails(last_blk, nb):
    grid_spec = pltpu.PrefetchScalarGridSpec(
        num_scalar_prefetch=1,
        grid=(N_EXPERTS,),
        in_specs=[],
        out_specs=pl.BlockSpec((ROW_PARTS * ROW_BLOCK, LANES), lambda e, last: (last[e], 0)),
    )
    return pl.pallas_call(
        _zero_tail_kernel,
        grid_spec=grid_spec,
        out_shape=jax.ShapeDtypeStruct((ROW_PARTS * nb * ROW_BLOCK, LANES), U32),
        compiler_params=_cparams(("arbitrary",)),
    )(last_blk)


def _dispatch_kernel(dest_ref, hp_ref, xs_in_ref, xs_ref, sem, *, tm):
    del xs_in_ref

    def row_copy(t, d):
        return pltpu.make_async_copy(
            hp_ref.at[pl.ds(pl.multiple_of(t * ROW_PARTS, ROW_PARTS), ROW_PARTS), :],
            xs_ref.at[pl.ds(pl.multiple_of(d, ROW_PARTS), ROW_PARTS), :], sem)

    def issue(t, c):
        for k in range(TOP_K):
            row_copy(t, dest_ref[k, t]).start(priority=k % 2)
        return c

    def drain(t, c):
        for k in range(TOP_K):
            row_copy(0, 0).wait()
        return c

    lax.fori_loop(0, tm, issue, 0)
    lax.fori_loop(0, tm, drain, 0)


def _dispatch(hp, dest, xs, tm):
    t = dest.shape[1]
    return pl.pallas_call(
        functools.partial(_dispatch_kernel, tm=tm),
        grid=(t // tm,),
        in_specs=[pl.BlockSpec((TOP_K, tm), lambda i: (0, i), memory_space=pltpu.SMEM),
                  pl.BlockSpec((ROW_PARTS * tm, LANES), lambda i: (i, 0)),
                  pl.BlockSpec(memory_space=pl.ANY)],
        out_specs=pl.BlockSpec(memory_space=pl.ANY),
        out_shape=jax.ShapeDtypeStruct(xs.shape, xs.dtype),
        scratch_shapes=[pltpu.SemaphoreType.DMA],
        input_output_aliases={2: 0},
        compiler_params=_cparams(("arbitrary",)),
    )(dest, hp, xs)


def _expert_kernel(be_ref, nu_ref, xs_ref, wg_ref, wu_ref, wd_ref, ys_ref, wg_s, wu_s, wd_s):
    i = pl.program_id(0)
    used = i < nu_ref[0]

    @pl.when(jnp.logical_and(used, jnp.logical_or(i == 0, be_ref[i] != be_ref[jnp.maximum(i - 1, 0)])))
    def _():
        wg_s[...] = wg_ref[0].astype(BF16)
        wu_s[...] = wu_ref[0].astype(BF16)
        wd_s[...] = wd_ref[0].astype(BF16)

    @pl.when(used)
    def _():
        x = _load_packed(xs_ref, ROW_BLOCK).astype(BF16)
        g = jnp.dot(x, wg_s[...], preferred_element_type=F32)
        u = jnp.dot(x, wu_s[...], preferred_element_type=F32)
        hmid = (_silu(g) * u).astype(BF16)
        _store_packed(ys_ref, jnp.dot(hmid, wd_s[...], preferred_element_type=F32))

    @pl.when(jnp.logical_not(used))
    def _():
        ys_ref[...] = jnp.zeros(ys_ref.shape, ys_ref.dtype)


def _experts(xs, blk_e, n_used, w_eg, w_eu, w_ed):
    nb = xs.shape[0] // (ROW_PARTS * ROW_BLOCK)
    rows_spec = lambda: pl.BlockSpec((ROW_PARTS * ROW_BLOCK, LANES), lambda i, be, nu: (i, 0))
    used_rows_spec = pl.BlockSpec((ROW_PARTS * ROW_BLOCK, LANES),
                                  lambda i, be, nu: (jnp.maximum(jnp.minimum(i, nu[0] - 1), 0), 0))
    w_up = lambda: pl.BlockSpec((1, D_MODEL, D_EXPERT), lambda i, be, nu: (be[i], 0, 0))
    grid_spec = pltpu.PrefetchScalarGridSpec(
        num_scalar_prefetch=2,
        grid=(nb,),
        in_specs=[used_rows_spec, w_up(), w_up(),
                  pl.BlockSpec((1, D_EXPERT, D_MODEL), lambda i, be, nu: (be[i], 0, 0))],
        out_specs=rows_spec(),
        scratch_shapes=[pltpu.VMEM((D_MODEL, D_EXPERT), BF16), pltpu.VMEM((D_MODEL, D_EXPERT), BF16),
                        pltpu.VMEM((D_EXPERT, D_MODEL), BF16)],
    )
    return pl.pallas_call(
        _expert_kernel,
        grid_spec=grid_spec,
        out_shape=jax.ShapeDtypeStruct(xs.shape, U32),
        compiler_params=_cparams(("arbitrary",)),
    )(blk_e, n_used, xs, w_eg, w_eu, w_ed)


def _combine_kernel(dest_ref, base_ref, wts_ref, g2_ref, ys_ref, o_ref, buf, sem, *, tm):
    def row_copy(d, k, t):
        return pltpu.make_async_copy(
            ys_ref.at[pl.ds(pl.multiple_of(d, ROW_PARTS), ROW_PARTS), :],
            buf.at[k, pl.ds(pl.multiple_of(t * ROW_PARTS, ROW_PARTS), ROW_PARTS), :], sem)

    def issue(t, c):
        for k in range(TOP_K):
            row_copy(dest_ref[k, t], k, t).start(priority=k % 2)
        return c

    def drain(t, c):
        for k in range(TOP_K):
            row_copy(0, 0, 0).wait()
        return c

    lax.fori_loop(0, tm, issue, 0)
    lax.fori_loop(0, tm, drain, 0)

    w = wts_ref[...]
    y = jnp.zeros((tm, D_MODEL), F32)
    for k in range(TOP_K):
        y = y + w[:, k:k + 1] * _load_packed(buf.at[k], tm)
    o_ref[...] = base_ref[...] + _mod_val(g2_ref) * y


def _combine(base, wts_t, g2, ys, dest, seq, tm):
    t = base.shape[0]
    return pl.pallas_call(
        functools.partial(_combine_kernel, tm=tm),
        grid=(t // tm,),
        in_specs=[pl.BlockSpec((TOP_K, tm), lambda i: (0, i), memory_space=pltpu.SMEM),
                  pl.BlockSpec((tm, D_MODEL), lambda i: (i, 0)),
                  pl.BlockSpec((tm, TOP_K), lambda i: (i, 0)),
                  _mod_spec(g2, seq, tm),
                  pl.BlockSpec(memory_space=pl.ANY)],
        out_specs=pl.BlockSpec((tm, D_MODEL), lambda i: (i, 0)),
        out_shape=jax.ShapeDtypeStruct((t, D_MODEL), F32),
        scratch_shapes=[pltpu.VMEM((TOP_K, ROW_PARTS * tm, LANES), U32), pltpu.SemaphoreType.DMA],
        compiler_params=_cparams(("arbitrary",)),
    )(dest, base, wts_t, g2, ys)


def _mix(x, mods, past_k, past_v, s0, count0, p):
    batch, seq, _ = x.shape
    t = batch * seq
    x2 = x.reshape(t, D_MODEL)
    if seq % 256 == 0:
        tile = lambda cap: min(seq, cap)
    else:
        mods = [jnp.broadcast_to(m, (batch, seq, D_MODEL)).reshape(t, D_MODEL) for m in mods]
        tile = lambda cap: min(t, cap)
    shift1, scale1, gate1, shift2, scale2, gate2 = mods

    h = _norm_mod(x2, p["norm1_w"], scale1, shift1, seq, tile(512))
    qn = jnp.tile(p["q_norm_w"].reshape(1, -1), (1, D_MODEL // HEAD_DIM_A))
    kn = jnp.tile(p["k_norm_w"].reshape(1, -1), (1, D_MODEL // HEAD_DIM_A))
    w_in = p["w_in"]
    (q,) = _proj(h, w_in, 0, [BF16], qn, out_scale=(HEAD_DIM_A ** -0.5 * LOG2E,))
    k, kb = _proj(h, w_in, 1, [F32, BF16], kn, split_first=True, tm=512)
    v, vb = _proj(h, w_in, 2, [F32, BF16])
    (qh,) = _proj(h, w_in, 3, [BF16])
    (fh,) = _proj(h, w_in, 4, [F32])
    (ih,) = _proj(h, w_in, 5, [BF16])
    (gh,) = _proj(h, w_in, 6, [BF16])
    (ga,) = _proj(h, w_in, 7, [BF16])
    (gb,) = _proj(h, w_in, 8, [BF16])

    if past_k is None:
        o_a = _attention_prompt(q, kb, vb, p["lam"], p["subln_w"], batch, seq)
    else:
        past = past_k.shape[1]
        lk = past + seq
        kb = jnp.concatenate([past_k.reshape(batch, past, D_MODEL).astype(BF16),
                              kb.reshape(batch, seq, D_MODEL)], axis=1).reshape(batch * lk, D_MODEL)
        vb = jnp.concatenate([past_v.reshape(batch, past, D_MODEL).astype(BF16),
                              vb.reshape(batch, seq, D_MODEL)], axis=1).reshape(batch * lk, D_MODEL)
        o_a = _attention_cached(q, kb, vb, p["lam"], p["subln_w"], batch, seq, lk)
    o_b, s_new = _hgrn(qh, fh, ih, gh, p["hgrn_lb"], p["hgrn_norm_w"], s0, batch, seq)

    base, hp, eidx, wts, pos, counts = _post(
        o_a, o_b, ga, gb, x2, gate1, scale2, shift2, gate2,
        p["w_proj_a"], p["w_proj_b"], p["w_out"], p["norm2_w"], p["w_router"], p["b_router"],
        p["w_sh_gate"], p["w_sh_up"], p["w_sh_down"], count0, seq, tile(512))
    return dict(base=base, hp=hp, eidx=eidx, wts=wts, pos=pos, counts=counts, k=k, v=v, s=s_new,
                gate2=gate2, seq=seq, tm=tile(512), shape=x.shape)


def kernel(x_prompt, x_sample, cache_attn_k, cache_attn_v, state_hgrn, c_prompt, c_sample, w_ada, b_ada, norm1_w, norm2_w, w_in, q_norm_w, k_norm_w, lambda_q1, lambda_k1, lambda_q2, lambda_k2, subln_w, hgrn_lb, hgrn_norm_w, w_proj_a, w_proj_b, w_out, w_router, b_router, w_exp_gate, w_exp_up, w_exp_down, w_sh_gate, w_sh_up, w_sh_down):
    bp, lp, _ = x_prompt.shape
    bs, ls, _ = x_sample.shape
    l = 0
    lam = (jnp.exp(jnp.sum(lambda_q1[l] * lambda_k1[l])) - jnp.exp(jnp.sum(lambda_q2[l] * lambda_k2[l]))
           + LAMBDA_INIT).astype(F32)
    row_norm = lambda w, s: math.sqrt(HEAD_DIM_A) * jnp.max(jnp.abs(w.astype(F32))) * (s * 1.01)
    knorm = row_norm(k_norm_w[l], 1.0)
    qnorm = row_norm(q_norm_w[l], HEAD_DIM_A ** -0.5 * LOG2E)
    s_bound = qnorm * knorm
    use_bound = (2.0 * s_bound < ATTN_BOUND_MAX_GAP).astype(F32)
    lam = jnp.stack([lam, s_bound, use_bound, jnp.zeros((), F32)]).reshape(1, 4)
    p = dict(
        norm1_w=norm1_w[l], norm2_w=norm2_w[l], w_in=w_in[l].astype(BF16),
        q_norm_w=q_norm_w[l], k_norm_w=k_norm_w[l], lam=lam, subln_w=subln_w[l],
        hgrn_lb=hgrn_lb, hgrn_norm_w=hgrn_norm_w[l],
        w_proj_a=w_proj_a[l].astype(BF16), w_proj_b=w_proj_b[l].astype(BF16), w_out=w_out[l].astype(BF16),
        w_router=w_router[l], b_router=b_router[l],
        w_sh_gate=w_sh_gate[l].astype(BF16), w_sh_up=w_sh_up[l].astype(BF16), w_sh_down=w_sh_down[l].astype(BF16),
    )
    mod = _ada(jnp.concatenate([c_prompt, c_sample], axis=0), w_ada[l], b_ada[l])
    mod = mod.reshape(bp + bs, 6, 1, D_MODEL)
    mods_p = [mod[:bp, j] for j in range(6)]
    mods_s = [mod[bp:, j] for j in range(6)]

    zero_state = jnp.zeros((bp, N_HEADS, HEAD_W, HEAD_W), F32)
    zero_count = jnp.zeros((N_EXPERTS, 1), F32)
    gp = _mix(x_prompt, mods_p, None, None, zero_state, zero_count, p)
    gs = _mix(x_sample, mods_s, cache_attn_k[l], cache_attn_v[l], state_hgrn[l], gp["counts"], p)
    groups = (gp, gs)

    counts = gs["counts"].reshape(N_EXPERTS).astype(I32)
    pcounts = (counts + ROW_BLOCK - 1) // ROW_BLOCK * ROW_BLOCK
    pend = jnp.cumsum(pcounts)
    pstart = pend - pcounts
    n_assign = (bp * lp + bs * ls) * TOP_K
    nb = -(-(n_assign + N_EXPERTS * (ROW_BLOCK - 1)) // ROW_BLOCK)
    blk_e = jnp.minimum(jnp.searchsorted(pend, jnp.arange(nb, dtype=I32) * ROW_BLOCK, side="right"),
                        N_EXPERTS - 1).astype(I32)
    n_used = (pend[-1] // ROW_BLOCK).astype(I32).reshape(1)
    pstart_col = pstart.astype(F32).reshape(N_EXPERTS, 1)

    last_blk = jnp.maximum(pend // ROW_BLOCK - 1, 0).astype(I32)
    xs = _zero_tails(last_blk, nb)
    for g in groups:
        g["dest"] = _dest(g["eidx"], g["pos"], pstart_col)
        xs = _dispatch(g["hp"], g["dest"], xs, g["tm"])
    ys = _experts(xs, blk_e, n_used, w_exp_gate[l], w_exp_up[l], w_exp_down[l])
    yp, ysm = [_combine(g["base"], g["wts"].T, g["gate2"], ys, g["dest"], g["seq"], g["tm"]).reshape(g["shape"])
               for g in groups]

    return (yp, ysm,
            gp["k"].reshape(1, bp, lp, N_HEADS, 2, HEAD_DIM_A), gp["v"].reshape(1, bp, lp, N_HEADS, HEAD_W),
            gp["s"][None],
            gs["k"].reshape(1, bs, ls, N_HEADS, 2, HEAD_DIM_A), gs["v"].reshape(1, bs, ls, N_HEADS, HEAD_W),
            gs["s"][None])
```

```python
import functools
import math

import jax
import jax.numpy as jnp
from jax import lax
from jax.experimental import pallas as pl
from jax.experimental.pallas import tpu as pltpu

F32 = jnp.float32
BF16 = jnp.bfloat16
U32 = jnp.uint32
I32 = jnp.int32

D_MODEL = 1024
N_HEADS = 8
HEAD_W = 128
HEAD_DIM_A = 64
CHUNK = 64
N_EXPERTS = 256
TOP_K = 8
N_GROUPS = 8
GROUP_W = N_EXPERTS // N_GROUPS
TOPK_GROUPS = 4
D_EXPERT = 256
ROUTED_SCALE = 2.5
EPS = 1e-6
LAMBDA_INIT = 0.8 - 0.6 * math.exp(-0.3 * 0)
LANES = 128
NEG = -1e30
LOG2E = math.log2(math.e)

ATTN_BOUND_MAX_GAP = 100.0
GLA_CHUNK = 32
ROW_BLOCK = 512
VMEM_LIMIT = 48 * 1024 * 1024


def _sigmoid(x):
    return 0.5 * jnp.tanh(0.5 * x) + 0.5


def _silu(x):
    return x * _sigmoid(x)


def _cparams(sem):
    return pltpu.CompilerParams(dimension_semantics=sem, vmem_limit_bytes=VMEM_LIMIT)


ROW_PARTS = 4


def _store_packed(ref, x, row0=0):
    m, half = x.shape[0], x.shape[1] // 2
    lo = pltpu.bitcast(x[:, :half].astype(BF16).astype(F32), U32)
    hi = pltpu.bitcast(x[:, half:].astype(BF16).astype(F32), U32)
    w = (hi & jnp.uint32(0xFFFF0000)) | (lo >> 16)
    for j in range(ROW_PARTS):
        ref[pl.ds(ROW_PARTS * row0 + j, m, stride=ROW_PARTS), :] = w[:, j * LANES:(j + 1) * LANES]


def _load_packed(ref, m, row0=0):
    parts = [ref[pl.ds(ROW_PARTS * row0 + j, m, stride=ROW_PARTS), :] for j in range(ROW_PARTS)]
    lo = [pltpu.bitcast(w << 16, F32) for w in parts]
    hi = [pltpu.bitcast(w & jnp.uint32(0xFFFF0000), F32) for w in parts]
    return jnp.concatenate(lo + hi, axis=1)


def _mod_spec(a, seq, tm):
    if a.ndim == 3:
        per_b = seq // tm
        return pl.BlockSpec((1, 1, D_MODEL), lambda i: (i // per_b, 0, 0))
    return pl.BlockSpec((tm, D_MODEL), lambda i: (i, 0))


def _mod_val(ref):
    v = ref[...]
    return v.reshape(v.shape[-2], v.shape[-1])


def _ada_kernel(c_ref, w_ref, b_ref, o_ref):
    c = _silu(c_ref[...])
    o_ref[...] = jnp.dot(c, w_ref[...], precision=lax.Precision.HIGHEST,
                         preferred_element_type=F32) + b_ref[...]


def _ada(c, w_ada, b_ada):
    n = c.shape[0]
    nj = w_ada.shape[1] // D_MODEL
    return pl.pallas_call(
        _ada_kernel,
        grid=(nj,),
        in_specs=[pl.BlockSpec((n, D_MODEL), lambda j: (0, 0)),
                  pl.BlockSpec((D_MODEL, D_MODEL), lambda j: (0, j)),
                  pl.BlockSpec((1, D_MODEL), lambda j: (0, j))],
        out_specs=pl.BlockSpec((n, D_MODEL), lambda j: (0, j)),
        out_shape=jax.ShapeDtypeStruct((n, w_ada.shape[1]), F32),
        compiler_params=_cparams(("arbitrary",)),
    )(c, w_ada, b_ada.reshape(1, -1))


def _norm_mod_kernel(x_ref, w_ref, sc_ref, sh_ref, o_ref):
    x = x_ref[...]
    y = x * lax.rsqrt(jnp.mean(x * x, axis=-1, keepdims=True) + EPS) * w_ref[...]
    o_ref[...] = (y * (1.0 + _mod_val(sc_ref)) + _mod_val(sh_ref)).astype(o_ref.dtype)


def _norm_mod(x2, w, scale, shift, seq, tm):
    t = x2.shape[0]
    return pl.pallas_call(
        _norm_mod_kernel,
        grid=(t // tm,),
        in_specs=[pl.BlockSpec((tm, D_MODEL), lambda i: (i, 0)),
                  pl.BlockSpec((1, D_MODEL), lambda i: (0, 0)),
                  _mod_spec(scale, seq, tm), _mod_spec(shift, seq, tm)],
        out_specs=pl.BlockSpec((tm, D_MODEL), lambda i: (i, 0)),
        out_shape=jax.ShapeDtypeStruct((t, D_MODEL), BF16),
        compiler_params=_cparams(("parallel",)),
    )(x2, w.reshape(1, -1), scale, shift)


def _head_norm(acc, w):
    r = lax.broadcasted_iota(I32, (HEAD_W, HEAD_W), 0) // HEAD_DIM_A
    c = lax.broadcasted_iota(I32, (HEAD_W, HEAD_W), 1) // HEAD_DIM_A
    ones = (r == c).astype(BF16)
    outs = []
    for h in range(N_HEADS):
        a = acc[:, h * HEAD_W:(h + 1) * HEAD_W]
        sq = a * a
        hi = sq.astype(BF16)
        lo = (sq - hi.astype(F32)).astype(BF16)
        ss = (jnp.dot(hi, ones, preferred_element_type=F32)
              + jnp.dot(lo, ones, preferred_element_type=F32))
        outs.append(a * lax.rsqrt(ss * (1.0 / HEAD_DIM_A) + EPS) * w[:, h * HEAD_W:(h + 1) * HEAD_W])
    return jnp.concatenate(outs, axis=1)


N_MAPS = D_MODEL // HEAD_DIM_A


def _proj_kernel(h_ref, w_ref, nw_ref, *o_refs, head_norm, out_scale, split_first):
    acc = jnp.dot(h_ref[...], w_ref[...], preferred_element_type=F32)
    if head_norm:
        acc = _head_norm(acc, nw_ref[...])
    for n, (o_ref, s) in enumerate(zip(o_refs, out_scale)):
        val = (acc if s == 1.0 else acc * s).astype(o_ref.dtype)
        if split_first and n == 0:
            for g in range(N_MAPS):
                o_ref[pl.ds(g, val.shape[0], stride=N_MAPS), :] = val[:, g * HEAD_DIM_A:(g + 1) * HEAD_DIM_A]
        else:
            o_ref[...] = val


def _proj(h, w_in_bf, group, out_dtypes, norm_w=None, out_scale=None, split_first=False, tm=1024):
    t = h.shape[0]
    tm = min(tm, t)
    nw = jnp.ones((1, D_MODEL), F32) if norm_w is None else norm_w
    out_scale = tuple(out_scale or (1.0,) * len(out_dtypes))
    specs = [pl.BlockSpec((tm, D_MODEL), lambda i: (i, 0)) for _ in out_dtypes]
    shapes = [jax.ShapeDtypeStruct((t, D_MODEL), dt) for dt in out_dtypes]
    if split_first:
        specs[0] = pl.BlockSpec((N_MAPS * tm, HEAD_DIM_A), lambda i: (i, 0))
        shapes[0] = jax.ShapeDtypeStruct((N_MAPS * t, HEAD_DIM_A), out_dtypes[0])
    outs = pl.pallas_call(
        functools.partial(_proj_kernel, head_norm=norm_w is not None, out_scale=out_scale,
                          split_first=split_first),
        grid=(t // tm,),
        in_specs=[pl.BlockSpec((tm, D_MODEL), lambda i: (i, 0)),
                  pl.BlockSpec((D_MODEL, D_MODEL), lambda i: (0, group)),
                  pl.BlockSpec((1, D_MODEL), lambda i: (0, 0))],
        out_specs=specs,
        out_shape=shapes,
        compiler_params=_cparams(("parallel",)),
    )(h, w_in_bf, nw)
    return outs


def _split_maps(q):
    lane = lax.broadcasted_iota(I32, q.shape, 1)
    zero = jnp.zeros_like(q)
    return jnp.where(lane < HEAD_DIM_A, q, zero), jnp.where(lane >= HEAD_DIM_A, q, zero)


def _finish_head(acc1, l1, acc2, l2, lam, sw):
    o = acc1 * (1.0 / l1) - acc2 * (lam / l2)
    o = o * lax.rsqrt(jnp.mean(o * o, axis=-1, keepdims=True) + EPS) * sw
    return o * (1.0 - LAMBDA_INIT)


def _attn_kernel(q_ref, k_ref, v_ref, lam_ref, sw_ref, o_ref, mx_s, mrep_s, ls_s, acc_s,
                 *, tq, tk, hp):
    qi = pl.program_id(2)
    nf = tk // LANES
    qs = []
    for hh in range(hp):
        qs.extend(_split_maps(q_ref[:, hh * HEAD_W:(hh + 1) * HEAD_W]))
    ns = 2 * hp

    r_chunk = lax.broadcasted_iota(I32, (tq, tk), 0) // CHUNK
    c_chunk = lax.broadcasted_iota(I32, (tq, tk), 1) // CHUNK
    diag_ok = c_chunk <= r_chunk

    def scores(kt, s, masked):
        k = k_ref[pl.ds(pl.multiple_of(kt * tk, tk), tk), (s // 2) * HEAD_W:(s // 2 + 1) * HEAD_W]
        sc = lax.dot_general(qs[s], k, (((1,), (1,)), ((), ())), preferred_element_type=F32)
        return jnp.where(diag_ok, sc, NEG) if masked else sc

    def fold(x, op):
        r = x[:, :LANES]
        for j in range(1, nf):
            r = op(r, x[:, j * LANES:(j + 1) * LANES])
        return r

    def pass1(kt, first):
        for s in range(ns):
            m = fold(scores(kt, s, first), jnp.maximum)
            mx_s[s] = m if first else jnp.maximum(mx_s[s], m)

    def pass2(kt, first):
        for s in range(ns):
            m = mrep_s[s]
            p = jnp.exp2(scores(kt, s, first) - jnp.concatenate([m] * nf, axis=1))
            v = v_ref[pl.ds(pl.multiple_of(kt * tk, tk), tk), (s // 2) * HEAD_W:(s // 2 + 1) * HEAD_W]
            pv = jnp.dot(p.astype(BF16), v, preferred_element_type=F32)
            ls_s[s] = fold(p, jnp.add) if first else ls_s[s] + fold(p, jnp.add)
            acc_s[s] = pv if first else acc_s[s] + pv

    def loop(fn):
        def body(kt, c):
            fn(kt, False)
            return c

        fn(qi, True)
        lax.fori_loop(0, qi, body, 0)

    use_bound = lam_ref[0, 2] > 0.5

    @pl.when(use_bound)
    def _():
        mrep_s[...] = jnp.full(mrep_s.shape, lam_ref[0, 1], F32)

    @pl.when(jnp.logical_not(use_bound))
    def _():
        loop(pass1)
        for s in range(ns):
            mrep_s[s] = jnp.broadcast_to(jnp.max(mx_s[s], axis=-1, keepdims=True), (tq, LANES))

    loop(pass2)

    lam = lam_ref[0, 0]
    for hh in range(hp):
        l1 = jnp.sum(ls_s[2 * hh], axis=-1, keepdims=True)
        l2 = jnp.sum(ls_s[2 * hh + 1], axis=-1, keepdims=True)
        o = _finish_head(acc_s[2 * hh], l1, acc_s[2 * hh + 1], l2, lam, sw_ref[...])
        o_ref[:, hh * HEAD_W:(hh + 1) * HEAD_W] = o.astype(o_ref.dtype)


def _attention_prompt(q, kb, vb, lam, subln_w, batch, seq):
    tq = tk = min(seq, 256)
    hp = 8
    nq = seq // tq
    kern = functools.partial(_attn_kernel, tq=tq, tk=tk, hp=hp)
    w = hp * HEAD_W
    return pl.pallas_call(
        kern,
        grid=(batch, N_HEADS // hp, nq),
        in_specs=[pl.BlockSpec((tq, w), lambda b, h, i: (b * nq + i, h)),
                  pl.BlockSpec((seq, w), lambda b, h, i: (b, h)),
                  pl.BlockSpec((seq, w), lambda b, h, i: (b, h)),
                  pl.BlockSpec((1, 4), lambda b, h, i: (0, 0), memory_space=pltpu.SMEM),
                  pl.BlockSpec((1, HEAD_W), lambda b, h, i: (0, 0))],
        out_specs=pl.BlockSpec((tq, w), lambda b, h, i: (b * nq + i, h)),
        out_shape=jax.ShapeDtypeStruct((batch * seq, D_MODEL), BF16),
        scratch_shapes=[pltpu.VMEM((2 * hp, tq, LANES), F32) for _ in range(4)],
        compiler_params=_cparams(("parallel", "parallel", "arbitrary")),
    )(q, kb, vb, lam, subln_w.reshape(1, -1))


def _attn_cached_kernel(q_ref, k_ref, v_ref, lam_ref, sw_ref, o_ref, *, lq, lk):
    qpos = (lk - lq) + lax.broadcasted_iota(I32, (lq, lk), 0)
    kpos = lax.broadcasted_iota(I32, (lq, lk), 1)
    allowed = (kpos // CHUNK) <= (qpos // CHUNK)
    lam = lam_ref[0, 0]
    for h in range(N_HEADS):
        sl = slice(h * HEAD_W, (h + 1) * HEAD_W)
        k = k_ref[:, sl]
        v = v_ref[:, sl]
        accs, ls = [], []
        for qm in _split_maps(q_ref[:, sl]):
            sc = lax.dot_general(qm, k, (((1,), (1,)), ((), ())), preferred_element_type=F32)
            sc = jnp.where(allowed, sc, NEG)
            p = jnp.exp2(sc - jnp.max(sc, axis=-1, keepdims=True))
            ls.append(jnp.sum(p, axis=-1, keepdims=True))
            accs.append(jnp.dot(p.astype(BF16), v, preferred_element_type=F32))
        o = _finish_head(accs[0], ls[0], accs[1], ls[1], lam, sw_ref[...])
        o_ref[:, sl] = o.astype(o_ref.dtype)


def _attention_cached(q, kb, vb, lam, subln_w, batch, lq, lk):
    return pl.pallas_call(
        functools.partial(_attn_cached_kernel, lq=lq, lk=lk),
        grid=(batch,),
        in_specs=[pl.BlockSpec((lq, D_MODEL), lambda b: (b, 0)),
                  pl.BlockSpec((lk, D_MODEL), lambda b: (b, 0)),
                  pl.BlockSpec((lk, D_MODEL), lambda b: (b, 0)),
                  pl.BlockSpec((1, 4), lambda b: (0, 0), memory_space=pltpu.SMEM),
                  pl.BlockSpec((1, HEAD_W), lambda b: (0, 0))],
        out_specs=pl.BlockSpec((lq, D_MODEL), lambda b: (b, 0)),
        out_shape=jax.ShapeDtypeStruct((batch * lq, D_MODEL), BF16),
        compiler_params=_cparams(("parallel",)),
    )(q, kb, vb, lam, subln_w.reshape(1, -1))


def _hgrn_kernel(qh_ref, fh_ref, ih_ref, gh_ref, lb_ref, nw_ref, s0_ref, o_ref, s_out_ref,
                 st_s, qin_s, qmid_s, kmid_s, kend_s, dec_s, *, ct, c):
    t = pl.program_id(1)
    nt = pl.num_programs(1)
    nc = ct // c

    @pl.when(t == 0)
    def _():
        for h in range(N_HEADS):
            st_s[h] = s0_ref[0, h].T

    a = lb_ref[...]
    amax = jnp.max(a, axis=0, keepdims=True)
    e = jnp.exp(a - amax)
    lb = e[0:1] / jnp.sum(e, axis=0, keepdims=True)

    row = lax.broadcasted_iota(I32, (c, c), 0)
    col = lax.broadcasted_iota(I32, (c, c), 1)
    causal = col <= row
    t_idx = lax.broadcasted_iota(I32, (c, D_MODEL), 0)

    def cumsum_rows(g):
        sh = 1
        while sh < c:
            g = g + jnp.where(t_idx >= sh, pltpu.roll(g, sh, axis=0), 0.0)
            sh *= 2
        return g

    for ci in range(nc):
        rows = slice(ci * c, (ci + 1) * c)
        f = lb + (1.0 - lb) * _sigmoid(fh_ref[rows, :])
        b = cumsum_rows(jnp.log2(f))
        b_last = b[c - 1:c, :]
        b_mid = b[c // 2 - 1:c // 2, :]
        q = _silu(qh_ref[rows, :].astype(F32)) * (HEAD_W ** -0.5)
        kk = 1.0 - f
        e_dn = jnp.exp2(b - b_mid)
        e_up = jnp.exp2(b_mid - b)
        qm = q * e_dn
        km = kk * e_up
        qmid_s[rows, :] = qm.astype(BF16)
        kmid_s[rows, :] = km.astype(BF16)
        qin_s[rows, :] = (qm * jnp.exp2(b_mid)).astype(BF16)
        kend_s[rows, :] = (km * jnp.exp2(b_last - b_mid)).astype(BF16)
        dec_s[ci:ci + 1, :] = jnp.exp2(b_last)

    nw = nw_ref[...]
    st = [st_s[h] for h in range(N_HEADS)]
    for ci in range(nc):
        rows = slice(ci * c, (ci + 1) * c)
        gate = _silu(gh_ref[rows, :].astype(F32))
        decay = dec_s[ci:ci + 1, :]
        for h in range(N_HEADS):
            sl = slice(h * HEAD_W, (h + 1) * HEAD_W)
            v = ih_ref[rows, sl]
            inter = lax.dot_general(qin_s[rows, sl], st[h].astype(BF16), (((1,), (1,)), ((), ())),
                                    preferred_element_type=F32)
            att = lax.dot_general(qmid_s[rows, sl], kmid_s[rows, sl], (((1,), (1,)), ((), ())),
                                  preferred_element_type=F32)
            att = jnp.where(causal, att, 0.0)
            o = inter + jnp.dot(att.astype(BF16), v, preferred_element_type=F32)
            upd = lax.dot_general(v, kend_s[rows, sl], (((0,), (0,)), ((), ())),
                                  preferred_element_type=F32)
            st[h] = decay[:, sl] * st[h] + upd
            o = o * lax.rsqrt(jnp.mean(o * o, axis=-1, keepdims=True) + EPS) * nw[:, sl]
            o_ref[rows, sl] = (o * gate[:, sl]).astype(o_ref.dtype)
    for h in range(N_HEADS):
        st_s[h] = st[h]

    @pl.when(t == nt - 1)
    def _():
        for h in range(N_HEADS):
            s_out_ref[0, h] = st_s[h].T


def _hgrn(qh, fh, ih, gh, hgrn_lb, norm_w, s0, batch, seq):
    ct = min(seq, 512)
    c = min(GLA_CHUNK, ct)
    nt = seq // ct
    kern = functools.partial(_hgrn_kernel, ct=ct, c=c)
    tok = pl.BlockSpec((ct, D_MODEL), lambda b, t: (b * nt + t, 0))
    st_spec = pl.BlockSpec((1, N_HEADS, HEAD_W, HEAD_W), lambda b, t: (b, 0, 0, 0))
    return pl.pallas_call(
        kern,
        grid=(batch, nt),
        in_specs=[tok, tok, tok, tok,
                  pl.BlockSpec((2, D_MODEL), lambda b, t: (0, 0)),
                  pl.BlockSpec((1, D_MODEL), lambda b, t: (0, 0)),
                  st_spec],
        out_specs=[tok, st_spec],
        out_shape=[jax.ShapeDtypeStruct((batch * seq, D_MODEL), BF16),
                   jax.ShapeDtypeStruct((batch, N_HEADS, HEAD_W, HEAD_W), F32)],
        scratch_shapes=[pltpu.VMEM((N_HEADS, HEAD_W, HEAD_W), F32)]
        + [pltpu.VMEM((ct, D_MODEL), BF16) for _ in range(4)]
        + [pltpu.VMEM((max(ct // c, 8), D_MODEL), F32)],
        compiler_params=_cparams(("parallel", "arbitrary")),
    )(qh, fh, ih, gh, hgrn_lb, jnp.tile(norm_w.reshape(1, -1), (1, N_HEADS)), s0)


def _route(h2, wrt_ref, brt_ref, carry_s, tm):
    nt_dims = (((1,), (1,)), ((), ()))
    w_hi = wrt_ref[0]
    w_lo = wrt_ref[1]
    h_hi = h2.astype(BF16)
    h_lo = (h2 - h_hi.astype(F32)).astype(BF16)
    logits = (lax.dot_general(w_hi, h_hi, nt_dims, preferred_element_type=F32)
              + lax.dot_general(w_hi, h_lo, nt_dims, preferred_element_type=F32)
              + lax.dot_general(w_lo, h_hi, nt_dims, preferred_element_type=F32))
    scores = _sigmoid(logits)
    biased = scores + brt_ref[...]
    big = float(2 * N_EXPERTS)

    x3 = biased.reshape(N_GROUPS, GROUP_W, tm)
    i3 = lax.broadcasted_iota(I32, x3.shape, 1).astype(F32)
    m1 = jnp.max(x3, axis=1, keepdims=True)
    i1 = jnp.min(jnp.where(x3 == m1, i3, big), axis=1, keepdims=True)
    m2 = jnp.max(jnp.where(i3 == i1, NEG, x3), axis=1, keepdims=True)
    gs = (m1 + m2).reshape(N_GROUPS, tm)

    g_iota = lax.broadcasted_iota(I32, (N_GROUPS, tm), 0).astype(F32)
    rem = gs
    gsel = jnp.zeros((N_GROUPS, tm), F32)
    for _ in range(TOPK_GROUPS):
        gm = jnp.max(rem, axis=0, keepdims=True)
        first = jnp.min(jnp.where(rem == gm, g_iota, big), axis=0, keepdims=True)
        sel = g_iota == first
        gsel = jnp.where(sel, 1.0, gsel)
        rem = jnp.where(sel, NEG, rem)
    keep = jnp.broadcast_to(gsel.reshape(N_GROUPS, 1, tm), x3.shape) > 0.5
    masked = jnp.where(keep, x3, NEG).reshape(N_EXPERTS, tm)

    e_iota = lax.broadcasted_iota(I32, (N_EXPERTS, tm), 0).astype(F32)
    onehot = jnp.zeros((N_EXPERTS, tm), F32)
    idxs, ws = [], []
    for _ in range(TOP_K):
        mk = jnp.max(masked, axis=0, keepdims=True)
        ik = jnp.min(jnp.where(masked == mk, e_iota, big), axis=0, keepdims=True)
        selk = e_iota == ik
        ws.append(jnp.sum(jnp.where(selk, scores, 0.0), axis=0, keepdims=True))
        idxs.append(ik)
        masked = jnp.where(selk, NEG, masked)
        onehot = jnp.where(selk, 1.0, onehot)
    wsum = ws[0]
    for k in range(1, TOP_K):
        wsum = wsum + ws[k]

    r = lax.broadcasted_iota(I32, (tm, tm), 0)
    cidx = lax.broadcasted_iota(I32, (tm, tm), 1)
    before = (r < cidx).astype(BF16)
    cum = jnp.dot(onehot.astype(BF16), before, preferred_element_type=F32) + carry_s[...]
    carry_s[...] = carry_s[...] + jnp.sum(onehot, axis=1, keepdims=True)

    k_iota = lax.broadcasted_iota(I32, (TOP_K, tm), 0)
    e_out = jnp.zeros((TOP_K, tm), F32)
    w_out = jnp.zeros((TOP_K, tm), F32)
    p_out = jnp.zeros((TOP_K, tm), F32)
    for k in range(TOP_K):
        pk = jnp.sum(jnp.where(e_iota == idxs[k], cum, 0.0), axis=0, keepdims=True)
        e_out = jnp.where(k_iota == k, idxs[k], e_out)
        w_out = jnp.where(k_iota == k, ws[k] / wsum * ROUTED_SCALE, w_out)
        p_out = jnp.where(k_iota == k, pk, p_out)
    return e_out, w_out, p_out


def _post_kernel(oa_ref, ob_ref, ga_ref, gb_ref, x_ref, g1_ref, sc2_ref, sh2_ref, g2_ref,
                 wpa_ref, wpb_ref, wo_ref, n2_ref, wrt_ref, brt_ref, wsg_ref, wsu_ref, wsd_ref, c0_ref,
                 base_ref, hp_ref, eidx_ref, wts_ref, pos_ref, cnt_ref, carry_s, *, tm):
    i = pl.program_id(0)

    @pl.when(i == 0)
    def _():
        carry_s[...] = c0_ref[...]

    pa = jnp.dot(oa_ref[...], wpa_ref[...], preferred_element_type=F32)
    pb = jnp.dot(ob_ref[...], wpb_ref[...], preferred_element_type=F32)
    u = _sigmoid(ga_ref[...].astype(F32)) * pa + _sigmoid(gb_ref[...].astype(F32)) * pb
    x1 = x_ref[...] + _mod_val(g1_ref) * jnp.dot(u.astype(BF16), wo_ref[...], preferred_element_type=F32)
    y = x1 * lax.rsqrt(jnp.mean(x1 * x1, axis=-1, keepdims=True) + EPS) * n2_ref[...]
    h2 = y * (1.0 + _mod_val(sc2_ref)) + _mod_val(sh2_ref)
    _store_packed(hp_ref, h2)

    hb = h2.astype(BF16)
    sg = jnp.dot(hb, wsg_ref[...], preferred_element_type=F32)
    su = jnp.dot(hb, wsu_ref[...], preferred_element_type=F32)
    shared = jnp.dot((_silu(sg) * su).astype(BF16), wsd_ref[...], preferred_element_type=F32)
    base_ref[...] = x1 + _mod_val(g2_ref) * shared

    e_out, w_out, p_out = _route(h2, wrt_ref, brt_ref, carry_s, tm)
    eidx_ref[...] = e_out.astype(I32)
    wts_ref[...] = w_out
    pos_ref[...] = p_out.astype(I32)
    cnt_ref[...] = carry_s[...]


def _post(oa, ob, ga, gb, x2, g1, sc2, sh2, g2, wpa, wpb, wo, n2, wr, br, wsg, wsu, wsd, count0, seq, tm):
    t = x2.shape[0]
    tok = lambda: pl.BlockSpec((tm, D_MODEL), lambda i: (i, 0))
    mod = lambda: _mod_spec(g1, seq, tm)
    full = lambda a: pl.BlockSpec(a.shape, lambda i: (0,) * a.ndim)
    n2 = n2.reshape(1, -1)
    wrt_hi = wr.T.astype(BF16)
    wrt = jnp.stack([wrt_hi, (wr.T - wrt_hi.astype(F32)).astype(BF16)])
    brt = br.reshape(-1, 1)
    k_out = lambda: pl.BlockSpec((TOP_K, tm), lambda i: (0, i))
    return pl.pallas_call(
        functools.partial(_post_kernel, tm=tm),
        grid=(t // tm,),
        in_specs=[tok(), tok(), tok(), tok(), tok(), mod(), mod(), mod(), mod(),
                  full(wpa), full(wpb), full(wo), full(n2), full(wrt), full(brt),
                  full(wsg), full(wsu), full(wsd), full(count0)],
        out_specs=[tok(), pl.BlockSpec((ROW_PARTS * tm, LANES), lambda i: (i, 0)),
                   k_out(), k_out(), k_out(),
                   pl.BlockSpec((N_EXPERTS, 1), lambda i: (0, 0))],
        out_shape=[jax.ShapeDtypeStruct((t, D_MODEL), F32),
                   jax.ShapeDtypeStruct((ROW_PARTS * t, LANES), U32),
                   jax.ShapeDtypeStruct((TOP_K, t), I32),
                   jax.ShapeDtypeStruct((TOP_K, t), F32),
                   jax.ShapeDtypeStruct((TOP_K, t), I32),
                   jax.ShapeDtypeStruct((N_EXPERTS, 1), F32)],
        scratch_shapes=[pltpu.VMEM((N_EXPERTS, 1), F32)],
        compiler_params=_cparams(("arbitrary",)),
    )(oa, ob, ga, gb, x2, g1, sc2, sh2, g2, wpa, wpb, wo, n2, wrt, brt, wsg, wsu, wsd, count0)


def _dest_kernel(eidx_ref, pos_ref, pstart_ref, o_ref, *, tm):
    e_iota = lax.broadcasted_iota(I32, (N_EXPERTS, tm), 0)
    k_iota = lax.broadcasted_iota(I32, (TOP_K, tm), 0)
    eidx = eidx_ref[...]
    start = jnp.zeros((TOP_K, tm), F32)
    for k in range(TOP_K):
        sel = e_iota == eidx[k:k + 1, :]
        sk = jnp.sum(jnp.where(sel, pstart_ref[...], 0.0), axis=0, keepdims=True)
        start = jnp.where(k_iota == k, sk, start)
    o_ref[...] = (start.astype(I32) + pos_ref[...]) * ROW_PARTS


def _dest(eidx, pos, pstart_col):
    t = eidx.shape[1]
    tm = min(t, 1024)
    blk = lambda: pl.BlockSpec((TOP_K, tm), lambda i: (0, i))
    return pl.pallas_call(
        functools.partial(_dest_kernel, tm=tm),
        grid=(t // tm,),
        in_specs=[blk(), blk(), pl.BlockSpec((N_EXPERTS, 1), lambda i: (0, 0))],
        out_specs=blk(),
        out_shape=jax.ShapeDtypeStruct((TOP_K, t), I32),
        compiler_params=_cparams(("parallel",)),
    )(eidx, pos, pstart_col)


def _zero_tail_kernel(last_ref, o_ref):
    del last_ref
    o_ref[...] = jnp.zeros(o_ref.shape, o_ref.dtype)


def _zero_tails(last_blk, nb):
    grid_spec = pltpu.PrefetchScalarGridSpec(
        num_scalar_prefetch=1,
        grid=(N_EXPERTS,),
        in_specs=[],
        out_specs=pl.BlockSpec((ROW_PARTS * ROW_BLOCK, LANES), lambda e, last: (last[e], 0)),
    )
    return pl.pallas_call(
        _zero_tail_kernel,
        grid_spec=grid_spec,
        out_shape=jax.ShapeDtypeStruct((ROW_PARTS * nb * ROW_BLOCK, LANES), U32),
        compiler_params=_cparams(("arbitrary",)),
    )(last_blk)


def _dispatch_kernel(dest_ref, hp_ref, xs_in_ref, xs_ref, sem, *, tm):
    del xs_in_ref

    def row_copy(t, d):
        return pltpu.make_async_copy(
            hp_ref.at[pl.ds(pl.multiple_of(t * ROW_PARTS, ROW_PARTS), ROW_PARTS), :],
            xs_ref.at[pl.ds(pl.multiple_of(d, ROW_PARTS), ROW_PARTS), :], sem)

    def issue(t, c):
        for k in range(TOP_K):
            row_copy(t, dest_ref[k, t]).start(priority=k % 2)
        return c

    def drain(t, c):
        for k in range(TOP_K):
            row_copy(0, 0).wait()
        return c

    lax.fori_loop(0, tm, issue, 0)
    lax.fori_loop(0, tm, drain, 0)


def _dispatch(hp, dest, xs, tm):
    t = dest.shape[1]
    return pl.pallas_call(
        functools.partial(_dispatch_kernel, tm=tm),
        grid=(t // tm,),
        in_specs=[pl.BlockSpec((TOP_K, tm), lambda i: (0, i), memory_space=pltpu.SMEM),
                  pl.BlockSpec((ROW_PARTS * tm, LANES), lambda i: (i, 0)),
                  pl.BlockSpec(memory_space=pl.ANY)],
        out_specs=pl.BlockSpec(memory_space=pl.ANY),
        out_shape=jax.ShapeDtypeStruct(xs.shape, xs.dtype),
        scratch_shapes=[pltpu.SemaphoreType.DMA],
        input_output_aliases={2: 0},
        compiler_params=_cparams(("arbitrary",)),
    )(dest, hp, xs)


def _expert_kernel(be_ref, nu_ref, xs_ref, wg_ref, wu_ref, wd_ref, ys_ref, wg_s, wu_s, wd_s):
    i = pl.program_id(0)
    used = i < nu_ref[0]

    @pl.when(jnp.logical_and(used, jnp.logical_or(i == 0, be_ref[i] != be_ref[jnp.maximum(i - 1, 0)])))
    def _():
        wg_s[...] = wg_ref[0].astype(BF16)
        wu_s[...] = wu_ref[0].astype(BF16)
        wd_s[...] = wd_ref[0].astype(BF16)

    @pl.when(used)
    def _():
        x = _load_packed(xs_ref, ROW_BLOCK).astype(BF16)
        g = jnp.dot(x, wg_s[...], preferred_element_type=F32)
        u = jnp.dot(x, wu_s[...], preferred_element_type=F32)
        hmid = (_silu(g) * u).astype(BF16)
        _store_packed(ys_ref, jnp.dot(hmid, wd_s[...], preferred_element_type=F32))

    @pl.when(jnp.logical_not(used))
    def _():
        ys_ref[...] = jnp.zeros(ys_ref.shape, ys_ref.dtype)


def _experts(xs, blk_e, n_used, w_eg, w_eu, w_ed):
    nb = xs.shape[0] // (ROW_PARTS * ROW_BLOCK)
    rows_spec = lambda: pl.BlockSpec((ROW_PARTS * ROW_BLOCK, LANES), lambda i, be, nu: (i, 0))
    used_rows_spec = pl.BlockSpec((ROW_PARTS * ROW_BLOCK, LANES),
                                  lambda i, be, nu: (jnp.maximum(jnp.minimum(i, nu[0] - 1), 0), 0))
    w_up = lambda: pl.BlockSpec((1, D_MODEL, D_EXPERT), lambda i, be, nu: (be[i], 0, 0))
    grid_spec = pltpu.PrefetchScalarGridSpec(
        num_scalar_prefetch=2,
        grid=(nb,),
        in_specs=[used_rows_spec, w_up(), w_up(),
                  pl.BlockSpec((1, D_EXPERT, D_MODEL), lambda i, be, nu: (be[i], 0, 0))],
        out_specs=rows_spec(),
        scratch_shapes=[pltpu.VMEM((D_MODEL, D_EXPERT), BF16), pltpu.VMEM((D_MODEL, D_EXPERT), BF16),
                        pltpu.VMEM((D_EXPERT, D_MODEL), BF16)],
    )
    return pl.pallas_call(
        _expert_kernel,
        grid_spec=grid_spec,
        out_shape=jax.ShapeDtypeStruct(xs.shape, U32),
        compiler_params=_cparams(("arbitrary",)),
    )(blk_e, n_used, xs, w_eg, w_eu, w_ed)


def _combine_kernel(dest_ref, base_ref, wts_ref, g2_ref, ys_ref, o_ref, buf, sem, *, tm):
    def row_copy(d, k, t):
        return pltpu.make_async_copy(
            ys_ref.at[pl.ds(pl.multiple_of(d, ROW_PARTS), ROW_PARTS), :],
            buf.at[k, pl.ds(pl.multiple_of(t * ROW_PARTS, ROW_PARTS), ROW_PARTS), :], sem)

    def issue(t, c):
        for k in range(TOP_K):
            row_copy(dest_ref[k, t], k, t).start(priority=k % 2)
        return c

    def drain(t, c):
        for k in range(TOP_K):
            row_copy(0, 0, 0).wait()
        return c

    lax.fori_loop(0, tm, issue, 0)
    lax.fori_loop(0, tm, drain, 0)

    w = wts_ref[...]
    y = jnp.zeros((tm, D_MODEL), F32)
    for k in range(TOP_K):
        y = y + w[:, k:k + 1] * _load_packed(buf.at[k], tm)
    o_ref[...] = base_ref[...] + _mod_val(g2_ref) * y


def _combine(base, wts_t, g2, ys, dest, seq, tm):
    t = base.shape[0]
    return pl.pallas_call(
        functools.partial(_combine_kernel, tm=tm),
        grid=(t // tm,),
        in_specs=[pl.BlockSpec((TOP_K, tm), lambda i: (0, i), memory_space=pltpu.SMEM),
                  pl.BlockSpec((tm, D_MODEL), lambda i: (i, 0)),
                  pl.BlockSpec((tm, TOP_K), lambda i: (i, 0)),
                  _mod_spec(g2, seq, tm),
                  pl.BlockSpec(memory_space=pl.ANY)],
        out_specs=pl.BlockSpec((tm, D_MODEL), lambda i: (i, 0)),
        out_shape=jax.ShapeDtypeStruct((t, D_MODEL), F32),
        scratch_shapes=[pltpu.VMEM((TOP_K, ROW_PARTS * tm, LANES), U32), pltpu.SemaphoreType.DMA],
        compiler_params=_cparams(("arbitrary",)),
    )(dest, base, wts_t, g2, ys)


def _mix(x, mods, past_k, past_v, s0, count0, p):
    batch, seq, _ = x.shape
    t = batch * seq
    x2 = x.reshape(t, D_MODEL)
    if seq % 256 == 0:
        tile = lambda cap: min(seq, cap)
    else:
        mods = [jnp.broadcast_to(m, (batch, seq, D_MODEL)).reshape(t, D_MODEL) for m in mods]
        tile = lambda cap: min(t, cap)
    shift1, scale1, gate1, shift2, scale2, gate2 = mods

    h = _norm_mod(x2, p["norm1_w"], scale1, shift1, seq, tile(512))
    qn = jnp.tile(p["q_norm_w"].reshape(1, -1), (1, D_MODEL // HEAD_DIM_A))
    kn = jnp.tile(p["k_norm_w"].reshape(1, -1), (1, D_MODEL // HEAD_DIM_A))
    w_in = p["w_in"]
    (q,) = _proj(h, w_in, 0, [BF16], qn, out_scale=(HEAD_DIM_A ** -0.5 * LOG2E,))
    k, kb = _proj(h, w_in, 1, [F32, BF16], kn, split_first=True, tm=512)
    v, vb = _proj(h, w_in, 2, [F32, BF16])
    (qh,) = _proj(h, w_in, 3, [BF16])
    (fh,) = _proj(h, w_in, 4, [F32])
    (ih,) = _proj(h, w_in, 5, [BF16])
    (gh,) = _proj(h, w_in, 6, [BF16])
    (ga,) = _proj(h, w_in, 7, [BF16])
    (gb,) = _proj(h, w_in, 8, [BF16])

    if past_k is None:
        o_a = _attention_prompt(q, kb, vb, p["lam"], p["subln_w"], batch, seq)
    else:
        past = past_k.shape[1]
        lk = past + seq
        kb = jnp.concatenate([past_k.reshape(batch, past, D_MODEL).astype(BF16),
                              kb.reshape(batch, seq, D_MODEL)], axis=1).reshape(batch * lk, D_MODEL)
        vb = jnp.concatenate([past_v.reshape(batch, past, D_MODEL).astype(BF16),
                              vb.reshape(batch, seq, D_MODEL)], axis=1).reshape(batch * lk, D_MODEL)
        o_a = _attention_cached(q, kb, vb, p["lam"], p["subln_w"], batch, seq, lk)
    o_b, s_new = _hgrn(qh, fh, ih, gh, p["hgrn_lb"], p["hgrn_norm_w"], s0, batch, seq)

    base, hp, eidx, wts, pos, counts = _post(
        o_a, o_b, ga, gb, x2, gate1, scale2, shift2, gate2,
        p["w_proj_a"], p["w_proj_b"], p["w_out"], p["norm2_w"], p["w_router"], p["b_router"],
        p["w_sh_gate"], p["w_sh_up"], p["w_sh_down"], count0, seq, tile(512))
    return dict(base=base, hp=hp, eidx=eidx, wts=wts, pos=pos, counts=counts, k=k, v=v, s=s_new,
                gate2=gate2, seq=seq, tm=tile(512), shape=x.shape)


def kernel(x_prompt, x_sample, cache_attn_k, cache_attn_v, state_hgrn, c_prompt, c_sample, w_ada, b_ada, norm1_w, norm2_w, w_in, q_norm_w, k_norm_w, lambda_q1, lambda_k1, lambda_q2, lambda_k2, subln_w, hgrn_lb, hgrn_norm_w, w_proj_a, w_proj_b, w_out, w_router, b_router, w_exp_gate, w_exp_up, w_exp_down, w_sh_gate, w_sh_up, w_sh_down):
    bp, lp, _ = x_prompt.shape
    bs, ls, _ = x_sample.shape
    l = 0
    lam = (jnp.exp(jnp.sum(lambda_q1[l] * lambda_k1[l])) - jnp.exp(jnp.sum(lambda_q2[l] * lambda_k2[l]))
           + LAMBDA_INIT).astype(F32)
    row_norm = lambda w, s: math.sqrt(HEAD_DIM_A) * jnp.max(jnp.abs(w.astype(F32))) * (s * 1.01)
    knorm = row_norm(k_norm_w[l], 1.0)
    qnorm = row_norm(q_norm_w[l], HEAD_DIM_A ** -0.5 * LOG2E)
    s_bound = qnorm * knorm
    use_bound = (2.0 * s_bound < ATTN_BOUND_MAX_GAP).astype(F32)
    lam = jnp.stack([lam, s_bound, use_bound, jnp.zeros((), F32)]).reshape(1, 4)
    p = dict(
        norm1_w=norm1_w[l], norm2_w=norm2_w[l], w_in=w_in[l].astype(BF16),
        q_norm_w=q_norm_w[l], k_norm_w=k_norm_w[l], lam=lam, subln_w=subln_w[l],
        hgrn_lb=hgrn_lb, hgrn_norm_w=hgrn_norm_w[l],
        w_proj_a=w_proj_a[l].astype(BF16), w_proj_b=w_proj_b[l].astype(BF16), w_out=w_out[l].astype(BF16),
        w_router=w_router[l], b_router=b_router[l],
        w_sh_gate=w_sh_gate[l].astype(BF16), w_sh_up=w_sh_up[l].astype(BF16), w_sh_down=w_sh_down[l].astype(BF16),
    )
    mod = _ada(jnp.concatenate([c_prompt, c_sample], axis=0), w_ada[l], b_ada[l])
    mod = mod.reshape(bp + bs, 6, 1, D_MODEL)
    mods_p = [mod[:bp, j] for j in range(6)]
    mods_s = [mod[bp:, j] for j in range(6)]

    zero_state = jnp.zeros((bp, N_HEADS, HEAD_W, HEAD_W), F32)
    zero_count = jnp.zeros((N_EXPERTS, 1), F32)
    gp = _mix(x_prompt, mods_p, None, None, zero_state, zero_count, p)
    gs = _mix(x_sample, mods_s, cache_attn_k[l], cache_attn_v[l], state_hgrn[l], gp["counts"], p)
    groups = (gp, gs)

    counts = gs["counts"].reshape(N_EXPERTS).astype(I32)
    pcounts = (counts + ROW_BLOCK - 1) // ROW_BLOCK * ROW_BLOCK
    pend = jnp.cumsum(pcounts)
    pstart = pend - pcounts
    n_assign = (bp * lp + bs * ls) * TOP_K
    nb = -(-(n_assign + N_EXPERTS * (ROW_BLOCK - 1)) // ROW_BLOCK)
    blk_start = jnp.arange(nb, dtype=I32)[:, None] * ROW_BLOCK
    blk_e = jnp.minimum(jnp.sum((pend[None, :] <= blk_start).astype(I32), axis=1), N_EXPERTS - 1)
    n_used = (pend[-1] // ROW_BLOCK).astype(I32).reshape(1)
    pstart_col = pstart.astype(F32).reshape(N_EXPERTS, 1)

    last_blk = jnp.maximum(pend // ROW_BLOCK - 1, 0).astype(I32)
    xs = _zero_tails(last_blk, nb)
    for g in groups:
        g["dest"] = _dest(g["eidx"], g["pos"], pstart_col)
        xs = _dispatch(g["hp"], g["dest"], xs, g["tm"])
    ys = _experts(xs, blk_e, n_used, w_exp_gate[l], w_exp_up[l], w_exp_down[l])
    yp, ysm = [_combine(g["base"], g["wts"].T, g["gate2"], ys, g["dest"], g["seq"], g["tm"]).reshape(g["shape"])
               for g in groups]

    return (yp, ysm,
            gp["k"].reshape(1, bp, lp, N_HEADS, 2, HEAD_DIM_A), gp["v"].reshape(1, bp, lp, N_HEADS, HEAD_W),
            gp["s"][None],
            gs["k"].reshape(1, bs, ls, N_HEADS, 2, HEAD_DIM_A), gs["v"].reshape(1, bs, ls, N_HEADS, HEAD_W),
            gs["s"][None])
```

```python
import functools
import math

import jax
import jax.numpy as jnp
from jax import lax
from jax.experimental import pallas as pl
from jax.experimental.pallas import tpu as pltpu

F32 = jnp.float32
BF16 = jnp.bfloat16
U32 = jnp.uint32
I32 = jnp.int32

D_MODEL = 1024
N_HEADS = 8
HEAD_W = 128
HEAD_DIM_A = 64
CHUNK = 64
N_EXPERTS = 256
TOP_K = 8
N_GROUPS = 8
GROUP_W = N_EXPERTS // N_GROUPS
TOPK_GROUPS = 4
D_EXPERT = 256
ROUTED_SCALE = 2.5
EPS = 1e-6
LAMBDA_INIT = 0.8 - 0.6 * math.exp(-0.3 * 0)
LANES = 128
NEG = -1e30
LOG2E = math.log2(math.e)

ATTN_BOUND_MAX_GAP = 100.0
GLA_CHUNK = 32
ROW_BLOCK = 512
VMEM_LIMIT = 48 * 1024 * 1024


def _sigmoid(x):
    return 0.5 * jnp.tanh(0.5 * x) + 0.5


def _silu(x):
    return x * _sigmoid(x)


def _cparams(sem):
    return pltpu.CompilerParams(dimension_semantics=sem, vmem_limit_bytes=VMEM_LIMIT)


ROW_PARTS = 4


def _store_packed(ref, x, row0=0):
    m, half = x.shape[0], x.shape[1] // 2
    lo = pltpu.bitcast(x[:, :half].astype(BF16).astype(F32), U32)
    hi = pltpu.bitcast(x[:, half:].astype(BF16).astype(F32), U32)
    w = (hi & jnp.uint32(0xFFFF0000)) | (lo >> 16)
    for j in range(ROW_PARTS):
        ref[pl.ds(ROW_PARTS * row0 + j, m, stride=ROW_PARTS), :] = w[:, j * LANES:(j + 1) * LANES]


def _load_packed(ref, m, row0=0):
    parts = [ref[pl.ds(ROW_PARTS * row0 + j, m, stride=ROW_PARTS), :] for j in range(ROW_PARTS)]
    lo = [pltpu.bitcast(w << 16, F32) for w in parts]
    hi = [pltpu.bitcast(w & jnp.uint32(0xFFFF0000), F32) for w in parts]
    return jnp.concatenate(lo + hi, axis=1)


def _mod_spec(a, seq, tm):
    if a.ndim == 3:
        per_b = seq // tm
        return pl.BlockSpec((1, 1, D_MODEL), lambda i: (i // per_b, 0, 0))
    return pl.BlockSpec((tm, D_MODEL), lambda i: (i, 0))


def _mod_val(ref):
    v = ref[...]
    return v.reshape(v.shape[-2], v.shape[-1])


def _ada_kernel(c_ref, w_ref, b_ref, o_ref):
    c = _silu(c_ref[...])
    o_ref[...] = jnp.dot(c, w_ref[...], precision=lax.Precision.HIGHEST,
                         preferred_element_type=F32) + b_ref[...]


def _ada(c, w_ada, b_ada):
    n = c.shape[0]
    nj = w_ada.shape[1] // D_MODEL
    return pl.pallas_call(
        _ada_kernel,
        grid=(nj,),
        in_specs=[pl.BlockSpec((n, D_MODEL), lambda j: (0, 0)),
                  pl.BlockSpec((D_MODEL, D_MODEL), lambda j: (0, j)),
                  pl.BlockSpec((1, D_MODEL), lambda j: (0, j))],
        out_specs=pl.BlockSpec((n, D_MODEL), lambda j: (0, j)),
        out_shape=jax.ShapeDtypeStruct((n, w_ada.shape[1]), F32),
        compiler_params=_cparams(("arbitrary",)),
    )(c, w_ada, b_ada.reshape(1, -1))


def _norm_mod_kernel(x_ref, w_ref, sc_ref, sh_ref, o_ref):
    x = x_ref[...]
    y = x * lax.rsqrt(jnp.mean(x * x, axis=-1, keepdims=True) + EPS) * w_ref[...]
    o_ref[...] = (y * (1.0 + _mod_val(sc_ref)) + _mod_val(sh_ref)).astype(o_ref.dtype)


def _norm_mod(x2, w, scale, shift, seq, tm):
    t = x2.shape[0]
    return pl.pallas_call(
        _norm_mod_kernel,
        grid=(t // tm,),
        in_specs=[pl.BlockSpec((tm, D_MODEL), lambda i: (i, 0)),
                  pl.BlockSpec((1, D_MODEL), lambda i: (0, 0)),
                  _mod_spec(scale, seq, tm), _mod_spec(shift, seq, tm)],
        out_specs=pl.BlockSpec((tm, D_MODEL), lambda i: (i, 0)),
        out_shape=jax.ShapeDtypeStruct((t, D_MODEL), BF16),
        compiler_params=_cparams(("parallel",)),
    )(x2, w.reshape(1, -1), scale, shift)


def _head_norm(acc, w):
    r = lax.broadcasted_iota(I32, (HEAD_W, HEAD_W), 0) // HEAD_DIM_A
    c = lax.broadcasted_iota(I32, (HEAD_W, HEAD_W), 1) // HEAD_DIM_A
    ones = (r == c).astype(BF16)
    outs = []
    for h in range(N_HEADS):
        a = acc[:, h * HEAD_W:(h + 1) * HEAD_W]
        sq = a * a
        hi = sq.astype(BF16)
        lo = (sq - hi.astype(F32)).astype(BF16)
        ss = (jnp.dot(hi, ones, preferred_element_type=F32)
              + jnp.dot(lo, ones, preferred_element_type=F32))
        outs.append(a * lax.rsqrt(ss * (1.0 / HEAD_DIM_A) + EPS) * w[:, h * HEAD_W:(h + 1) * HEAD_W])
    return jnp.concatenate(outs, axis=1)


N_MAPS = D_MODEL // HEAD_DIM_A


def _proj_kernel(h_ref, w_ref, nw_ref, *o_refs, head_norm, out_scale, split_first):
    acc = jnp.dot(h_ref[...], w_ref[...], preferred_element_type=F32)
    if head_norm:
        acc = _head_norm(acc, nw_ref[...])
    for n, (o_ref, s) in enumerate(zip(o_refs, out_scale)):
        val = (acc if s == 1.0 else acc * s).astype(o_ref.dtype)
        if split_first and n == 0:
            for g in range(N_MAPS):
                o_ref[pl.ds(g, val.shape[0], stride=N_MAPS), :] = val[:, g * HEAD_DIM_A:(g + 1) * HEAD_DIM_A]
        else:
            o_ref[...] = val


def _proj(h, w_in_bf, group, out_dtypes, norm_w=None, out_scale=None, split_first=False, tm=1024):
    t = h.shape[0]
    tm = min(tm, t)
    nw = jnp.ones((1, D_MODEL), F32) if norm_w is None else norm_w
    out_scale = tuple(out_scale or (1.0,) * len(out_dtypes))
    specs = [pl.BlockSpec((tm, D_MODEL), lambda i: (i, 0)) for _ in out_dtypes]
    shapes = [jax.ShapeDtypeStruct((t, D_MODEL), dt) for dt in out_dtypes]
    if split_first:
        specs[0] = pl.BlockSpec((N_MAPS * tm, HEAD_DIM_A), lambda i: (i, 0))
        shapes[0] = jax.ShapeDtypeStruct((N_MAPS * t, HEAD_DIM_A), out_dtypes[0])
    outs = pl.pallas_call(
        functools.partial(_proj_kernel, head_norm=norm_w is not None, out_scale=out_scale,
                          split_first=split_first),
        grid=(t // tm,),
        in_specs=[pl.BlockSpec((tm, D_MODEL), lambda i: (i, 0)),
                  pl.BlockSpec((D_MODEL, D_MODEL), lambda i: (0, group)),
                  pl.BlockSpec((1, D_MODEL), lambda i: (0, 0))],
        out_specs=specs,
        out_shape=shapes,
        compiler_params=_cparams(("parallel",)),
    )(h, w_in_bf, nw)
    return outs


def _split_maps(q):
    lane = lax.broadcasted_iota(I32, q.shape, 1)
    zero = jnp.zeros_like(q)
    return jnp.where(lane < HEAD_DIM_A, q, zero), jnp.where(lane >= HEAD_DIM_A, q, zero)


def _finish_head(acc1, l1, acc2, l2, lam, sw):
    o = acc1 * (1.0 / l1) - acc2 * (lam / l2)
    o = o * lax.rsqrt(jnp.mean(o * o, axis=-1, keepdims=True) + EPS) * sw
    return o * (1.0 - LAMBDA_INIT)


def _attn_kernel(q_ref, k_ref, v_ref, lam_ref, sw_ref, o_ref, mx_s, mrep_s, ls_s, acc_s,
                 *, tq, tk, hp):
    qi = pl.program_id(2)
    nf = tk // LANES
    qs = []
    for hh in range(hp):
        qs.extend(_split_maps(q_ref[:, hh * HEAD_W:(hh + 1) * HEAD_W]))
    ns = 2 * hp

    r_chunk = lax.broadcasted_iota(I32, (tq, tk), 0) // CHUNK
    c_chunk = lax.broadcasted_iota(I32, (tq, tk), 1) // CHUNK
    diag_ok = c_chunk <= r_chunk

    def scores(kt, s, masked):
        k = k_ref[pl.ds(pl.multiple_of(kt * tk, tk), tk), (s // 2) * HEAD_W:(s // 2 + 1) * HEAD_W]
        sc = lax.dot_general(qs[s], k, (((1,), (1,)), ((), ())), preferred_element_type=F32)
        return jnp.where(diag_ok, sc, NEG) if masked else sc

    def fold(x, op):
        r = x[:, :LANES]
        for j in range(1, nf):
            r = op(r, x[:, j * LANES:(j + 1) * LANES])
        return r

    def pass1(kt, first):
        for s in range(ns):
            m = fold(scores(kt, s, first), jnp.maximum)
            mx_s[s] = m if first else jnp.maximum(mx_s[s], m)

    def pass2(kt, first):
        for s in range(ns):
            m = mrep_s[s]
            p = jnp.exp2(scores(kt, s, first) - jnp.concatenate([m] * nf, axis=1))
            v = v_ref[pl.ds(pl.multiple_of(kt * tk, tk), tk), (s // 2) * HEAD_W:(s // 2 + 1) * HEAD_W]
            pv = jnp.dot(p.astype(BF16), v, preferred_element_type=F32)
            ls_s[s] = fold(p, jnp.add) if first else ls_s[s] + fold(p, jnp.add)
            acc_s[s] = pv if first else acc_s[s] + pv

    def loop(fn):
        def body(kt, c):
            fn(kt, False)
            return c

        fn(qi, True)
        lax.fori_loop(0, qi, body, 0)

    use_bound = lam_ref[0, 2] > 0.5

    @pl.when(use_bound)
    def _():
        mrep_s[...] = jnp.full(mrep_s.shape, lam_ref[0, 1], F32)

    @pl.when(jnp.logical_not(use_bound))
    def _():
        loop(pass1)
        for s in range(ns):
            mrep_s[s] = jnp.broadcast_to(jnp.max(mx_s[s], axis=-1, keepdims=True), (tq, LANES))

    loop(pass2)

    lam = lam_ref[0, 0]
    for hh in range(hp):
        l1 = jnp.sum(ls_s[2 * hh], axis=-1, keepdims=True)
        l2 = jnp.sum(ls_s[2 * hh + 1], axis=-1, keepdims=True)
        o = _finish_head(acc_s[2 * hh], l1, acc_s[2 * hh + 1], l2, lam, sw_ref[...])
        o_ref[:, hh * HEAD_W:(hh + 1) * HEAD_W] = o.astype(o_ref.dtype)


def _attention_prompt(q, kb, vb, lam, subln_w, batch, seq):
    tq = tk = min(seq, 256)
    hp = 8
    nq = seq // tq
    kern = functools.partial(_attn_kernel, tq=tq, tk=tk, hp=hp)
    w = hp * HEAD_W
    return pl.pallas_call(
        kern,
        grid=(batch, N_HEADS // hp, nq),
        in_specs=[pl.BlockSpec((tq, w), lambda b, h, i: (b * nq + i, h)),
                  pl.BlockSpec((seq, w), lambda b, h, i: (b, h)),
                  pl.BlockSpec((seq, w), lambda b, h, i: (b, h)),
                  pl.BlockSpec((1, 4), lambda b, h, i: (0, 0), memory_space=pltpu.SMEM),
                  pl.BlockSpec((1, HEAD_W), lambda b, h, i: (0, 0))],
        out_specs=pl.BlockSpec((tq, w), lambda b, h, i: (b * nq + i, h)),
        out_shape=jax.ShapeDtypeStruct((batch * seq, D_MODEL), BF16),
        scratch_shapes=[pltpu.VMEM((2 * hp, tq, LANES), F32) for _ in range(4)],
        compiler_params=_cparams(("parallel", "parallel", "arbitrary")),
    )(q, kb, vb, lam, subln_w.reshape(1, -1))


def _attn_cached_kernel(q_ref, k_ref, v_ref, lam_ref, sw_ref, o_ref, *, lq, lk):
    qpos = (lk - lq) + lax.broadcasted_iota(I32, (lq, lk), 0)
    kpos = lax.broadcasted_iota(I32, (lq, lk), 1)
    allowed = (kpos // CHUNK) <= (qpos // CHUNK)
    lam = lam_ref[0, 0]
    for h in range(N_HEADS):
        sl = slice(h * HEAD_W, (h + 1) * HEAD_W)
        k = k_ref[:, sl]
        v = v_ref[:, sl]
        accs, ls = [], []
        for qm in _split_maps(q_ref[:, sl]):
            sc = lax.dot_general(qm, k, (((1,), (1,)), ((), ())), preferred_element_type=F32)
            sc = jnp.where(allowed, sc, NEG)
            p = jnp.exp2(sc - jnp.max(sc, axis=-1, keepdims=True))
            ls.append(jnp.sum(p, axis=-1, keepdims=True))
            accs.append(jnp.dot(p.astype(BF16), v, preferred_element_type=F32))
        o = _finish_head(accs[0], ls[0], accs[1], ls[1], lam, sw_ref[...])
        o_ref[:, sl] = o.astype(o_ref.dtype)


def _attention_cached(q, kb, vb, lam, subln_w, batch, lq, lk):
    return pl.pallas_call(
        functools.partial(_attn_cached_kernel, lq=lq, lk=lk),
        grid=(batch,),
        in_specs=[pl.BlockSpec((lq, D_MODEL), lambda b: (b, 0)),
                  pl.BlockSpec((lk, D_MODEL), lambda b: (b, 0)),
                  pl.BlockSpec((lk, D_MODEL), lambda b: (b, 0)),
                  pl.BlockSpec((1, 4), lambda b: (0, 0), memory_space=pltpu.SMEM),
                  pl.BlockSpec((1, HEAD_W), lambda b: (0, 0))],
        out_specs=pl.BlockSpec((lq, D_MODEL), lambda b: (b, 0)),
        out_shape=jax.ShapeDtypeStruct((batch * lq, D_MODEL), BF16),
        compiler_params=_cparams(("parallel",)),
    )(q, kb, vb, lam, subln_w.reshape(1, -1))


def _hgrn_kernel(qh_ref, fh_ref, ih_ref, gh_ref, lb_ref, nw_ref, s0_ref, o_ref, s_out_ref,
                 st_s, qin_s, qmid_s, kmid_s, kend_s, dec_s, *, ct, c):
    t = pl.program_id(1)
    nt = pl.num_programs(1)
    nc = ct // c

    @pl.when(t == 0)
    def _():
        for h in range(N_HEADS):
            st_s[h] = s0_ref[0, h].T

    a = lb_ref[...]
    amax = jnp.max(a, axis=0, keepdims=True)
    e = jnp.exp(a - amax)
    lb = e[0:1] / jnp.sum(e, axis=0, keepdims=True)

    row = lax.broadcasted_iota(I32, (c, c), 0)
    col = lax.broadcasted_iota(I32, (c, c), 1)
    causal = col <= row
    t_idx = lax.broadcasted_iota(I32, (c, D_MODEL), 0)

    def cumsum_rows(g):
        sh = 1
        while sh < c:
            g = g + jnp.where(t_idx >= sh, pltpu.roll(g, sh, axis=0), 0.0)
            sh *= 2
        return g

    for ci in range(nc):
        rows = slice(ci * c, (ci + 1) * c)
        f = lb + (1.0 - lb) * _sigmoid(fh_ref[rows, :])
        b = cumsum_rows(jnp.log2(f))
        b_last = b[c - 1:c, :]
        b_mid = b[c // 2 - 1:c // 2, :]
        q = _silu(qh_ref[rows, :].astype(F32)) * (HEAD_W ** -0.5)
        kk = 1.0 - f
        e_dn = jnp.exp2(b - b_mid)
        e_up = jnp.exp2(b_mid - b)
        qm = q * e_dn
        km = kk * e_up
        qmid_s[rows, :] = qm.astype(BF16)
        kmid_s[rows, :] = km.astype(BF16)
        qin_s[rows, :] = (qm * jnp.exp2(b_mid)).astype(BF16)
        kend_s[rows, :] = (km * jnp.exp2(b_last - b_mid)).astype(BF16)
        dec_s[ci:ci + 1, :] = jnp.exp2(b_last)

    nw = nw_ref[...]
    st = [st_s[h] for h in range(N_HEADS)]
    for ci in range(nc):
        rows = slice(ci * c, (ci + 1) * c)
        gate = _silu(gh_ref[rows, :].astype(F32))
        decay = dec_s[ci:ci + 1, :]
        for h in range(N_HEADS):
            sl = slice(h * HEAD_W, (h + 1) * HEAD_W)
            v = ih_ref[rows, sl]
            inter = lax.dot_general(qin_s[rows, sl], st[h].astype(BF16), (((1,), (1,)), ((), ())),
                                    preferred_element_type=F32)
            att = lax.dot_general(qmid_s[rows, sl], kmid_s[rows, sl], (((1,), (1,)), ((), ())),
                                  preferred_element_type=F32)
            att = jnp.where(causal, att, 0.0)
            o = inter + jnp.dot(att.astype(BF16), v, preferred_element_type=F32)
            upd = lax.dot_general(v, kend_s[rows, sl], (((0,), (0,)), ((), ())),
                                  preferred_element_type=F32)
            st[h] = decay[:, sl] * st[h] + upd
            o = o * lax.rsqrt(jnp.mean(o * o, axis=-1, keepdims=True) + EPS) * nw[:, sl]
            o_ref[rows, sl] = (o * gate[:, sl]).astype(o_ref.dtype)
    for h in range(N_HEADS):
        st_s[h] = st[h]

    @pl.when(t == nt - 1)
    def _():
        for h in range(N_HEADS):
            s_out_ref[0, h] = st_s[h].T


def _hgrn(qh, fh, ih, gh, hgrn_lb, norm_w, s0, batch, seq):
    ct = min(seq, 512)
    c = min(GLA_CHUNK, ct)
    nt = seq // ct
    kern = functools.partial(_hgrn_kernel, ct=ct, c=c)
    tok = pl.BlockSpec((ct, D_MODEL), lambda b, t: (b * nt + t, 0))
    st_spec = pl.BlockSpec((1, N_HEADS, HEAD_W, HEAD_W), lambda b, t: (b, 0, 0, 0))
    return pl.pallas_call(
        kern,
        grid=(batch, nt),
        in_specs=[tok, tok, tok, tok,
                  pl.BlockSpec((2, D_MODEL), lambda b, t: (0, 0)),
                  pl.BlockSpec((1, D_MODEL), lambda b, t: (0, 0)),
                  st_spec],
        out_specs=[tok, st_spec],
        out_shape=[jax.ShapeDtypeStruct((batch * seq, D_MODEL), BF16),
                   jax.ShapeDtypeStruct((batch, N_HEADS, HEAD_W, HEAD_W), F32)],
        scratch_shapes=[pltpu.VMEM((N_HEADS, HEAD_W, HEAD_W), F32)]
        + [pltpu.VMEM((ct, D_MODEL), BF16) for _ in range(4)]
        + [pltpu.VMEM((max(ct // c, 8), D_MODEL), F32)],
        compiler_params=_cparams(("parallel", "arbitrary")),
    )(qh, fh, ih, gh, hgrn_lb, jnp.tile(norm_w.reshape(1, -1), (1, N_HEADS)), s0)


def _route(h2, wrt_ref, brt_ref, carry_s, tm):
    nt_dims = (((1,), (1,)), ((), ()))
    w_hi = wrt_ref[0]
    w_lo = wrt_ref[1]
    h_hi = h2.astype(BF16)
    h_lo = (h2 - h_hi.astype(F32)).astype(BF16)
    logits = (lax.dot_general(w_hi, h_hi, nt_dims, preferred_element_type=F32)
              + lax.dot_general(w_hi, h_lo, nt_dims, preferred_element_type=F32)
              + lax.dot_general(w_lo, h_hi, nt_dims, preferred_element_type=F32))
    scores = _sigmoid(logits)
    biased = scores + brt_ref[...]
    big = float(2 * N_EXPERTS)

    x3 = biased.reshape(N_GROUPS, GROUP_W, tm)
    i3 = lax.broadcasted_iota(I32, x3.shape, 1).astype(F32)
    m1 = jnp.max(x3, axis=1, keepdims=True)
    i1 = jnp.min(jnp.where(x3 == m1, i3, big), axis=1, keepdims=True)
    m2 = jnp.max(jnp.where(i3 == i1, NEG, x3), axis=1, keepdims=True)
    gs = (m1 + m2).reshape(N_GROUPS, tm)

    g_iota = lax.broadcasted_iota(I32, (N_GROUPS, tm), 0).astype(F32)
    rem = gs
    gsel = jnp.zeros((N_GROUPS, tm), F32)
    for _ in range(TOPK_GROUPS):
        gm = jnp.max(rem, axis=0, keepdims=True)
        first = jnp.min(jnp.where(rem == gm, g_iota, big), axis=0, keepdims=True)
        sel = g_iota == first
        gsel = jnp.where(sel, 1.0, gsel)
        rem = jnp.where(sel, NEG, rem)
    keep = jnp.broadcast_to(gsel.reshape(N_GROUPS, 1, tm), x3.shape) > 0.5
    masked = jnp.where(keep, x3, NEG).reshape(N_EXPERTS, tm)

    e_iota = lax.broadcasted_iota(I32, (N_EXPERTS, tm), 0).astype(F32)
    onehot = jnp.zeros((N_EXPERTS, tm), F32)
    idxs, ws = [], []
    for _ in range(TOP_K):
        mk = jnp.max(masked, axis=0, keepdims=True)
        ik = jnp.min(jnp.where(masked == mk, e_iota, big), axis=0, keepdims=True)
        selk = e_iota == ik
        ws.append(jnp.sum(jnp.where(selk, scores, 0.0), axis=0, keepdims=True))
        idxs.append(ik)
        masked = jnp.where(selk, NEG, masked)
        onehot = jnp.where(selk, 1.0, onehot)
    wsum = ws[0]
    for k in range(1, TOP_K):
        wsum = wsum + ws[k]

    r = lax.broadcasted_iota(I32, (tm, tm), 0)
    cidx = lax.broadcasted_iota(I32, (tm, tm), 1)
    before = (r < cidx).astype(BF16)
    cum = jnp.dot(onehot.astype(BF16), before, preferred_element_type=F32) + carry_s[...]
    carry_s[...] = carry_s[...] + jnp.sum(onehot, axis=1, keepdims=True)

    k_iota = lax.broadcasted_iota(I32, (TOP_K, tm), 0)
    e_out = jnp.zeros((TOP_K, tm), F32)
    w_out = jnp.zeros((TOP_K, tm), F32)
    p_out = jnp.zeros((TOP_K, tm), F32)
    for k in range(TOP_K):
        pk = jnp.sum(jnp.where(e_iota == idxs[k], cum, 0.0), axis=0, keepdims=True)
        e_out = jnp.where(k_iota == k, idxs[k], e_out)
        w_out = jnp.where(k_iota == k, ws[k] / wsum * ROUTED_SCALE, w_out)
        p_out = jnp.where(k_iota == k, pk, p_out)
    return e_out, w_out, p_out


def _post_kernel(oa_ref, ob_ref, ga_ref, gb_ref, x_ref, g1_ref, sc2_ref, sh2_ref, g2_ref,
                 wpa_ref, wpb_ref, wo_ref, n2_ref, wrt_ref, brt_ref, wsg_ref, wsu_ref, wsd_ref, c0_ref,
                 base_ref, hp_ref, eidx_ref, wts_ref, pos_ref, cnt_ref, carry_s, *, tm):
    i = pl.program_id(0)

    @pl.when(i == 0)
    def _():
        carry_s[...] = c0_ref[...]

    pa = jnp.dot(oa_ref[...], wpa_ref[...], preferred_element_type=F32)
    pb = jnp.dot(ob_ref[...], wpb_ref[...], preferred_element_type=F32)
    u = _sigmoid(ga_ref[...].astype(F32)) * pa + _sigmoid(gb_ref[...].astype(F32)) * pb
    x1 = x_ref[...] + _mod_val(g1_ref) * jnp.dot(u.astype(BF16), wo_ref[...], preferred_element_type=F32)
    y = x1 * lax.rsqrt(jnp.mean(x1 * x1, axis=-1, keepdims=True) + EPS) * n2_ref[...]
    h2 = y * (1.0 + _mod_val(sc2_ref)) + _mod_val(sh2_ref)
    _store_packed(hp_ref, h2)

    hb = h2.astype(BF16)
    sg = jnp.dot(hb, wsg_ref[...], preferred_element_type=F32)
    su = jnp.dot(hb, wsu_ref[...], preferred_element_type=F32)
    shared = jnp.dot((_silu(sg) * su).astype(BF16), wsd_ref[...], preferred_element_type=F32)
    base_ref[...] = x1 + _mod_val(g2_ref) * shared

    e_out, w_out, p_out = _route(h2, wrt_ref, brt_ref, carry_s, tm)
    eidx_ref[...] = e_out.astype(I32)
    wts_ref[...] = w_out
    pos_ref[...] = p_out.astype(I32)
    cnt_ref[...] = carry_s[...]


def _post(oa, ob, ga, gb, x2, g1, sc2, sh2, g2, wpa, wpb, wo, n2, wr, br, wsg, wsu, wsd, count0, seq, tm):
    t = x2.shape[0]
    tok = lambda: pl.BlockSpec((tm, D_MODEL), lambda i: (i, 0))
    mod = lambda: _mod_spec(g1, seq, tm)
    full = lambda a: pl.BlockSpec(a.shape, lambda i: (0,) * a.ndim)
    n2 = n2.reshape(1, -1)
    wrt_hi = wr.T.astype(BF16)
    wrt = jnp.stack([wrt_hi, (wr.T - wrt_hi.astype(F32)).astype(BF16)])
    brt = br.reshape(-1, 1)
    k_out = lambda: pl.BlockSpec((TOP_K, tm), lambda i: (0, i))
    return pl.pallas_call(
        functools.partial(_post_kernel, tm=tm),
        grid=(t // tm,),
        in_specs=[tok(), tok(), tok(), tok(), tok(), mod(), mod(), mod(), mod(),
                  full(wpa), full(wpb), full(wo), full(n2), full(wrt), full(brt),
                  full(wsg), full(wsu), full(wsd), full(count0)],
        out_specs=[tok(), pl.BlockSpec((ROW_PARTS * tm, LANES), lambda i: (i, 0)),
                   k_out(), k_out(), k_out(),
                   pl.BlockSpec((N_EXPERTS, 1), lambda i: (0, 0))],
        out_shape=[jax.ShapeDtypeStruct((t, D_MODEL), F32),
                   jax.ShapeDtypeStruct((ROW_PARTS * t, LANES), U32),
                   jax.ShapeDtypeStruct((TOP_K, t), I32),
                   jax.ShapeDtypeStruct((TOP_K, t), F32),
                   jax.ShapeDtypeStruct((TOP_K, t), I32),
                   jax.ShapeDtypeStruct((N_EXPERTS, 1), F32)],
        scratch_shapes=[pltpu.VMEM((N_EXPERTS, 1), F32)],
        compiler_params=_cparams(("arbitrary",)),
    )(oa, ob, ga, gb, x2, g1, sc2, sh2, g2, wpa, wpb, wo, n2, wrt, brt, wsg, wsu, wsd, count0)


def _dest_kernel(eidx_ref, pos_ref, pstart_ref, o_ref, *, tm):
    e_iota = lax.broadcasted_iota(I32, (N_EXPERTS, tm), 0)
    k_iota = lax.broadcasted_iota(I32, (TOP_K, tm), 0)
    eidx = eidx_ref[...]
    start = jnp.zeros((TOP_K, tm), F32)
    for k in range(TOP_K):
        sel = e_iota == eidx[k:k + 1, :]
        sk = jnp.sum(jnp.where(sel, pstart_ref[...], 0.0), axis=0, keepdims=True)
        start = jnp.where(k_iota == k, sk, start)
    o_ref[...] = (start.astype(I32) + pos_ref[...]) * ROW_PARTS


def _dest(eidx, pos, pstart_col):
    t = eidx.shape[1]
    tm = min(t, 1024)
    blk = lambda: pl.BlockSpec((TOP_K, tm), lambda i: (0, i))
    return pl.pallas_call(
        functools.partial(_dest_kernel, tm=tm),
        grid=(t // tm,),
        in_specs=[blk(), blk(), pl.BlockSpec((N_EXPERTS, 1), lambda i: (0, 0))],
        out_specs=blk(),
        out_shape=jax.ShapeDtypeStruct((TOP_K, t), I32),
        compiler_params=_cparams(("parallel",)),
    )(eidx, pos, pstart_col)


def _zero_tail_kernel(last_ref, o_ref):
    del last_ref
    o_ref[...] = jnp.zeros(o_ref.shape, o_ref.dtype)


def _zero_tails(last_blk, nb):
    grid_spec = pltpu.PrefetchScalarGridSpec(
        num_scalar_prefetch=1,
        grid=(N_EXPERTS,),
        in_specs=[],
        out_specs=pl.BlockSpec((ROW_PARTS * ROW_BLOCK, LANES), lambda e, last: (last[e], 0)),
    )
    return pl.pallas_call(
        _zero_tail_kernel,
        grid_spec=grid_spec,
        out_shape=jax.ShapeDtypeStruct((ROW_PARTS * nb * ROW_BLOCK, LANES), U32),
        compiler_params=_cparams(("arbitrary",)),
    )(last_blk)


def _dispatch_kernel(dest_ref, hp_ref, xs_in_ref, xs_ref, sem, *, tm):
    del xs_in_ref

    def row_copy(t, d):
        return pltpu.make_async_copy(
            hp_ref.at[pl.ds(pl.multiple_of(t * ROW_PARTS, ROW_PARTS), ROW_PARTS), :],
            xs_ref.at[pl.ds(pl.multiple_of(d, ROW_PARTS), ROW_PARTS), :], sem)

    def issue(t, c):
        for k in range(TOP_K):
            row_copy(t, dest_ref[k, t]).start(priority=k % 2)
        return c

    def drain(t, c):
        for k in range(TOP_K):
            row_copy(0, 0).wait()
        return c

    lax.fori_loop(0, tm, issue, 0)
    lax.fori_loop(0, tm, drain, 0)


def _dispatch(hp, dest, xs, tm):
    t = dest.shape[1]
    return pl.pallas_call(
        functools.partial(_dispatch_kernel, tm=tm),
        grid=(t // tm,),
        in_specs=[pl.BlockSpec((TOP_K, tm), lambda i: (0, i), memory_space=pltpu.SMEM),
                  pl.BlockSpec((ROW_PARTS * tm, LANES), lambda i: (i, 0)),
                  pl.BlockSpec(memory_space=pl.ANY)],
        out_specs=pl.BlockSpec(memory_space=pl.ANY),
        out_shape=jax.ShapeDtypeStruct(xs.shape, xs.dtype),
        scratch_shapes=[pltpu.SemaphoreType.DMA],
        input_output_aliases={2: 0},
        compiler_params=_cparams(("arbitrary",)),
    )(dest, hp, xs)


def _expert_kernel(be_ref, nu_ref, xs_ref, wg_ref, wu_ref, wd_ref, ys_ref, wg_s, wu_s, wd_s):
    i = pl.program_id(0)
    used = i < nu_ref[0]

    @pl.when(jnp.logical_and(used, jnp.logical_or(i == 0, be_ref[i] != be_ref[jnp.maximum(i - 1, 0)])))
    def _():
        wg_s[...] = wg_ref[0].astype(BF16)
        wu_s[...] = wu_ref[0].astype(BF16)
        wd_s[...] = wd_ref[0].astype(BF16)

    @pl.when(used)
    def _():
        x = _load_packed(xs_ref, ROW_BLOCK).astype(BF16)
        g = jnp.dot(x, wg_s[...], preferred_element_type=F32)
        u = jnp.dot(x, wu_s[...], preferred_element_type=F32)
        hmid = (_silu(g) * u).astype(BF16)
        _store_packed(ys_ref, jnp.dot(hmid, wd_s[...], preferred_element_type=F32))

    @pl.when(jnp.logical_not(used))
    def _():
        ys_ref[...] = jnp.zeros(ys_ref.shape, ys_ref.dtype)


def _experts(xs, blk_e, n_used, w_eg, w_eu, w_ed):
    nb = xs.shape[0] // (ROW_PARTS * ROW_BLOCK)
    rows_spec = lambda: pl.BlockSpec((ROW_PARTS * ROW_BLOCK, LANES), lambda i, be, nu: (i, 0))
    used_rows_spec = pl.BlockSpec((ROW_PARTS * ROW_BLOCK, LANES),
                                  lambda i, be, nu: (jnp.maximum(jnp.minimum(i, nu[0] - 1), 0), 0))
    w_up = lambda: pl.BlockSpec((1, D_MODEL, D_EXPERT), lambda i, be, nu: (be[i], 0, 0))
    grid_spec = pltpu.PrefetchScalarGridSpec(
        num_scalar_prefetch=2,
        grid=(nb,),
        in_specs=[used_rows_spec, w_up(), w_up(),
                  pl.BlockSpec((1, D_EXPERT, D_MODEL), lambda i, be, nu: (be[i], 0, 0))],
        out_specs=rows_spec(),
        scratch_shapes=[pltpu.VMEM((D_MODEL, D_EXPERT), BF16), pltpu.VMEM((D_MODEL, D_EXPERT), BF16),
                        pltpu.VMEM((D_EXPERT, D_MODEL), BF16)],
    )
    return pl.pallas_call(
        _expert_kernel,
        grid_spec=grid_spec,
        out_shape=jax.ShapeDtypeStruct(xs.shape, U32),
        compiler_params=_cparams(("arbitrary",)),
    )(blk_e, n_used, xs, w_eg, w_eu, w_ed)


def _combine_kernel(dest_ref, dnext_ref, base_ref, wts_ref, g2_ref, ys_ref, o_ref, buf, sem, *, tm):
    i = pl.program_id(0)
    n = pl.num_programs(0)
    slot = lax.rem(i, 2)

    def row_copy(d, sl, k, t):
        return pltpu.make_async_copy(
            ys_ref.at[pl.ds(pl.multiple_of(d, ROW_PARTS), ROW_PARTS), :],
            buf.at[sl, k, pl.ds(pl.multiple_of(t * ROW_PARTS, ROW_PARTS), ROW_PARTS), :], sem.at[sl])

    def request(d_ref, sl):
        def issue(t, c):
            for k in range(TOP_K):
                row_copy(d_ref[k, t], sl, k, t).start(priority=k % 2)
            return c

        lax.fori_loop(0, tm, issue, 0)

    @pl.when(i == 0)
    def _():
        request(dest_ref, 0)

    @pl.when(i + 1 < n)
    def _():
        request(dnext_ref, 1 - slot)

    def drain(t, c):
        for k in range(TOP_K):
            row_copy(0, slot, 0, 0).wait()
        return c

    lax.fori_loop(0, tm, drain, 0)

    w = wts_ref[...]
    y = jnp.zeros((tm, D_MODEL), F32)
    for k in range(TOP_K):
        y = y + w[:, k:k + 1] * _load_packed(buf.at[slot, k], tm)
    o_ref[...] = base_ref[...] + _mod_val(g2_ref) * y


def _combine(base, wts_t, g2, ys, dest, seq, tm):
    t = base.shape[0]
    n = t // tm
    return pl.pallas_call(
        functools.partial(_combine_kernel, tm=tm),
        grid=(n,),
        in_specs=[pl.BlockSpec((TOP_K, tm), lambda i: (0, i), memory_space=pltpu.SMEM),
                  pl.BlockSpec((TOP_K, tm), lambda i: (0, jnp.minimum(i + 1, n - 1)), memory_space=pltpu.SMEM),
                  pl.BlockSpec((tm, D_MODEL), lambda i: (i, 0)),
                  pl.BlockSpec((tm, TOP_K), lambda i: (i, 0)),
                  _mod_spec(g2, seq, tm),
                  pl.BlockSpec(memory_space=pl.ANY)],
        out_specs=pl.BlockSpec((tm, D_MODEL), lambda i: (i, 0)),
        out_shape=jax.ShapeDtypeStruct((t, D_MODEL), F32),
        scratch_shapes=[pltpu.VMEM((2, TOP_K, ROW_PARTS * tm, LANES), U32), pltpu.SemaphoreType.DMA((2,))],
        compiler_params=_cparams(("arbitrary",)),
    )(dest, dest, base, wts_t, g2, ys)


def _mix(x, mods, past_k, past_v, s0, count0, p):
    batch, seq, _ = x.shape
    t = batch * seq
    x2 = x.reshape(t, D_MODEL)
    if seq % 256 == 0:
        tile = lambda cap: min(seq, cap)
    else:
        mods = [jnp.broadcast_to(m, (batch, seq, D_MODEL)).reshape(t, D_MODEL) for m in mods]
        tile = lambda cap: min(t, cap)
    shift1, scale1, gate1, shift2, scale2, gate2 = mods

    h = _norm_mod(x2, p["norm1_w"], scale1, shift1, seq, tile(512))
    qn = jnp.tile(p["q_norm_w"].reshape(1, -1), (1, D_MODEL // HEAD_DIM_A))
    kn = jnp.tile(p["k_norm_w"].reshape(1, -1), (1, D_MODEL // HEAD_DIM_A))
    w_in = p["w_in"]
    (q,) = _proj(h, w_in, 0, [BF16], qn, out_scale=(HEAD_DIM_A ** -0.5 * LOG2E,))
    k, kb = _proj(h, w_in, 1, [F32, BF16], kn, split_first=True, tm=512)
    v, vb = _proj(h, w_in, 2, [F32, BF16])
    (qh,) = _proj(h, w_in, 3, [BF16])
    (fh,) = _proj(h, w_in, 4, [F32])
    (ih,) = _proj(h, w_in, 5, [BF16])
    (gh,) = _proj(h, w_in, 6, [BF16])
    (ga,) = _proj(h, w_in, 7, [BF16])
    (gb,) = _proj(h, w_in, 8, [BF16])

    if past_k is None:
        o_a = _attention_prompt(q, kb, vb, p["lam"], p["subln_w"], batch, seq)
    else:
        past = past_k.shape[1]
        lk = past + seq
        kb = jnp.concatenate([past_k.reshape(batch, past, D_MODEL).astype(BF16),
                              kb.reshape(batch, seq, D_MODEL)], axis=1).reshape(batch * lk, D_MODEL)
        vb = jnp.concatenate([past_v.reshape(batch, past, D_MODEL).astype(BF16),
                              vb.reshape(batch, seq, D_MODEL)], axis=1).reshape(batch * lk, D_MODEL)
        o_a = _attention_cached(q, kb, vb, p["lam"], p["subln_w"], batch, seq, lk)
    o_b, s_new = _hgrn(qh, fh, ih, gh, p["hgrn_lb"], p["hgrn_norm_w"], s0, batch, seq)

    base, hp, eidx, wts, pos, counts = _post(
        o_a, o_b, ga, gb, x2, gate1, scale2, shift2, gate2,
        p["w_proj_a"], p["w_proj_b"], p["w_out"], p["norm2_w"], p["w_router"], p["b_router"],
        p["w_sh_gate"], p["w_sh_up"], p["w_sh_down"], count0, seq, tile(512))
    return dict(base=base, hp=hp, eidx=eidx, wts=wts, pos=pos, counts=counts, k=k, v=v, s=s_new,
                gate2=gate2, seq=seq, tm=tile(512), shape=x.shape)


def kernel(x_prompt, x_sample, cache_attn_k, cache_attn_v, state_hgrn, c_prompt, c_sample, w_ada, b_ada, norm1_w, norm2_w, w_in, q_norm_w, k_norm_w, lambda_q1, lambda_k1, lambda_q2, lambda_k2, subln_w, hgrn_lb, hgrn_norm_w, w_proj_a, w_proj_b, w_out, w_router, b_router, w_exp_gate, w_exp_up, w_exp_down, w_sh_gate, w_sh_up, w_sh_down):
    bp, lp, _ = x_prompt.shape
    bs, ls, _ = x_sample.shape
    l = 0
    lam = (jnp.exp(jnp.sum(lambda_q1[l] * lambda_k1[l])) - jnp.exp(jnp.sum(lambda_q2[l] * lambda_k2[l]))
           + LAMBDA_INIT).astype(F32)
    row_norm = lambda w, s: math.sqrt(HEAD_DIM_A) * jnp.max(jnp.abs(w.astype(F32))) * (s * 1.01)
    knorm = row_norm(k_norm_w[l], 1.0)
    qnorm = row_norm(q_norm_w[l], HEAD_DIM_A ** -0.5 * LOG2E)
    s_bound = qnorm * knorm
    use_bound = (2.0 * s_bound < ATTN_BOUND_MAX_GAP).astype(F32)
    lam = jnp.stack([lam, s_bound, use_bound, jnp.zeros((), F32)]).reshape(1, 4)
    p = dict(
        norm1_w=norm1_w[l], norm2_w=norm2_w[l], w_in=w_in[l].astype(BF16),
        q_norm_w=q_norm_w[l], k_norm_w=k_norm_w[l], lam=lam, subln_w=subln_w[l],
        hgrn_lb=hgrn_lb, hgrn_norm_w=hgrn_norm_w[l],
        w_proj_a=w_proj_a[l].astype(BF16), w_proj_b=w_proj_b[l].astype(BF16), w_out=w_out[l].astype(BF16),
        w_router=w_router[l], b_router=b_router[l],
        w_sh_gate=w_sh_gate[l].astype(BF16), w_sh_up=w_sh_up[l].astype(BF16), w_sh_down=w_sh_down[l].astype(BF16),
    )
    mod = _ada(jnp.concatenate([c_prompt, c_sample], axis=0), w_ada[l], b_ada[l])
    mod = mod.reshape(bp + bs, 6, 1, D_MODEL)
    mods_p = [mod[:bp, j] for j in range(6)]
    mods_s = [mod[bp:, j] for j in range(6)]

    zero_state = jnp.zeros((bp, N_HEADS, HEAD_W, HEAD_W), F32)
    zero_count = jnp.zeros((N_EXPERTS, 1), F32)
    gp = _mix(x_prompt, mods_p, None, None, zero_state, zero_count, p)
    gs = _mix(x_sample, mods_s, cache_attn_k[l], cache_attn_v[l], state_hgrn[l], gp["counts"], p)
    groups = (gp, gs)

    counts = gs["counts"].reshape(N_EXPERTS).astype(I32)
    pcounts = (counts + ROW_BLOCK - 1) // ROW_BLOCK * ROW_BLOCK
    pend = jnp.cumsum(pcounts)
    pstart = pend - pcounts
    n_assign = (bp * lp + bs * ls) * TOP_K
    nb = -(-(n_assign + N_EXPERTS * (ROW_BLOCK - 1)) // ROW_BLOCK)
    blk_start = jnp.arange(nb, dtype=I32)[:, None] * ROW_BLOCK
    blk_e = jnp.minimum(jnp.sum((pend[None, :] <= blk_start).astype(I32), axis=1), N_EXPERTS - 1)
    n_used = (pend[-1] // ROW_BLOCK).astype(I32).reshape(1)
    pstart_col = pstart.astype(F32).reshape(N_EXPERTS, 1)

    last_blk = jnp.maximum(pend // ROW_BLOCK - 1, 0).astype(I32)
    xs = _zero_tails(last_blk, nb)
    for g in groups:
        g["dest"] = _dest(g["eidx"], g["pos"], pstart_col)
        xs = _dispatch(g["hp"], g["dest"], xs, g["tm"])
    ys = _experts(xs, blk_e, n_used, w_exp_gate[l], w_exp_up[l], w_exp_down[l])
    yp, ysm = [_combine(g["base"], g["wts"].T, g["gate2"], ys, g["dest"], g["seq"], g["tm"]).reshape(g["shape"])
               for g in groups]

    return (yp, ysm,
            gp["k"].reshape(1, bp, lp, N_HEADS, 2, HEAD_DIM_A), gp["v"].reshape(1, bp, lp, N_HEADS, HEAD_W),
            gp["s"][None],
            gs["k"].reshape(1, bs, ls, N_HEADS, 2, HEAD_DIM_A), gs["v"].reshape(1, bs, ls, N_HEADS, HEAD_W),
            gs["s"][None])
```

```python
import functools
import math

import jax
import jax.numpy as jnp
from jax import lax
from jax.experimental import pallas as pl
from jax.experimental.pallas import tpu as pltpu

F32 = jnp.float32
BF16 = jnp.bfloat16
U32 = jnp.uint32
I32 = jnp.int32

D_MODEL = 1024
N_HEADS = 8
HEAD_W = 128
HEAD_DIM_A = 64
CHUNK = 64
N_EXPERTS = 256
TOP_K = 8
N_GROUPS = 8
GROUP_W = N_EXPERTS // N_GROUPS
TOPK_GROUPS = 4
D_EXPERT = 256
ROUTED_SCALE = 2.5
EPS = 1e-6
LAMBDA_INIT = 0.8 - 0.6 * math.exp(-0.3 * 0)
LANES = 128
NEG = -1e30
LOG2E = math.log2(math.e)

ATTN_BOUND_MAX_GAP = 100.0
GLA_CHUNK = 32
ROW_BLOCK = 512
VMEM_LIMIT = 48 * 1024 * 1024


def _sigmoid(x):
    return 0.5 * jnp.tanh(0.5 * x) + 0.5


def _silu(x):
    return x * _sigmoid(x)


def _cparams(sem):
    return pltpu.CompilerParams(dimension_semantics=sem, vmem_limit_bytes=VMEM_LIMIT)


ROW_PARTS = 4


def _store_packed(ref, x, row0=0):
    m, half = x.shape[0], x.shape[1] // 2
    lo = pltpu.bitcast(x[:, :half].astype(BF16).astype(F32), U32)
    hi = pltpu.bitcast(x[:, half:].astype(BF16).astype(F32), U32)
    w = (hi & jnp.uint32(0xFFFF0000)) | (lo >> 16)
    for j in range(ROW_PARTS):
        ref[pl.ds(ROW_PARTS * row0 + j, m, stride=ROW_PARTS), :] = w[:, j * LANES:(j + 1) * LANES]


def _load_packed(ref, m, row0=0):
    parts = [ref[pl.ds(ROW_PARTS * row0 + j, m, stride=ROW_PARTS), :] for j in range(ROW_PARTS)]
    lo = [pltpu.bitcast(w << 16, F32) for w in parts]
    hi = [pltpu.bitcast(w & jnp.uint32(0xFFFF0000), F32) for w in parts]
    return jnp.concatenate(lo + hi, axis=1)


def _mod_spec(a, seq, tm):
    if a.ndim == 3:
        per_b = seq // tm
        return pl.BlockSpec((1, 1, D_MODEL), lambda i: (i // per_b, 0, 0))
    return pl.BlockSpec((tm, D_MODEL), lambda i: (i, 0))


def _mod_val(ref):
    v = ref[...]
    return v.reshape(v.shape[-2], v.shape[-1])


def _ada_kernel(c_ref, w_ref, b_ref, o_ref):
    c = _silu(c_ref[...])
    o_ref[...] = jnp.dot(c, w_ref[...], precision=lax.Precision.HIGHEST,
                         preferred_element_type=F32) + b_ref[...]


def _ada(c, w_ada, b_ada):
    n = c.shape[0]
    nj = w_ada.shape[1] // D_MODEL
    return pl.pallas_call(
        _ada_kernel,
        grid=(nj,),
        in_specs=[pl.BlockSpec((n, D_MODEL), lambda j: (0, 0)),
                  pl.BlockSpec((D_MODEL, D_MODEL), lambda j: (0, j)),
                  pl.BlockSpec((1, D_MODEL), lambda j: (0, j))],
        out_specs=pl.BlockSpec((n, D_MODEL), lambda j: (0, j)),
        out_shape=jax.ShapeDtypeStruct((n, w_ada.shape[1]), F32),
        compiler_params=_cparams(("arbitrary",)),
    )(c, w_ada, b_ada.reshape(1, -1))


def _norm_mod_kernel(x_ref, w_ref, sc_ref, sh_ref, o_ref):
    x = x_ref[...]
    y = x * lax.rsqrt(jnp.mean(x * x, axis=-1, keepdims=True) + EPS) * w_ref[...]
    o_ref[...] = (y * (1.0 + _mod_val(sc_ref)) + _mod_val(sh_ref)).astype(o_ref.dtype)


def _norm_mod(x2, w, scale, shift, seq, tm):
    t = x2.shape[0]
    return pl.pallas_call(
        _norm_mod_kernel,
        grid=(t // tm,),
        in_specs=[pl.BlockSpec((tm, D_MODEL), lambda i: (i, 0)),
                  pl.BlockSpec((1, D_MODEL), lambda i: (0, 0)),
                  _mod_spec(scale, seq, tm), _mod_spec(shift, seq, tm)],
        out_specs=pl.BlockSpec((tm, D_MODEL), lambda i: (i, 0)),
        out_shape=jax.ShapeDtypeStruct((t, D_MODEL), BF16),
        compiler_params=_cparams(("parallel",)),
    )(x2, w.reshape(1, -1), scale, shift)


def _head_norm(acc, w):
    r = lax.broadcasted_iota(I32, (HEAD_W, HEAD_W), 0) // HEAD_DIM_A
    c = lax.broadcasted_iota(I32, (HEAD_W, HEAD_W), 1) // HEAD_DIM_A
    ones = (r == c).astype(BF16)
    outs = []
    for h in range(N_HEADS):
        a = acc[:, h * HEAD_W:(h + 1) * HEAD_W]
        sq = a * a
        hi = sq.astype(BF16)
        lo = (sq - hi.astype(F32)).astype(BF16)
        ss = (jnp.dot(hi, ones, preferred_element_type=F32)
              + jnp.dot(lo, ones, preferred_element_type=F32))
        outs.append(a * lax.rsqrt(ss * (1.0 / HEAD_DIM_A) + EPS) * w[:, h * HEAD_W:(h + 1) * HEAD_W])
    return jnp.concatenate(outs, axis=1)


N_MAPS = D_MODEL // HEAD_DIM_A


def _proj_kernel(h_ref, w_ref, nw_ref, *o_refs, head_norm, out_scale, split_first):
    acc = jnp.dot(h_ref[...], w_ref[...], preferred_element_type=F32)
    if head_norm:
        acc = _head_norm(acc, nw_ref[...])
    for n, (o_ref, s) in enumerate(zip(o_refs, out_scale)):
        val = (acc if s == 1.0 else acc * s).astype(o_ref.dtype)
        if split_first and n == 0:
            for g in range(N_MAPS):
                o_ref[pl.ds(g, val.shape[0], stride=N_MAPS), :] = val[:, g * HEAD_DIM_A:(g + 1) * HEAD_DIM_A]
        else:
            o_ref[...] = val


def _proj(h, w_in_bf, group, out_dtypes, norm_w=None, out_scale=None, split_first=False, tm=1024):
    t = h.shape[0]
    tm = min(tm, t)
    nw = jnp.ones((1, D_MODEL), F32) if norm_w is None else norm_w
    out_scale = tuple(out_scale or (1.0,) * len(out_dtypes))
    specs = [pl.BlockSpec((tm, D_MODEL), lambda i: (i, 0)) for _ in out_dtypes]
    shapes = [jax.ShapeDtypeStruct((t, D_MODEL), dt) for dt in out_dtypes]
    if split_first:
        specs[0] = pl.BlockSpec((N_MAPS * tm, HEAD_DIM_A), lambda i: (i, 0))
        shapes[0] = jax.ShapeDtypeStruct((N_MAPS * t, HEAD_DIM_A), out_dtypes[0])
    outs = pl.pallas_call(
        functools.partial(_proj_kernel, head_norm=norm_w is not None, out_scale=out_scale,
                          split_first=split_first),
        grid=(t // tm,),
        in_specs=[pl.BlockSpec((tm, D_MODEL), lambda i: (i, 0)),
                  pl.BlockSpec((D_MODEL, D_MODEL), lambda i: (0, group)),
                  pl.BlockSpec((1, D_MODEL), lambda i: (0, 0))],
        out_specs=specs,
        out_shape=shapes,
        compiler_params=_cparams(("parallel",)),
    )(h, w_in_bf, nw)
    return outs


def _split_maps(q):
    lane = lax.broadcasted_iota(I32, q.shape, 1)
    zero = jnp.zeros_like(q)
    return jnp.where(lane < HEAD_DIM_A, q, zero), jnp.where(lane >= HEAD_DIM_A, q, zero)


def _finish_head(acc1, l1, acc2, l2, lam, sw):
    o = acc1 * (1.0 / l1) - acc2 * (lam / l2)
    o = o * lax.rsqrt(jnp.mean(o * o, axis=-1, keepdims=True) + EPS) * sw
    return o * (1.0 - LAMBDA_INIT)


def _attn_kernel(q_ref, k_ref, v_ref, lam_ref, sw_ref, o_ref, mx_s, mrep_s, ls_s, acc_s,
                 *, tq, tk, hp):
    qi = pl.program_id(2)
    nf = tk // LANES
    qs = []
    for hh in range(hp):
        qs.extend(_split_maps(q_ref[:, hh * HEAD_W:(hh + 1) * HEAD_W]))
    ns = 2 * hp

    r_chunk = lax.broadcasted_iota(I32, (tq, tk), 0) // CHUNK
    c_chunk = lax.broadcasted_iota(I32, (tq, tk), 1) // CHUNK
    diag_ok = c_chunk <= r_chunk

    def scores(kt, s, masked):
        k = k_ref[pl.ds(pl.multiple_of(kt * tk, tk), tk), (s // 2) * HEAD_W:(s // 2 + 1) * HEAD_W]
        sc = lax.dot_general(qs[s], k, (((1,), (1,)), ((), ())), preferred_element_type=F32)
        return jnp.where(diag_ok, sc, NEG) if masked else sc

    def fold(x, op):
        r = x[:, :LANES]
        for j in range(1, nf):
            r = op(r, x[:, j * LANES:(j + 1) * LANES])
        return r

    def pass1(kt, first):
        for s in range(ns):
            m = fold(scores(kt, s, first), jnp.maximum)
            mx_s[s] = m if first else jnp.maximum(mx_s[s], m)

    def pass2(kt, first):
        for s in range(ns):
            m = mrep_s[s]
            p = jnp.exp2(scores(kt, s, first) - jnp.concatenate([m] * nf, axis=1))
            v = v_ref[pl.ds(pl.multiple_of(kt * tk, tk), tk), (s // 2) * HEAD_W:(s // 2 + 1) * HEAD_W]
            pv = jnp.dot(p.astype(BF16), v, preferred_element_type=F32)
            ls_s[s] = fold(p, jnp.add) if first else ls_s[s] + fold(p, jnp.add)
            acc_s[s] = pv if first else acc_s[s] + pv

    def loop(fn):
        def body(kt, c):
            fn(kt, False)
            return c

        fn(qi, True)
        lax.fori_loop(0, qi, body, 0)

    use_bound = lam_ref[0, 2] > 0.5

    @pl.when(use_bound)
    def _():
        mrep_s[...] = jnp.full(mrep_s.shape, lam_ref[0, 1], F32)

    @pl.when(jnp.logical_not(use_bound))
    def _():
        loop(pass1)
        for s in range(ns):
            mrep_s[s] = jnp.broadcast_to(jnp.max(mx_s[s], axis=-1, keepdims=True), (tq, LANES))

    loop(pass2)

    lam = lam_ref[0, 0]
    ones = jnp.ones((LANES, LANES), BF16)

    def lane_sum(x):
        hi = x.astype(BF16)
        lo = (x - hi.astype(F32)).astype(BF16)
        return (jnp.dot(hi, ones, preferred_element_type=F32)
                + jnp.dot(lo, ones, preferred_element_type=F32))

    for hh in range(hp):
        o = _finish_head(acc_s[2 * hh], lane_sum(ls_s[2 * hh]), acc_s[2 * hh + 1], lane_sum(ls_s[2 * hh + 1]),
                         lam, sw_ref[...])
        o_ref[:, hh * HEAD_W:(hh + 1) * HEAD_W] = o.astype(o_ref.dtype)


def _attention_prompt(q, kb, vb, lam, subln_w, batch, seq):
    tq = tk = min(seq, 256)
    hp = 8
    nq = seq // tq
    kern = functools.partial(_attn_kernel, tq=tq, tk=tk, hp=hp)
    w = hp * HEAD_W
    return pl.pallas_call(
        kern,
        grid=(batch, N_HEADS // hp, nq),
        in_specs=[pl.BlockSpec((tq, w), lambda b, h, i: (b * nq + i, h)),
                  pl.BlockSpec((seq, w), lambda b, h, i: (b, h)),
                  pl.BlockSpec((seq, w), lambda b, h, i: (b, h)),
                  pl.BlockSpec((1, 4), lambda b, h, i: (0, 0), memory_space=pltpu.SMEM),
                  pl.BlockSpec((1, HEAD_W), lambda b, h, i: (0, 0))],
        out_specs=pl.BlockSpec((tq, w), lambda b, h, i: (b * nq + i, h)),
        out_shape=jax.ShapeDtypeStruct((batch * seq, D_MODEL), BF16),
        scratch_shapes=[pltpu.VMEM((2 * hp, tq, LANES), F32) for _ in range(4)],
        compiler_params=_cparams(("parallel", "parallel", "arbitrary")),
    )(q, kb, vb, lam, subln_w.reshape(1, -1))


def _attn_cached_kernel(q_ref, k_ref, v_ref, lam_ref, sw_ref, o_ref, *, lq, lk):
    qpos = (lk - lq) + lax.broadcasted_iota(I32, (lq, lk), 0)
    kpos = lax.broadcasted_iota(I32, (lq, lk), 1)
    allowed = (kpos // CHUNK) <= (qpos // CHUNK)
    lam = lam_ref[0, 0]
    for h in range(N_HEADS):
        sl = slice(h * HEAD_W, (h + 1) * HEAD_W)
        k = k_ref[:, sl]
        v = v_ref[:, sl]
        accs, ls = [], []
        for qm in _split_maps(q_ref[:, sl]):
            sc = lax.dot_general(qm, k, (((1,), (1,)), ((), ())), preferred_element_type=F32)
            sc = jnp.where(allowed, sc, NEG)
            p = jnp.exp2(sc - jnp.max(sc, axis=-1, keepdims=True))
            ls.append(jnp.sum(p, axis=-1, keepdims=True))
            accs.append(jnp.dot(p.astype(BF16), v, preferred_element_type=F32))
        o = _finish_head(accs[0], ls[0], accs[1], ls[1], lam, sw_ref[...])
        o_ref[:, sl] = o.astype(o_ref.dtype)


def _attention_cached(q, kb, vb, lam, subln_w, batch, lq, lk):
    return pl.pallas_call(
        functools.partial(_attn_cached_kernel, lq=lq, lk=lk),
        grid=(batch,),
        in_specs=[pl.BlockSpec((lq, D_MODEL), lambda b: (b, 0)),
                  pl.BlockSpec((lk, D_MODEL), lambda b: (b, 0)),
                  pl.BlockSpec((lk, D_MODEL), lambda b: (b, 0)),
                  pl.BlockSpec((1, 4), lambda b: (0, 0), memory_space=pltpu.SMEM),
                  pl.BlockSpec((1, HEAD_W), lambda b: (0, 0))],
        out_specs=pl.BlockSpec((lq, D_MODEL), lambda b: (b, 0)),
        out_shape=jax.ShapeDtypeStruct((batch * lq, D_MODEL), BF16),
        compiler_params=_cparams(("parallel",)),
    )(q, kb, vb, lam, subln_w.reshape(1, -1))


def _hgrn_kernel(qh_ref, fh_ref, ih_ref, gh_ref, lb_ref, nw_ref, s0_ref, o_ref, s_out_ref,
                 st_s, qin_s, qmid_s, kmid_s, kend_s, dec_s, *, ct, c):
    t = pl.program_id(1)
    nt = pl.num_programs(1)
    nc = ct // c

    @pl.when(t == 0)
    def _():
        for h in range(N_HEADS):
            st_s[h] = s0_ref[0, h].T

    a = lb_ref[...]
    amax = jnp.max(a, axis=0, keepdims=True)
    e = jnp.exp(a - amax)
    lb = e[0:1] / jnp.sum(e, axis=0, keepdims=True)

    row = lax.broadcasted_iota(I32, (c, c), 0)
    col = lax.broadcasted_iota(I32, (c, c), 1)
    causal = col <= row
    t_idx = lax.broadcasted_iota(I32, (c, D_MODEL), 0)

    def cumsum_rows(g):
        sh = 1
        while sh < c:
            g = g + jnp.where(t_idx >= sh, pltpu.roll(g, sh, axis=0), 0.0)
            sh *= 2
        return g

    for ci in range(nc):
        rows = slice(ci * c, (ci + 1) * c)
        f = lb + (1.0 - lb) * _sigmoid(fh_ref[rows, :])
        b = cumsum_rows(jnp.log2(f))
        b_last = b[c - 1:c, :]
        b_mid = b[c // 2 - 1:c // 2, :]
        q = _silu(qh_ref[rows, :].astype(F32)) * (HEAD_W ** -0.5)
        kk = 1.0 - f
        e_dn = jnp.exp2(b - b_mid)
        e_up = jnp.exp2(b_mid - b)
        qm = q * e_dn
        km = kk * e_up
        qmid_s[rows, :] = qm.astype(BF16)
        kmid_s[rows, :] = km.astype(BF16)
        qin_s[rows, :] = (qm * jnp.exp2(b_mid)).astype(BF16)
        kend_s[rows, :] = (km * jnp.exp2(b_last - b_mid)).astype(BF16)
        dec_s[ci:ci + 1, :] = jnp.exp2(b_last)

    nw = nw_ref[...]
    st = [st_s[h] for h in range(N_HEADS)]
    for ci in range(nc):
        rows = slice(ci * c, (ci + 1) * c)
        gate = _silu(gh_ref[rows, :].astype(F32))
        decay = dec_s[ci:ci + 1, :]
        for h in range(N_HEADS):
            sl = slice(h * HEAD_W, (h + 1) * HEAD_W)
            v = ih_ref[rows, sl]
            inter = lax.dot_general(qin_s[rows, sl], st[h].astype(BF16), (((1,), (1,)), ((), ())),
                                    preferred_element_type=F32)
            att = lax.dot_general(qmid_s[rows, sl], kmid_s[rows, sl], (((1,), (1,)), ((), ())),
                                  preferred_element_type=F32)
            att = jnp.where(causal, att, 0.0)
            o = inter + jnp.dot(att.astype(BF16), v, preferred_element_type=F32)
            upd = lax.dot_general(v, kend_s[rows, sl], (((0,), (0,)), ((), ())),
                                  preferred_element_type=F32)
            st[h] = decay[:, sl] * st[h] + upd
            o = o * lax.rsqrt(jnp.mean(o * o, axis=-1, keepdims=True) + EPS) * nw[:, sl]
            o_ref[rows, sl] = (o * gate[:, sl]).astype(o_ref.dtype)
    for h in range(N_HEADS):
        st_s[h] = st[h]

    @pl.when(t == nt - 1)
    def _():
        for h in range(N_HEADS):
            s_out_ref[0, h] = st_s[h].T


def _hgrn(qh, fh, ih, gh, hgrn_lb, norm_w, s0, batch, seq):
    ct = min(seq, 512)
    c = min(GLA_CHUNK, ct)
    nt = seq // ct
    kern = functools.partial(_hgrn_kernel, ct=ct, c=c)
    tok = pl.BlockSpec((ct, D_MODEL), lambda b, t: (b * nt + t, 0))
    st_spec = pl.BlockSpec((1, N_HEADS, HEAD_W, HEAD_W), lambda b, t: (b, 0, 0, 0))
    return pl.pallas_call(
        kern,
        grid=(batch, nt),
        in_specs=[tok, tok, tok, tok,
                  pl.BlockSpec((2, D_MODEL), lambda b, t: (0, 0)),
                  pl.BlockSpec((1, D_MODEL), lambda b, t: (0, 0)),
                  st_spec],
        out_specs=[tok, st_spec],
        out_shape=[jax.ShapeDtypeStruct((batch * seq, D_MODEL), BF16),
                   jax.ShapeDtypeStruct((batch, N_HEADS, HEAD_W, HEAD_W), F32)],
        scratch_shapes=[pltpu.VMEM((N_HEADS, HEAD_W, HEAD_W), F32)]
        + [pltpu.VMEM((ct, D_MODEL), BF16) for _ in range(4)]
        + [pltpu.VMEM((max(ct // c, 8), D_MODEL), F32)],
        compiler_params=_cparams(("parallel", "arbitrary")),
    )(qh, fh, ih, gh, hgrn_lb, jnp.tile(norm_w.reshape(1, -1), (1, N_HEADS)), s0)


def _route(h2, wrt_ref, brt_ref, carry_s, tm):
    nt_dims = (((1,), (1,)), ((), ()))
    w_hi = wrt_ref[0]
    w_lo = wrt_ref[1]
    h_hi = h2.astype(BF16)
    h_lo = (h2 - h_hi.astype(F32)).astype(BF16)
    logits = (lax.dot_general(w_hi, h_hi, nt_dims, preferred_element_type=F32)
              + lax.dot_general(w_hi, h_lo, nt_dims, preferred_element_type=F32)
              + lax.dot_general(w_lo, h_hi, nt_dims, preferred_element_type=F32))
    scores = _sigmoid(logits)
    biased = scores + brt_ref[...]
    big = float(2 * N_EXPERTS)

    x3 = biased.reshape(N_GROUPS, GROUP_W, tm)
    i3 = lax.broadcasted_iota(I32, x3.shape, 1).astype(F32)
    m1 = jnp.max(x3, axis=1, keepdims=True)
    i1 = jnp.min(jnp.where(x3 == m1, i3, big), axis=1, keepdims=True)
    m2 = jnp.max(jnp.where(i3 == i1, NEG, x3), axis=1, keepdims=True)
    gs = (m1 + m2).reshape(N_GROUPS, tm)

    g_iota = lax.broadcasted_iota(I32, (N_GROUPS, tm), 0).astype(F32)
    rem = gs
    gsel = jnp.zeros((N_GROUPS, tm), F32)
    for _ in range(TOPK_GROUPS):
        gm = jnp.max(rem, axis=0, keepdims=True)
        first = jnp.min(jnp.where(rem == gm, g_iota, big), axis=0, keepdims=True)
        sel = g_iota == first
        gsel = jnp.where(sel, 1.0, gsel)
        rem = jnp.where(sel, NEG, rem)
    keep = jnp.broadcast_to(gsel.reshape(N_GROUPS, 1, tm), x3.shape) > 0.5
    masked = jnp.where(keep, x3, NEG).reshape(N_EXPERTS, tm)

    e_iota = lax.broadcasted_iota(I32, (N_EXPERTS, tm), 0).astype(F32)
    onehot = jnp.zeros((N_EXPERTS, tm), F32)
    idxs, ws = [], []
    for _ in range(TOP_K):
        mk = jnp.max(masked, axis=0, keepdims=True)
        ik = jnp.min(jnp.where(masked == mk, e_iota, big), axis=0, keepdims=True)
        selk = e_iota == ik
        ws.append(jnp.sum(jnp.where(selk, scores, 0.0), axis=0, keepdims=True))
        idxs.append(ik)
        masked = jnp.where(selk, NEG, masked)
        onehot = jnp.where(selk, 1.0, onehot)
    wsum = ws[0]
    for k in range(1, TOP_K):
        wsum = wsum + ws[k]

    r = lax.broadcasted_iota(I32, (tm, tm), 0)
    cidx = lax.broadcasted_iota(I32, (tm, tm), 1)
    before = (r < cidx).astype(BF16)
    cum = jnp.dot(onehot.astype(BF16), before, preferred_element_type=F32) + carry_s[...]
    carry_s[...] = carry_s[...] + jnp.sum(onehot, axis=1, keepdims=True)

    k_iota = lax.broadcasted_iota(I32, (TOP_K, tm), 0)
    e_out = jnp.zeros((TOP_K, tm), F32)
    w_out = jnp.zeros((TOP_K, tm), F32)
    p_out = jnp.zeros((TOP_K, tm), F32)
    for k in range(TOP_K):
        pk = jnp.sum(jnp.where(e_iota == idxs[k], cum, 0.0), axis=0, keepdims=True)
        e_out = jnp.where(k_iota == k, idxs[k], e_out)
        w_out = jnp.where(k_iota == k, ws[k] / wsum * ROUTED_SCALE, w_out)
        p_out = jnp.where(k_iota == k, pk, p_out)
    return e_out, w_out, p_out


def _post_kernel(oa_ref, ob_ref, ga_ref, gb_ref, x_ref, g1_ref, sc2_ref, sh2_ref, g2_ref,
                 wpa_ref, wpb_ref, wo_ref, n2_ref, wrt_ref, brt_ref, wsg_ref, wsu_ref, wsd_ref, c0_ref,
                 base_ref, hp_ref, eidx_ref, wts_ref, pos_ref, cnt_ref, carry_s, *, tm):
    i = pl.program_id(0)

    @pl.when(i == 0)
    def _():
        carry_s[...] = c0_ref[...]

    pa = jnp.dot(oa_ref[...], wpa_ref[...], preferred_element_type=F32)
    pb = jnp.dot(ob_ref[...], wpb_ref[...], preferred_element_type=F32)
    u = _sigmoid(ga_ref[...].astype(F32)) * pa + _sigmoid(gb_ref[...].astype(F32)) * pb
    x1 = x_ref[...] + _mod_val(g1_ref) * jnp.dot(u.astype(BF16), wo_ref[...], preferred_element_type=F32)
    y = x1 * lax.rsqrt(jnp.mean(x1 * x1, axis=-1, keepdims=True) + EPS) * n2_ref[...]
    h2 = y * (1.0 + _mod_val(sc2_ref)) + _mod_val(sh2_ref)
    _store_packed(hp_ref, h2)

    hb = h2.astype(BF16)
    sg = jnp.dot(hb, wsg_ref[...], preferred_element_type=F32)
    su = jnp.dot(hb, wsu_ref[...], preferred_element_type=F32)
    shared = jnp.dot((_silu(sg) * su).astype(BF16), wsd_ref[...], preferred_element_type=F32)
    base_ref[...] = x1 + _mod_val(g2_ref) * shared

    e_out, w_out, p_out = _route(h2, wrt_ref, brt_ref, carry_s, tm)
    eidx_ref[...] = e_out.astype(I32)
    wts_ref[...] = w_out
    pos_ref[...] = p_out.astype(I32)
    cnt_ref[...] = carry_s[...]


def _post(oa, ob, ga, gb, x2, g1, sc2, sh2, g2, wpa, wpb, wo, n2, wr, br, wsg, wsu, wsd, count0, seq, tm):
    t = x2.shape[0]
    tok = lambda: pl.BlockSpec((tm, D_MODEL), lambda i: (i, 0))
    mod = lambda: _mod_spec(g1, seq, tm)
    full = lambda a: pl.BlockSpec(a.shape, lambda i: (0,) * a.ndim)
    n2 = n2.reshape(1, -1)
    wrt_hi = wr.T.astype(BF16)
    wrt = jnp.stack([wrt_hi, (wr.T - wrt_hi.astype(F32)).astype(BF16)])
    brt = br.reshape(-1, 1)
    k_out = lambda: pl.BlockSpec((TOP_K, tm), lambda i: (0, i))
    return pl.pallas_call(
        functools.partial(_post_kernel, tm=tm),
        grid=(t // tm,),
        in_specs=[tok(), tok(), tok(), tok(), tok(), mod(), mod(), mod(), mod(),
                  full(wpa), full(wpb), full(wo), full(n2), full(wrt), full(brt),
                  full(wsg), full(wsu), full(wsd), full(count0)],
        out_specs=[tok(), pl.BlockSpec((ROW_PARTS * tm, LANES), lambda i: (i, 0)),
                   k_out(), k_out(), k_out(),
                   pl.BlockSpec((N_EXPERTS, 1), lambda i: (0, 0))],
        out_shape=[jax.ShapeDtypeStruct((t, D_MODEL), F32),
                   jax.ShapeDtypeStruct((ROW_PARTS * t, LANES), U32),
                   jax.ShapeDtypeStruct((TOP_K, t), I32),
                   jax.ShapeDtypeStruct((TOP_K, t), F32),
                   jax.ShapeDtypeStruct((TOP_K, t), I32),
                   jax.ShapeDtypeStruct((N_EXPERTS, 1), F32)],
        scratch_shapes=[pltpu.VMEM((N_EXPERTS, 1), F32)],
        compiler_params=_cparams(("arbitrary",)),
    )(oa, ob, ga, gb, x2, g1, sc2, sh2, g2, wpa, wpb, wo, n2, wrt, brt, wsg, wsu, wsd, count0)


def _dest_kernel(eidx_ref, pos_ref, pstart_ref, o_ref, *, tm):
    e_iota = lax.broadcasted_iota(I32, (N_EXPERTS, tm), 0)
    k_iota = lax.broadcasted_iota(I32, (TOP_K, tm), 0)
    eidx = eidx_ref[...]
    start = jnp.zeros((TOP_K, tm), F32)
    for k in range(TOP_K):
        sel = e_iota == eidx[k:k + 1, :]
        sk = jnp.sum(jnp.where(sel, pstart_ref[...], 0.0), axis=0, keepdims=True)
        start = jnp.where(k_iota == k, sk, start)
    o_ref[...] = (start.astype(I32) + pos_ref[...]) * ROW_PARTS


def _dest(eidx, pos, pstart_col):
    t = eidx.shape[1]
    tm = min(t, 1024)
    blk = lambda: pl.BlockSpec((TOP_K, tm), lambda i: (0, i))
    return pl.pallas_call(
        functools.partial(_dest_kernel, tm=tm),
        grid=(t // tm,),
        in_specs=[blk(), blk(), pl.BlockSpec((N_EXPERTS, 1), lambda i: (0, 0))],
        out_specs=blk(),
        out_shape=jax.ShapeDtypeStruct((TOP_K, t), I32),
        compiler_params=_cparams(("parallel",)),
    )(eidx, pos, pstart_col)


def _zero_tail_kernel(last_ref, o_ref):
    del last_ref
    o_ref[...] = jnp.zeros(o_ref.shape, o_ref.dtype)


def _zero_tails(last_blk, nb):
    grid_spec = pltpu.PrefetchScalarGridSpec(
        num_scalar_prefetch=1,
        grid=(N_EXPERTS,),
        in_specs=[],
        out_specs=pl.BlockSpec((ROW_PARTS * ROW_BLOCK, LANES), lambda e, last: (last[e], 0)),
    )
    return pl.pallas_call(
        _zero_tail_kernel,
        grid_spec=grid_spec,
        out_shape=jax.ShapeDtypeStruct((ROW_PARTS * nb * ROW_BLOCK, LANES), U32),
        compiler_params=_cparams(("arbitrary",)),
    )(last_blk)


def _dispatch_kernel(dest_ref, hp_ref, xs_in_ref, xs_ref, sem, *, tm):
    del xs_in_ref

    def row_copy(t, d):
        return pltpu.make_async_copy(
            hp_ref.at[pl.ds(pl.multiple_of(t * ROW_PARTS, ROW_PARTS), ROW_PARTS), :],
            xs_ref.at[pl.ds(pl.multiple_of(d, ROW_PARTS), ROW_PARTS), :], sem)

    def issue(t, c):
        for k in range(TOP_K):
            row_copy(t, dest_ref[k, t]).start(priority=k % 2)
        return c

    def drain(t, c):
        for k in range(TOP_K):
            row_copy(0, 0).wait()
        return c

    lax.fori_loop(0, tm, issue, 0)
    lax.fori_loop(0, tm, drain, 0)


def _dispatch(hp, dest, xs, tm):
    t = dest.shape[1]
    return pl.pallas_call(
        functools.partial(_dispatch_kernel, tm=tm),
        grid=(t // tm,),
        in_specs=[pl.BlockSpec((TOP_K, tm), lambda i: (0, i), memory_space=pltpu.SMEM),
                  pl.BlockSpec((ROW_PARTS * tm, LANES), lambda i: (i, 0)),
                  pl.BlockSpec(memory_space=pl.ANY)],
        out_specs=pl.BlockSpec(memory_space=pl.ANY),
        out_shape=jax.ShapeDtypeStruct(xs.shape, xs.dtype),
        scratch_shapes=[pltpu.SemaphoreType.DMA],
        input_output_aliases={2: 0},
        compiler_params=_cparams(("arbitrary",)),
    )(dest, hp, xs)


def _expert_kernel(be_ref, nu_ref, xs_ref, wg_ref, wu_ref, wd_ref, ys_ref, wg_s, wu_s, wd_s):
    i = pl.program_id(0)
    used = i < nu_ref[0]

    @pl.when(jnp.logical_and(used, jnp.logical_or(i == 0, be_ref[i] != be_ref[jnp.maximum(i - 1, 0)])))
    def _():
        wg_s[...] = wg_ref[0].astype(BF16)
        wu_s[...] = wu_ref[0].astype(BF16)
        wd_s[...] = wd_ref[0].astype(BF16)

    @pl.when(used)
    def _():
        x = _load_packed(xs_ref, ROW_BLOCK).astype(BF16)
        g = jnp.dot(x, wg_s[...], preferred_element_type=F32)
        u = jnp.dot(x, wu_s[...], preferred_element_type=F32)
        hmid = (_silu(g) * u).astype(BF16)
        _store_packed(ys_ref, jnp.dot(hmid, wd_s[...], preferred_element_type=F32))

    @pl.when(jnp.logical_not(used))
    def _():
        ys_ref[...] = jnp.zeros(ys_ref.shape, ys_ref.dtype)


def _experts(xs, blk_e, n_used, w_eg, w_eu, w_ed):
    nb = xs.shape[0] // (ROW_PARTS * ROW_BLOCK)
    rows_spec = lambda: pl.BlockSpec((ROW_PARTS * ROW_BLOCK, LANES), lambda i, be, nu: (i, 0))
    used_rows_spec = pl.BlockSpec((ROW_PARTS * ROW_BLOCK, LANES),
                                  lambda i, be, nu: (jnp.maximum(jnp.minimum(i, nu[0] - 1), 0), 0))
    w_up = lambda: pl.BlockSpec((1, D_MODEL, D_EXPERT), lambda i, be, nu: (be[i], 0, 0))
    grid_spec = pltpu.PrefetchScalarGridSpec(
        num_scalar_prefetch=2,
        grid=(nb,),
        in_specs=[used_rows_spec, w_up(), w_up(),
                  pl.BlockSpec((1, D_EXPERT, D_MODEL), lambda i, be, nu: (be[i], 0, 0))],
        out_specs=rows_spec(),
        scratch_shapes=[pltpu.VMEM((D_MODEL, D_EXPERT), BF16), pltpu.VMEM((D_MODEL, D_EXPERT), BF16),
                        pltpu.VMEM((D_EXPERT, D_MODEL), BF16)],
    )
    return pl.pallas_call(
        _expert_kernel,
        grid_spec=grid_spec,
        out_shape=jax.ShapeDtypeStruct(xs.shape, U32),
        compiler_params=_cparams(("arbitrary",)),
    )(blk_e, n_used, xs, w_eg, w_eu, w_ed)


def _combine_kernel(dest_ref, base_ref, wts_ref, g2_ref, ys_ref, o_ref, buf, sem, *, tm):
    def row_copy(d, k, t):
        return pltpu.make_async_copy(
            ys_ref.at[pl.ds(pl.multiple_of(d, ROW_PARTS), ROW_PARTS), :],
            buf.at[k, pl.ds(pl.multiple_of(t * ROW_PARTS, ROW_PARTS), ROW_PARTS), :], sem)

    def issue(t, c):
        for k in range(TOP_K):
            row_copy(dest_ref[k, t], k, t).start(priority=k % 2)
        return c

    def drain(t, c):
        for k in range(TOP_K):
            row_copy(0, 0, 0).wait()
        return c

    lax.fori_loop(0, tm, issue, 0)
    lax.fori_loop(0, tm, drain, 0)

    w = wts_ref[...]
    y = jnp.zeros((tm, D_MODEL), F32)
    for k in range(TOP_K):
        y = y + w[:, k:k + 1] * _load_packed(buf.at[k], tm)
    o_ref[...] = base_ref[...] + _mod_val(g2_ref) * y


def _combine(base, wts_t, g2, ys, dest, seq, tm):
    t = base.shape[0]
    return pl.pallas_call(
        functools.partial(_combine_kernel, tm=tm),
        grid=(t // tm,),
        in_specs=[pl.BlockSpec((TOP_K, tm), lambda i: (0, i), memory_space=pltpu.SMEM),
                  pl.BlockSpec((tm, D_MODEL), lambda i: (i, 0)),
                  pl.BlockSpec((tm, TOP_K), lambda i: (i, 0)),
                  _mod_spec(g2, seq, tm),
                  pl.BlockSpec(memory_space=pl.ANY)],
        out_specs=pl.BlockSpec((tm, D_MODEL), lambda i: (i, 0)),
        out_shape=jax.ShapeDtypeStruct((t, D_MODEL), F32),
        scratch_shapes=[pltpu.VMEM((TOP_K, ROW_PARTS * tm, LANES), U32), pltpu.SemaphoreType.DMA],
        compiler_params=_cparams(("arbitrary",)),
    )(dest, base, wts_t, g2, ys)


def _mix(x, mods, past_k, past_v, s0, count0, p):
    batch, seq, _ = x.shape
    t = batch * seq
    x2 = x.reshape(t, D_MODEL)
    if seq % 256 == 0:
        tile = lambda cap: min(seq, cap)
    else:
        mods = [jnp.broadcast_to(m, (batch, seq, D_MODEL)).reshape(t, D_MODEL) for m in mods]
        tile = lambda cap: min(t, cap)
    shift1, scale1, gate1, shift2, scale2, gate2 = mods

    h = _norm_mod(x2, p["norm1_w"], scale1, shift1, seq, tile(512))
    qn = jnp.tile(p["q_norm_w"].reshape(1, -1), (1, D_MODEL // HEAD_DIM_A))
    kn = jnp.tile(p["k_norm_w"].reshape(1, -1), (1, D_MODEL // HEAD_DIM_A))
    w_in = p["w_in"]
    (q,) = _proj(h, w_in, 0, [BF16], qn, out_scale=(HEAD_DIM_A ** -0.5 * LOG2E,))
    k, kb = _proj(h, w_in, 1, [F32, BF16], kn, split_first=True, tm=512)
    v, vb = _proj(h, w_in, 2, [F32, BF16])
    (qh,) = _proj(h, w_in, 3, [BF16])
    (fh,) = _proj(h, w_in, 4, [F32])
    (ih,) = _proj(h, w_in, 5, [BF16])
    (gh,) = _proj(h, w_in, 6, [BF16])
    (ga,) = _proj(h, w_in, 7, [BF16])
    (gb,) = _proj(h, w_in, 8, [BF16])

    if past_k is None:
        o_a = _attention_prompt(q, kb, vb, p["lam"], p["subln_w"], batch, seq)
    else:
        past = past_k.shape[1]
        lk = past + seq
        kb = jnp.concatenate([past_k.reshape(batch, past, D_MODEL).astype(BF16),
                              kb.reshape(batch, seq, D_MODEL)], axis=1).reshape(batch * lk, D_MODEL)
        vb = jnp.concatenate([past_v.reshape(batch, past, D_MODEL).astype(BF16),
                              vb.reshape(batch, seq, D_MODEL)], axis=1).reshape(batch * lk, D_MODEL)
        o_a = _attention_cached(q, kb, vb, p["lam"], p["subln_w"], batch, seq, lk)
    o_b, s_new = _hgrn(qh, fh, ih, gh, p["hgrn_lb"], p["hgrn_norm_w"], s0, batch, seq)

    base, hp, eidx, wts, pos, counts = _post(
        o_a, o_b, ga, gb, x2, gate1, scale2, shift2, gate2,
        p["w_proj_a"], p["w_proj_b"], p["w_out"], p["norm2_w"], p["w_router"], p["b_router"],
        p["w_sh_gate"], p["w_sh_up"], p["w_sh_down"], count0, seq, tile(512))
    return dict(base=base, hp=hp, eidx=eidx, wts=wts, pos=pos, counts=counts, k=k, v=v, s=s_new,
                gate2=gate2, seq=seq, tm=tile(512), shape=x.shape)


def kernel(x_prompt, x_sample, cache_attn_k, cache_attn_v, state_hgrn, c_prompt, c_sample, w_ada, b_ada, norm1_w, norm2_w, w_in, q_norm_w, k_norm_w, lambda_q1, lambda_k1, lambda_q2, lambda_k2, subln_w, hgrn_lb, hgrn_norm_w, w_proj_a, w_proj_b, w_out, w_router, b_router, w_exp_gate, w_exp_up, w_exp_down, w_sh_gate, w_sh_up, w_sh_down):
    bp, lp, _ = x_prompt.shape
    bs, ls, _ = x_sample.shape
    l = 0
    lam = (jnp.exp(jnp.sum(lambda_q1[l] * lambda_k1[l])) - jnp.exp(jnp.sum(lambda_q2[l] * lambda_k2[l]))
           + LAMBDA_INIT).astype(F32)
    row_norm = lambda w, s: math.sqrt(HEAD_DIM_A) * jnp.max(jnp.abs(w.astype(F32))) * (s * 1.01)
    knorm = row_norm(k_norm_w[l], 1.0)
    qnorm = row_norm(q_norm_w[l], HEAD_DIM_A ** -0.5 * LOG2E)
    s_bound = qnorm * knorm
    use_bound = (2.0 * s_bound < ATTN_BOUND_MAX_GAP).astype(F32)
    lam = jnp.stack([lam, s_bound, use_bound, jnp.zeros((), F32)]).reshape(1, 4)
    p = dict(
        norm1_w=norm1_w[l], norm2_w=norm2_w[l], w_in=w_in[l].astype(BF16),
        q_norm_w=q_norm_w[l], k_norm_w=k_norm_w[l], lam=lam, subln_w=subln_w[l],
        hgrn_lb=hgrn_lb, hgrn_norm_w=hgrn_norm_w[l],
        w_proj_a=w_proj_a[l].astype(BF16), w_proj_b=w_proj_b[l].astype(BF16), w_out=w_out[l].astype(BF16),
        w_router=w_router[l], b_router=b_router[l],
        w_sh_gate=w_sh_gate[l].astype(BF16), w_sh_up=w_sh_up[l].astype(BF16), w_sh_down=w_sh_down[l].astype(BF16),
    )
    mod = _ada(jnp.concatenate([c_prompt, c_sample], axis=0), w_ada[l], b_ada[l])
    mod = mod.reshape(bp + bs, 6, 1, D_MODEL)
    mods_p = [mod[:bp, j] for j in range(6)]
    mods_s = [mod[bp:, j] for j in range(6)]

    zero_state = jnp.zeros((bp, N_HEADS, HEAD_W, HEAD_W), F32)
    zero_count = jnp.zeros((N_EXPERTS, 1), F32)
    gp = _mix(x_prompt, mods_p, None, None, zero_state, zero_count, p)
    gs = _mix(x_sample, mods_s, cache_attn_k[l], cache_attn_v[l], state_hgrn[l], gp["counts"], p)
    groups = (gp, gs)

    counts = gs["counts"].reshape(N_EXPERTS).astype(I32)
    pcounts = (counts + ROW_BLOCK - 1) // ROW_BLOCK * ROW_BLOCK
    pend = jnp.cumsum(pcounts)
    pstart = pend - pcounts
    n_assign = (bp * lp + bs * ls) * TOP_K
    nb = -(-(n_assign + N_EXPERTS * (ROW_BLOCK - 1)) // ROW_BLOCK)
    blk_start = jnp.arange(nb, dtype=I32)[:, None] * ROW_BLOCK
    blk_e = jnp.minimum(jnp.sum((pend[None, :] <= blk_start).astype(I32), axis=1), N_EXPERTS - 1)
    n_used = (pend[-1] // ROW_BLOCK).astype(I32).reshape(1)
    pstart_col = pstart.astype(F32).reshape(N_EXPERTS, 1)

    last_blk = jnp.maximum(pend // ROW_BLOCK - 1, 0).astype(I32)
    xs = _zero_tails(last_blk, nb)
    for g in groups:
        g["dest"] = _dest(g["eidx"], g["pos"], pstart_col)
        xs = _dispatch(g["hp"], g["dest"], xs, g["tm"])
    ys = _experts(xs, blk_e, n_used, w_exp_gate[l], w_exp_up[l], w_exp_down[l])
    yp, ysm = [_combine(g["base"], g["wts"].T, g["gate2"], ys, g["dest"], g["seq"], g["tm"]).reshape(g["shape"])
               for g in groups]

    return (yp, ysm,
            gp["k"].reshape(1, bp, lp, N_HEADS, 2, HEAD_DIM_A), gp["v"].reshape(1, bp, lp, N_HEADS, HEAD_W),
            gp["s"][None],
            gs["k"].reshape(1, bs, ls, N_HEADS, 2, HEAD_DIM_A), gs["v"].reshape(1, bs, ls, N_HEADS, HEAD_W),
            gs["s"][None])
```

```python
import functools
import math

import jax
import jax.numpy as jnp
from jax import lax
from jax.experimental import pallas as pl
from jax.experimental.pallas import tpu as pltpu

F32 = jnp.float32
BF16 = jnp.bfloat16
U32 = jnp.uint32
I32 = jnp.int32

D_MODEL = 1024
N_HEADS = 8
HEAD_W = 128
HEAD_DIM_A = 64
CHUNK = 64
N_EXPERTS = 256
TOP_K = 8
N_GROUPS = 8
GROUP_W = N_EXPERTS // N_GROUPS
TOPK_GROUPS = 4
D_EXPERT = 256
ROUTED_SCALE = 2.5
EPS = 1e-6
LAMBDA_INIT = 0.8 - 0.6 * math.exp(-0.3 * 0)
LANES = 128
NEG = -1e30
LOG2E = math.log2(math.e)

ATTN_BOUND_MAX_GAP = 100.0
GLA_CHUNK = 32
ROW_BLOCK = 512
VMEM_LIMIT = 48 * 1024 * 1024


def _sigmoid(x):
    return 0.5 * jnp.tanh(0.5 * x) + 0.5


def _silu(x):
    return x * _sigmoid(x)


def _cparams(sem):
    return pltpu.CompilerParams(dimension_semantics=sem, vmem_limit_bytes=VMEM_LIMIT)


ROW_PARTS = 4


def _store_packed(ref, x, row0=0):
    m, half = x.shape[0], x.shape[1] // 2
    lo = pltpu.bitcast(x[:, :half].astype(BF16).astype(F32), U32)
    hi = pltpu.bitcast(x[:, half:].astype(BF16).astype(F32), U32)
    w = (hi & jnp.uint32(0xFFFF0000)) | (lo >> 16)
    for j in range(ROW_PARTS):
        ref[pl.ds(ROW_PARTS * row0 + j, m, stride=ROW_PARTS), :] = w[:, j * LANES:(j + 1) * LANES]


def _load_packed(ref, m, row0=0):
    parts = [ref[pl.ds(ROW_PARTS * row0 + j, m, stride=ROW_PARTS), :] for j in range(ROW_PARTS)]
    lo = [pltpu.bitcast(w << 16, F32) for w in parts]
    hi = [pltpu.bitcast(w & jnp.uint32(0xFFFF0000), F32) for w in parts]
    return jnp.concatenate(lo + hi, axis=1)


def _mod_spec(a, seq, tm):
    if a.ndim == 3:
        per_b = seq // tm
        return pl.BlockSpec((1, 1, D_MODEL), lambda i: (i // per_b, 0, 0))
    return pl.BlockSpec((tm, D_MODEL), lambda i: (i, 0))


def _mod_val(ref):
    v = ref[...]
    return v.reshape(v.shape[-2], v.shape[-1])


def _ada_kernel(c_ref, w_ref, b_ref, o_ref):
    c = _silu(c_ref[...])
    o_ref[...] = jnp.dot(c, w_ref[...], precision=lax.Precision.HIGHEST,
                         preferred_element_type=F32) + b_ref[...]


def _ada(c, w_ada, b_ada):
    n = c.shape[0]
    nj = w_ada.shape[1] // D_MODEL
    return pl.pallas_call(
        _ada_kernel,
        grid=(nj,),
        in_specs=[pl.BlockSpec((n, D_MODEL), lambda j: (0, 0)),
                  pl.BlockSpec((D_MODEL, D_MODEL), lambda j: (0, j)),
                  pl.BlockSpec((1, D_MODEL), lambda j: (0, j))],
        out_specs=pl.BlockSpec((n, D_MODEL), lambda j: (0, j)),
        out_shape=jax.ShapeDtypeStruct((n, w_ada.shape[1]), F32),
        compiler_params=_cparams(("arbitrary",)),
    )(c, w_ada, b_ada.reshape(1, -1))


def _norm_mod_kernel(x_ref, w_ref, sc_ref, sh_ref, o_ref):
    x = x_ref[...]
    y = x * lax.rsqrt(jnp.mean(x * x, axis=-1, keepdims=True) + EPS) * w_ref[...]
    o_ref[...] = (y * (1.0 + _mod_val(sc_ref)) + _mod_val(sh_ref)).astype(o_ref.dtype)


def _norm_mod(x2, w, scale, shift, seq, tm):
    t = x2.shape[0]
    return pl.pallas_call(
        _norm_mod_kernel,
        grid=(t // tm,),
        in_specs=[pl.BlockSpec((tm, D_MODEL), lambda i: (i, 0)),
                  pl.BlockSpec((1, D_MODEL), lambda i: (0, 0)),
                  _mod_spec(scale, seq, tm), _mod_spec(shift, seq, tm)],
        out_specs=pl.BlockSpec((tm, D_MODEL), lambda i: (i, 0)),
        out_shape=jax.ShapeDtypeStruct((t, D_MODEL), BF16),
        compiler_params=_cparams(("parallel",)),
    )(x2, w.reshape(1, -1), scale, shift)


def _head_norm(acc, w):
    r = lax.broadcasted_iota(I32, (HEAD_W, HEAD_W), 0) // HEAD_DIM_A
    c = lax.broadcasted_iota(I32, (HEAD_W, HEAD_W), 1) // HEAD_DIM_A
    ones = (r == c).astype(BF16)
    outs = []
    for h in range(N_HEADS):
        a = acc[:, h * HEAD_W:(h + 1) * HEAD_W]
        sq = a * a
        hi = sq.astype(BF16)
        lo = (sq - hi.astype(F32)).astype(BF16)
        ss = (jnp.dot(hi, ones, preferred_element_type=F32)
              + jnp.dot(lo, ones, preferred_element_type=F32))
        outs.append(a * lax.rsqrt(ss * (1.0 / HEAD_DIM_A) + EPS) * w[:, h * HEAD_W:(h + 1) * HEAD_W])
    return jnp.concatenate(outs, axis=1)


N_MAPS = D_MODEL // HEAD_DIM_A


def _proj_kernel(h_ref, w_ref, nw_ref, *o_refs, head_norm, out_scale, split_first):
    acc = jnp.dot(h_ref[...], w_ref[...], preferred_element_type=F32)
    if head_norm:
        acc = _head_norm(acc, nw_ref[...])
    for n, (o_ref, s) in enumerate(zip(o_refs, out_scale)):
        val = (acc if s == 1.0 else acc * s).astype(o_ref.dtype)
        if split_first and n == 0:
            for g in range(N_MAPS):
                o_ref[pl.ds(g, val.shape[0], stride=N_MAPS), :] = val[:, g * HEAD_DIM_A:(g + 1) * HEAD_DIM_A]
        else:
            o_ref[...] = val


def _proj(h, w_in_bf, group, out_dtypes, norm_w=None, out_scale=None, split_first=False, tm=1024):
    t = h.shape[0]
    tm = min(tm, t)
    nw = jnp.ones((1, D_MODEL), F32) if norm_w is None else norm_w
    out_scale = tuple(out_scale or (1.0,) * len(out_dtypes))
    specs = [pl.BlockSpec((tm, D_MODEL), lambda i: (i, 0)) for _ in out_dtypes]
    shapes = [jax.ShapeDtypeStruct((t, D_MODEL), dt) for dt in out_dtypes]
    if split_first:
        specs[0] = pl.BlockSpec((N_MAPS * tm, HEAD_DIM_A), lambda i: (i, 0))
        shapes[0] = jax.ShapeDtypeStruct((N_MAPS * t, HEAD_DIM_A), out_dtypes[0])
    outs = pl.pallas_call(
        functools.partial(_proj_kernel, head_norm=norm_w is not None, out_scale=out_scale,
                          split_first=split_first),
        grid=(t // tm,),
        in_specs=[pl.BlockSpec((tm, D_MODEL), lambda i: (i, 0)),
                  pl.BlockSpec((D_MODEL, D_MODEL), lambda i: (0, group)),
                  pl.BlockSpec((1, D_MODEL), lambda i: (0, 0))],
        out_specs=specs,
        out_shape=shapes,
        compiler_params=_cparams(("parallel",)),
    )(h, w_in_bf, nw)
    return outs


def _split_maps(q):
    lane = lax.broadcasted_iota(I32, q.shape, 1)
    zero = jnp.zeros_like(q)
    return jnp.where(lane < HEAD_DIM_A, q, zero), jnp.where(lane >= HEAD_DIM_A, q, zero)


def _finish_head(acc1, l1, acc2, l2, lam, sw):
    o = acc1 * (1.0 / l1) - acc2 * (lam / l2)
    o = o * lax.rsqrt(jnp.mean(o * o, axis=-1, keepdims=True) + EPS) * sw
    return o * (1.0 - LAMBDA_INIT)


def _attn_kernel(q_ref, k_ref, v_ref, lam_ref, sw_ref, o_ref, mx_s, mrep_s, ls_s, acc_s,
                 *, tq, tk, hp):
    qi = pl.program_id(2)
    nf = tk // LANES
    qs = []
    for hh in range(hp):
        qs.extend(_split_maps(q_ref[:, hh * HEAD_W:(hh + 1) * HEAD_W]))
    ns = 2 * hp

    r_chunk = lax.broadcasted_iota(I32, (tq, tk), 0) // CHUNK
    c_chunk = lax.broadcasted_iota(I32, (tq, tk), 1) // CHUNK
    diag_ok = c_chunk <= r_chunk

    def scores(kt, s, masked):
        k = k_ref[pl.ds(pl.multiple_of(kt * tk, tk), tk), (s // 2) * HEAD_W:(s // 2 + 1) * HEAD_W]
        sc = lax.dot_general(qs[s], k, (((1,), (1,)), ((), ())), preferred_element_type=F32)
        return jnp.where(diag_ok, sc, NEG) if masked else sc

    def fold(x, op):
        r = x[:, :LANES]
        for j in range(1, nf):
            r = op(r, x[:, j * LANES:(j + 1) * LANES])
        return r

    def pass1(kt, first):
        for s in range(ns):
            m = fold(scores(kt, s, first), jnp.maximum)
            mx_s[s] = m if first else jnp.maximum(mx_s[s], m)

    def pass2(kt, first):
        for s in range(ns):
            m = mrep_s[s]
            p = jnp.exp2(scores(kt, s, first) - jnp.concatenate([m] * nf, axis=1))
            v = v_ref[pl.ds(pl.multiple_of(kt * tk, tk), tk), (s // 2) * HEAD_W:(s // 2 + 1) * HEAD_W]
            pv = jnp.dot(p.astype(BF16), v, preferred_element_type=F32)
            ls_s[s] = fold(p, jnp.add) if first else ls_s[s] + fold(p, jnp.add)
            acc_s[s] = pv if first else acc_s[s] + pv

    def loop(fn):
        def body(kt, c):
            fn(kt, False)
            return c

        fn(qi, True)
        lax.fori_loop(0, qi, body, 0)

    use_bound = lam_ref[0, 2] > 0.5

    @pl.when(use_bound)
    def _():
        mrep_s[...] = jnp.full(mrep_s.shape, lam_ref[0, 1], F32)

    @pl.when(jnp.logical_not(use_bound))
    def _():
        loop(pass1)
        for s in range(ns):
            mrep_s[s] = jnp.broadcast_to(jnp.max(mx_s[s], axis=-1, keepdims=True), (tq, LANES))

    loop(pass2)

    lam = lam_ref[0, 0]
    ones = jnp.ones((LANES, LANES), BF16)

    def lane_sum(x):
        hi = x.astype(BF16)
        lo = (x - hi.astype(F32)).astype(BF16)
        return (jnp.dot(hi, ones, preferred_element_type=F32)
                + jnp.dot(lo, ones, preferred_element_type=F32))

    for hh in range(hp):
        o = _finish_head(acc_s[2 * hh], lane_sum(ls_s[2 * hh]), acc_s[2 * hh + 1], lane_sum(ls_s[2 * hh + 1]),
                         lam, sw_ref[...])
        o_ref[:, hh * HEAD_W:(hh + 1) * HEAD_W] = o.astype(o_ref.dtype)


def _attention_prompt(q, kb, vb, lam, subln_w, batch, seq):
    tq = tk = min(seq, 256)
    hp = 8
    nq = seq // tq
    kern = functools.partial(_attn_kernel, tq=tq, tk=tk, hp=hp)
    w = hp * HEAD_W
    return pl.pallas_call(
        kern,
        grid=(batch, N_HEADS // hp, nq),
        in_specs=[pl.BlockSpec((tq, w), lambda b, h, i: (b * nq + i, h)),
                  pl.BlockSpec((seq, w), lambda b, h, i: (b, h)),
                  pl.BlockSpec((seq, w), lambda b, h, i: (b, h)),
                  pl.BlockSpec((1, 4), lambda b, h, i: (0, 0), memory_space=pltpu.SMEM),
                  pl.BlockSpec((1, HEAD_W), lambda b, h, i: (0, 0))],
        out_specs=pl.BlockSpec((tq, w), lambda b, h, i: (b * nq + i, h)),
        out_shape=jax.ShapeDtypeStruct((batch * seq, D_MODEL), BF16),
        scratch_shapes=[pltpu.VMEM((2 * hp, tq, LANES), F32) for _ in range(4)],
        compiler_params=_cparams(("parallel", "parallel", "arbitrary")),
    )(q, kb, vb, lam, subln_w.reshape(1, -1))


def _attn_cached_kernel(q_ref, k_ref, v_ref, lam_ref, sw_ref, o_ref, *, lq, lk):
    qpos = (lk - lq) + lax.broadcasted_iota(I32, (lq, lk), 0)
    kpos = lax.broadcasted_iota(I32, (lq, lk), 1)
    allowed = (kpos // CHUNK) <= (qpos // CHUNK)
    lam = lam_ref[0, 0]
    for h in range(N_HEADS):
        sl = slice(h * HEAD_W, (h + 1) * HEAD_W)
        k = k_ref[:, sl]
        v = v_ref[:, sl]
        accs, ls = [], []
        for qm in _split_maps(q_ref[:, sl]):
            sc = lax.dot_general(qm, k, (((1,), (1,)), ((), ())), preferred_element_type=F32)
            sc = jnp.where(allowed, sc, NEG)
            p = jnp.exp2(sc - jnp.max(sc, axis=-1, keepdims=True))
            ls.append(jnp.sum(p, axis=-1, keepdims=True))
            accs.append(jnp.dot(p.astype(BF16), v, preferred_element_type=F32))
        o = _finish_head(accs[0], ls[0], accs[1], ls[1], lam, sw_ref[...])
        o_ref[:, sl] = o.astype(o_ref.dtype)


def _attention_cached(q, kb, vb, lam, subln_w, batch, lq, lk):
    return pl.pallas_call(
        functools.partial(_attn_cached_kernel, lq=lq, lk=lk),
        grid=(batch,),
        in_specs=[pl.BlockSpec((lq, D_MODEL), lambda b: (b, 0)),
                  pl.BlockSpec((lk, D_MODEL), lambda b: (b, 0)),
                  pl.BlockSpec((lk, D_MODEL), lambda b: (b, 0)),
                  pl.BlockSpec((1, 4), lambda b: (0, 0), memory_space=pltpu.SMEM),
                  pl.BlockSpec((1, HEAD_W), lambda b: (0, 0))],
        out_specs=pl.BlockSpec((lq, D_MODEL), lambda b: (b, 0)),
        out_shape=jax.ShapeDtypeStruct((batch * lq, D_MODEL), BF16),
        compiler_params=_cparams(("parallel",)),
    )(q, kb, vb, lam, subln_w.reshape(1, -1))


def _hgrn_kernel(qh_ref, fh_ref, ih_ref, gh_ref, lb_ref, nw_ref, s0_ref, o_ref, s_out_ref,
                 st_s, qin_s, qmid_s, kmid_s, kend_s, dec_s, *, ct, c):
    t = pl.program_id(1)
    nt = pl.num_programs(1)
    nc = ct // c

    @pl.when(t == 0)
    def _():
        for h in range(N_HEADS):
            st_s[h] = s0_ref[0, h].T

    a = lb_ref[...]
    amax = jnp.max(a, axis=0, keepdims=True)
    e = jnp.exp(a - amax)
    lb = e[0:1] / jnp.sum(e, axis=0, keepdims=True)

    row = lax.broadcasted_iota(I32, (c, c), 0)
    col = lax.broadcasted_iota(I32, (c, c), 1)
    causal = col <= row
    t_idx = lax.broadcasted_iota(I32, (c, D_MODEL), 0)

    def cumsum_rows(g):
        sh = 1
        while sh < c:
            g = g + jnp.where(t_idx >= sh, pltpu.roll(g, sh, axis=0), 0.0)
            sh *= 2
        return g

    for ci in range(nc):
        rows = slice(ci * c, (ci + 1) * c)
        f = lb + (1.0 - lb) * _sigmoid(fh_ref[rows, :])
        b = cumsum_rows(jnp.log2(f))
        b_last = b[c - 1:c, :]
        b_mid = b[c // 2 - 1:c // 2, :]
        q = _silu(qh_ref[rows, :].astype(F32)) * (HEAD_W ** -0.5)
        kk = 1.0 - f
        e_dn = jnp.exp2(b - b_mid)
        e_up = jnp.exp2(b_mid - b)
        qm = q * e_dn
        km = kk * e_up
        qmid_s[rows, :] = qm.astype(BF16)
        kmid_s[rows, :] = km.astype(BF16)
        qin_s[rows, :] = (qm * jnp.exp2(b_mid)).astype(BF16)
        kend_s[rows, :] = (km * jnp.exp2(b_last - b_mid)).astype(BF16)
        dec_s[ci:ci + 1, :] = jnp.exp2(b_last)

    nw = nw_ref[...]
    st = [st_s[h] for h in range(N_HEADS)]
    for ci in range(nc):
        rows = slice(ci * c, (ci + 1) * c)
        gate = _silu(gh_ref[rows, :].astype(F32))
        decay = dec_s[ci:ci + 1, :]
        for h in range(N_HEADS):
            sl = slice(h * HEAD_W, (h + 1) * HEAD_W)
            v = ih_ref[rows, sl]
            inter = lax.dot_general(qin_s[rows, sl], st[h].astype(BF16), (((1,), (1,)), ((), ())),
                                    preferred_element_type=F32)
            att = lax.dot_general(qmid_s[rows, sl], kmid_s[rows, sl], (((1,), (1,)), ((), ())),
                                  preferred_element_type=F32)
            att = jnp.where(causal, att, 0.0)
            o = inter + jnp.dot(att.astype(BF16), v, preferred_element_type=F32)
            upd = lax.dot_general(v, kend_s[rows, sl], (((0,), (0,)), ((), ())),
                                  preferred_element_type=F32)
            st[h] = decay[:, sl] * st[h] + upd
            o = o * lax.rsqrt(jnp.mean(o * o, axis=-1, keepdims=True) + EPS) * nw[:, sl]
            o_ref[rows, sl] = (o * gate[:, sl]).astype(o_ref.dtype)
    for h in range(N_HEADS):
        st_s[h] = st[h]

    @pl.when(t == nt - 1)
    def _():
        for h in range(N_HEADS):
            s_out_ref[0, h] = st_s[h].T


def _hgrn(qh, fh, ih, gh, hgrn_lb, norm_w, s0, batch, seq):
    ct = min(seq, 512)
    c = min(GLA_CHUNK, ct)
    nt = seq // ct
    kern = functools.partial(_hgrn_kernel, ct=ct, c=c)
    tok = pl.BlockSpec((ct, D_MODEL), lambda b, t: (b * nt + t, 0))
    st_spec = pl.BlockSpec((1, N_HEADS, HEAD_W, HEAD_W), lambda b, t: (b, 0, 0, 0))
    return pl.pallas_call(
        kern,
        grid=(batch, nt),
        in_specs=[tok, tok, tok, tok,
                  pl.BlockSpec((2, D_MODEL), lambda b, t: (0, 0)),
                  pl.BlockSpec((1, D_MODEL), lambda b, t: (0, 0)),
                  st_spec],
        out_specs=[tok, st_spec],
        out_shape=[jax.ShapeDtypeStruct((batch * seq, D_MODEL), BF16),
                   jax.ShapeDtypeStruct((batch, N_HEADS, HEAD_W, HEAD_W), F32)],
        scratch_shapes=[pltpu.VMEM((N_HEADS, HEAD_W, HEAD_W), F32)]
        + [pltpu.VMEM((ct, D_MODEL), BF16) for _ in range(4)]
        + [pltpu.VMEM((max(ct // c, 8), D_MODEL), F32)],
        compiler_params=_cparams(("parallel", "arbitrary")),
    )(qh, fh, ih, gh, hgrn_lb, jnp.tile(norm_w.reshape(1, -1), (1, N_HEADS)), s0)


def _route(h2, wrt_ref, brt_ref, carry_s, tm):
    nt_dims = (((1,), (1,)), ((), ()))
    w_hi = wrt_ref[0]
    w_lo = wrt_ref[1]
    h_hi = h2.astype(BF16)
    h_lo = (h2 - h_hi.astype(F32)).astype(BF16)
    logits = (lax.dot_general(w_hi, h_hi, nt_dims, preferred_element_type=F32)
              + lax.dot_general(w_hi, h_lo, nt_dims, preferred_element_type=F32)
              + lax.dot_general(w_lo, h_hi, nt_dims, preferred_element_type=F32))
    scores = _sigmoid(logits)
    biased = scores + brt_ref[...]
    big = float(2 * N_EXPERTS)

    x3 = biased.reshape(N_GROUPS, GROUP_W, tm)
    i3 = lax.broadcasted_iota(I32, x3.shape, 1).astype(F32)
    m1 = jnp.max(x3, axis=1, keepdims=True)
    i1 = jnp.min(jnp.where(x3 == m1, i3, big), axis=1, keepdims=True)
    m2 = jnp.max(jnp.where(i3 == i1, NEG, x3), axis=1, keepdims=True)
    gs = (m1 + m2).reshape(N_GROUPS, tm)

    g_iota = lax.broadcasted_iota(I32, (N_GROUPS, tm), 0).astype(F32)
    rem = gs
    gsel = jnp.zeros((N_GROUPS, tm), F32)
    for _ in range(TOPK_GROUPS):
        gm = jnp.max(rem, axis=0, keepdims=True)
        first = jnp.min(jnp.where(rem == gm, g_iota, big), axis=0, keepdims=True)
        sel = g_iota == first
        gsel = jnp.where(sel, 1.0, gsel)
        rem = jnp.where(sel, NEG, rem)
    keep = jnp.broadcast_to(gsel.reshape(N_GROUPS, 1, tm), x3.shape) > 0.5
    masked = jnp.where(keep, x3, NEG).reshape(N_EXPERTS, tm)

    e_iota = lax.broadcasted_iota(I32, (N_EXPERTS, tm), 0).astype(F32)
    onehot = jnp.zeros((N_EXPERTS, tm), F32)
    idxs, ws = [], []
    for _ in range(TOP_K):
        mk = jnp.max(masked, axis=0, keepdims=True)
        ik = jnp.min(jnp.where(masked == mk, e_iota, big), axis=0, keepdims=True)
        selk = e_iota == ik
        ws.append(jnp.sum(jnp.where(selk, scores, 0.0), axis=0, keepdims=True))
        idxs.append(ik)
        masked = jnp.where(selk, NEG, masked)
        onehot = jnp.where(selk, 1.0, onehot)
    wsum = ws[0]
    for k in range(1, TOP_K):
        wsum = wsum + ws[k]

    r = lax.broadcasted_iota(I32, (tm, tm), 0)
    cidx = lax.broadcasted_iota(I32, (tm, tm), 1)
    before = (r < cidx).astype(BF16)
    cum = jnp.dot(onehot.astype(BF16), before, preferred_element_type=F32) + carry_s[...]
    carry_s[...] = carry_s[...] + jnp.sum(onehot, axis=1, keepdims=True)

    k_iota = lax.broadcasted_iota(I32, (TOP_K, tm), 0)
    e_out = jnp.zeros((TOP_K, tm), F32)
    w_out = jnp.zeros((TOP_K, tm), F32)
    p_out = jnp.zeros((TOP_K, tm), F32)
    for k in range(TOP_K):
        pk = jnp.sum(jnp.where(e_iota == idxs[k], cum, 0.0), axis=0, keepdims=True)
        e_out = jnp.where(k_iota == k, idxs[k], e_out)
        w_out = jnp.where(k_iota == k, ws[k] / wsum * ROUTED_SCALE, w_out)
        p_out = jnp.where(k_iota == k, pk, p_out)
    return e_out, w_out, p_out


def _post_kernel(oa_ref, ob_ref, ga_ref, gb_ref, x_ref, g1_ref, sc2_ref, sh2_ref, g2_ref,
                 wpa_ref, wpb_ref, wo_ref, n2_ref, wrt_ref, brt_ref, wsg_ref, wsu_ref, wsd_ref, c0_ref,
                 base_ref, hp_ref, eidx_ref, wts_ref, pos_ref, cnt_ref, carry_s, *, tm):
    i = pl.program_id(0)

    @pl.when(i == 0)
    def _():
        carry_s[...] = c0_ref[...]

    pa = jnp.dot(oa_ref[...], wpa_ref[...], preferred_element_type=F32)
    pb = jnp.dot(ob_ref[...], wpb_ref[...], preferred_element_type=F32)
    u = _sigmoid(ga_ref[...].astype(F32)) * pa + _sigmoid(gb_ref[...].astype(F32)) * pb
    x1 = x_ref[...] + _mod_val(g1_ref) * jnp.dot(u.astype(BF16), wo_ref[...], preferred_element_type=F32)
    y = x1 * lax.rsqrt(jnp.mean(x1 * x1, axis=-1, keepdims=True) + EPS) * n2_ref[...]
    h2 = y * (1.0 + _mod_val(sc2_ref)) + _mod_val(sh2_ref)
    _store_packed(hp_ref, h2)

    hb = h2.astype(BF16)
    sg = jnp.dot(hb, wsg_ref[...], preferred_element_type=F32)
    su = jnp.dot(hb, wsu_ref[...], preferred_element_type=F32)
    shared = jnp.dot((_silu(sg) * su).astype(BF16), wsd_ref[...], preferred_element_type=F32)
    base_ref[...] = x1 + _mod_val(g2_ref) * shared

    e_out, w_out, p_out = _route(h2, wrt_ref, brt_ref, carry_s, tm)
    eidx_ref[...] = e_out.astype(I32)
    wts_ref[...] = w_out
    pos_ref[...] = p_out.astype(I32)
    cnt_ref[...] = carry_s[...]


def _post(oa, ob, ga, gb, x2, g1, sc2, sh2, g2, wpa, wpb, wo, n2, wr, br, wsg, wsu, wsd, count0, seq, tm):
    t = x2.shape[0]
    tok = lambda: pl.BlockSpec((tm, D_MODEL), lambda i: (i, 0))
    mod = lambda: _mod_spec(g1, seq, tm)
    full = lambda a: pl.BlockSpec(a.shape, lambda i: (0,) * a.ndim)
    n2 = n2.reshape(1, -1)
    wrt_hi = wr.T.astype(BF16)
    wrt = jnp.stack([wrt_hi, (wr.T - wrt_hi.astype(F32)).astype(BF16)])
    brt = br.reshape(-1, 1)
    k_out = lambda: pl.BlockSpec((TOP_K, tm), lambda i: (0, i))
    return pl.pallas_call(
        functools.partial(_post_kernel, tm=tm),
        grid=(t // tm,),
        in_specs=[tok(), tok(), tok(), tok(), tok(), mod(), mod(), mod(), mod(),
                  full(wpa), full(wpb), full(wo), full(n2), full(wrt), full(brt),
                  full(wsg), full(wsu), full(wsd), full(count0)],
        out_specs=[tok(), pl.BlockSpec((ROW_PARTS * tm, LANES), lambda i: (i, 0)),
                   k_out(), k_out(), k_out(),
                   pl.BlockSpec((N_EXPERTS, 1), lambda i: (0, 0))],
        out_shape=[jax.ShapeDtypeStruct((t, D_MODEL), F32),
                   jax.ShapeDtypeStruct((ROW_PARTS * t, LANES), U32),
                   jax.ShapeDtypeStruct((TOP_K, t), I32),
                   jax.ShapeDtypeStruct((TOP_K, t), F32),
                   jax.ShapeDtypeStruct((TOP_K, t), I32),
                   jax.ShapeDtypeStruct((N_EXPERTS, 1), F32)],
        scratch_shapes=[pltpu.VMEM((N_EXPERTS, 1), F32)],
        compiler_params=_cparams(("arbitrary",)),
    )(oa, ob, ga, gb, x2, g1, sc2, sh2, g2, wpa, wpb, wo, n2, wrt, brt, wsg, wsu, wsd, count0)


def _dest_kernel(eidx_ref, pos_ref, pstart_ref, o_ref, *, tm):
    e_iota = lax.broadcasted_iota(I32, (N_EXPERTS, tm), 0)
    k_iota = lax.broadcasted_iota(I32, (TOP_K, tm), 0)
    eidx = eidx_ref[...]
    start = jnp.zeros((TOP_K, tm), F32)
    for k in range(TOP_K):
        sel = e_iota == eidx[k:k + 1, :]
        sk = jnp.sum(jnp.where(sel, pstart_ref[...], 0.0), axis=0, keepdims=True)
        start = jnp.where(k_iota == k, sk, start)
    o_ref[...] = (start.astype(I32) + pos_ref[...]) * ROW_PARTS


def _dest(eidx, pos, pstart_col):
    t = eidx.shape[1]
    tm = min(t, 1024)
    blk = lambda: pl.BlockSpec((TOP_K, tm), lambda i: (0, i))
    return pl.pallas_call(
        functools.partial(_dest_kernel, tm=tm),
        grid=(t // tm,),
        in_specs=[blk(), blk(), pl.BlockSpec((N_EXPERTS, 1), lambda i: (0, 0))],
        out_specs=blk(),
        out_shape=jax.ShapeDtypeStruct((TOP_K, t), I32),
        compiler_params=_cparams(("parallel",)),
    )(eidx, pos, pstart_col)


def _zero_tail_kernel(last_ref, o_ref):
    del last_ref
    o_ref[...] = jnp.zeros(o_ref.shape, o_ref.dtype)


def _zero_tails(last_blk, nb):
    grid_spec = pltpu.PrefetchScalarGridSpec(
        num_scalar_prefetch=1,
        grid=(N_EXPERTS,),
        in_specs=[],
        out_specs=pl.BlockSpec((ROW_PARTS * ROW_BLOCK, LANES), lambda e, last: (last[e], 0)),
    )
    return pl.pallas_call(
        _zero_tail_kernel,
        grid_spec=grid_spec,
        out_shape=jax.ShapeDtypeStruct((ROW_PARTS * nb * ROW_BLOCK, LANES), U32),
        compiler_params=_cparams(("arbitrary",)),
    )(last_blk)


def _dispatch_kernel(dest_ref, hp_ref, xs_in_ref, xs_ref, sem, *, tm):
    del xs_in_ref

    def row_copy(t, d):
        return pltpu.make_async_copy(
            hp_ref.at[pl.ds(pl.multiple_of(t * ROW_PARTS, ROW_PARTS), ROW_PARTS), :],
            xs_ref.at[pl.ds(pl.multiple_of(d, ROW_PARTS), ROW_PARTS), :], sem)

    def issue(t, c):
        for k in range(TOP_K):
            row_copy(t, dest_ref[k, t]).start(priority=k % 2)
        return c

    lax.fori_loop(0, tm, issue, 0)
    for _ in range(TOP_K):
        pltpu.make_async_copy(hp_ref, xs_ref.at[pl.ds(0, ROW_PARTS * tm), :], sem).wait()


def _dispatch(hp, dest, xs, tm):
    t = dest.shape[1]
    return pl.pallas_call(
        functools.partial(_dispatch_kernel, tm=tm),
        grid=(t // tm,),
        in_specs=[pl.BlockSpec((TOP_K, tm), lambda i: (0, i), memory_space=pltpu.SMEM),
                  pl.BlockSpec((ROW_PARTS * tm, LANES), lambda i: (i, 0)),
                  pl.BlockSpec(memory_space=pl.ANY)],
        out_specs=pl.BlockSpec(memory_space=pl.ANY),
        out_shape=jax.ShapeDtypeStruct(xs.shape, xs.dtype),
        scratch_shapes=[pltpu.SemaphoreType.DMA],
        input_output_aliases={2: 0},
        compiler_params=_cparams(("arbitrary",)),
    )(dest, hp, xs)


def _expert_kernel(be_ref, nu_ref, xs_ref, wg_ref, wu_ref, wd_ref, ys_ref, wg_s, wu_s, wd_s):
    i = pl.program_id(0)
    used = i < nu_ref[0]

    @pl.when(jnp.logical_and(used, jnp.logical_or(i == 0, be_ref[i] != be_ref[jnp.maximum(i - 1, 0)])))
    def _():
        wg_s[...] = wg_ref[0].astype(BF16)
        wu_s[...] = wu_ref[0].astype(BF16)
        wd_s[...] = wd_ref[0].astype(BF16)

    @pl.when(used)
    def _():
        x = _load_packed(xs_ref, ROW_BLOCK).astype(BF16)
        g = jnp.dot(x, wg_s[...], preferred_element_type=F32)
        u = jnp.dot(x, wu_s[...], preferred_element_type=F32)
        hmid = (_silu(g) * u).astype(BF16)
        _store_packed(ys_ref, jnp.dot(hmid, wd_s[...], preferred_element_type=F32))

    @pl.when(jnp.logical_not(used))
    def _():
        ys_ref[...] = jnp.zeros(ys_ref.shape, ys_ref.dtype)


def _experts(xs, blk_e, n_used, w_eg, w_eu, w_ed):
    nb = xs.shape[0] // (ROW_PARTS * ROW_BLOCK)
    rows_spec = lambda: pl.BlockSpec((ROW_PARTS * ROW_BLOCK, LANES), lambda i, be, nu: (i, 0))
    used_rows_spec = pl.BlockSpec((ROW_PARTS * ROW_BLOCK, LANES),
                                  lambda i, be, nu: (jnp.maximum(jnp.minimum(i, nu[0] - 1), 0), 0))
    w_up = lambda: pl.BlockSpec((1, D_MODEL, D_EXPERT), lambda i, be, nu: (be[i], 0, 0))
    grid_spec = pltpu.PrefetchScalarGridSpec(
        num_scalar_prefetch=2,
        grid=(nb,),
        in_specs=[used_rows_spec, w_up(), w_up(),
                  pl.BlockSpec((1, D_EXPERT, D_MODEL), lambda i, be, nu: (be[i], 0, 0))],
        out_specs=rows_spec(),
        scratch_shapes=[pltpu.VMEM((D_MODEL, D_EXPERT), BF16), pltpu.VMEM((D_MODEL, D_EXPERT), BF16),
                        pltpu.VMEM((D_EXPERT, D_MODEL), BF16)],
    )
    return pl.pallas_call(
        _expert_kernel,
        grid_spec=grid_spec,
        out_shape=jax.ShapeDtypeStruct(xs.shape, U32),
        compiler_params=_cparams(("arbitrary",)),
    )(blk_e, n_used, xs, w_eg, w_eu, w_ed)


def _combine_kernel(dest_ref, base_ref, wts_ref, g2_ref, ys_ref, o_ref, buf, sem, *, tm):
    def row_copy(d, k, t):
        return pltpu.make_async_copy(
            ys_ref.at[pl.ds(pl.multiple_of(d, ROW_PARTS), ROW_PARTS), :],
            buf.at[k, pl.ds(pl.multiple_of(t * ROW_PARTS, ROW_PARTS), ROW_PARTS), :], sem)

    def issue(t, c):
        for k in range(TOP_K):
            row_copy(dest_ref[k, t], k, t).start(priority=k % 2)
        return c

    lax.fori_loop(0, tm, issue, 0)
    for k in range(TOP_K):
        pltpu.make_async_copy(ys_ref.at[pl.ds(0, ROW_PARTS * tm), :], buf.at[k], sem).wait()

    w = wts_ref[...]
    y = jnp.zeros((tm, D_MODEL), F32)
    for k in range(TOP_K):
        y = y + w[:, k:k + 1] * _load_packed(buf.at[k], tm)
    o_ref[...] = base_ref[...] + _mod_val(g2_ref) * y


def _combine(base, wts_t, g2, ys, dest, seq, tm):
    t = base.shape[0]
    return pl.pallas_call(
        functools.partial(_combine_kernel, tm=tm),
        grid=(t // tm,),
        in_specs=[pl.BlockSpec((TOP_K, tm), lambda i: (0, i), memory_space=pltpu.SMEM),
                  pl.BlockSpec((tm, D_MODEL), lambda i: (i, 0)),
                  pl.BlockSpec((tm, TOP_K), lambda i: (i, 0)),
                  _mod_spec(g2, seq, tm),
                  pl.BlockSpec(memory_space=pl.ANY)],
        out_specs=pl.BlockSpec((tm, D_MODEL), lambda i: (i, 0)),
        out_shape=jax.ShapeDtypeStruct((t, D_MODEL), F32),
        scratch_shapes=[pltpu.VMEM((TOP_K, ROW_PARTS * tm, LANES), U32), pltpu.SemaphoreType.DMA],
        compiler_params=_cparams(("arbitrary",)),
    )(dest, base, wts_t, g2, ys)


def _mix(x, mods, past_k, past_v, s0, count0, p):
    batch, seq, _ = x.shape
    t = batch * seq
    x2 = x.reshape(t, D_MODEL)
    if seq % 256 == 0:
        tile = lambda cap: min(seq, cap)
    else:
        mods = [jnp.broadcast_to(m, (batch, seq, D_MODEL)).reshape(t, D_MODEL) for m in mods]
        tile = lambda cap: min(t, cap)
    shift1, scale1, gate1, shift2, scale2, gate2 = mods

    h = _norm_mod(x2, p["norm1_w"], scale1, shift1, seq, tile(512))
    qn = jnp.tile(p["q_norm_w"].reshape(1, -1), (1, D_MODEL // HEAD_DIM_A))
    kn = jnp.tile(p["k_norm_w"].reshape(1, -1), (1, D_MODEL // HEAD_DIM_A))
    w_in = p["w_in"]
    (q,) = _proj(h, w_in, 0, [BF16], qn, out_scale=(HEAD_DIM_A ** -0.5 * LOG2E,))
    k, kb = _proj(h, w_in, 1, [F32, BF16], kn, split_first=True, tm=512)
    v, vb = _proj(h, w_in, 2, [F32, BF16])
    (qh,) = _proj(h, w_in, 3, [BF16])
    (fh,) = _proj(h, w_in, 4, [F32])
    (ih,) = _proj(h, w_in, 5, [BF16])
    (gh,) = _proj(h, w_in, 6, [BF16])
    (ga,) = _proj(h, w_in, 7, [BF16])
    (gb,) = _proj(h, w_in, 8, [BF16])

    if past_k is None:
        o_a = _attention_prompt(q, kb, vb, p["lam"], p["subln_w"], batch, seq)
    else:
        past = past_k.shape[1]
        lk = past + seq
        kb = jnp.concatenate([past_k.reshape(batch, past, D_MODEL).astype(BF16),
                              kb.reshape(batch, seq, D_MODEL)], axis=1).reshape(batch * lk, D_MODEL)
        vb = jnp.concatenate([past_v.reshape(batch, past, D_MODEL).astype(BF16),
                              vb.reshape(batch, seq, D_MODEL)], axis=1).reshape(batch * lk, D_MODEL)
        o_a = _attention_cached(q, kb, vb, p["lam"], p["subln_w"], batch, seq, lk)
    o_b, s_new = _hgrn(qh, fh, ih, gh, p["hgrn_lb"], p["hgrn_norm_w"], s0, batch, seq)

    base, hp, eidx, wts, pos, counts = _post(
        o_a, o_b, ga, gb, x2, gate1, scale2, shift2, gate2,
        p["w_proj_a"], p["w_proj_b"], p["w_out"], p["norm2_w"], p["w_router"], p["b_router"],
        p["w_sh_gate"], p["w_sh_up"], p["w_sh_down"], count0, seq, tile(512))
    return dict(base=base, hp=hp, eidx=eidx, wts=wts, pos=pos, counts=counts, k=k, v=v, s=s_new,
                gate2=gate2, seq=seq, tm=tile(512), shape=x.shape)


def kernel(x_prompt, x_sample, cache_attn_k, cache_attn_v, state_hgrn, c_prompt, c_sample, w_ada, b_ada, norm1_w, norm2_w, w_in, q_norm_w, k_norm_w, lambda_q1, lambda_k1, lambda_q2, lambda_k2, subln_w, hgrn_lb, hgrn_norm_w, w_proj_a, w_proj_b, w_out, w_router, b_router, w_exp_gate, w_exp_up, w_exp_down, w_sh_gate, w_sh_up, w_sh_down):
    bp, lp, _ = x_prompt.shape
    bs, ls, _ = x_sample.shape
    l = 0
    lam = (jnp.exp(jnp.sum(lambda_q1[l] * lambda_k1[l])) - jnp.exp(jnp.sum(lambda_q2[l] * lambda_k2[l]))
           + LAMBDA_INIT).astype(F32)
    row_norm = lambda w, s: math.sqrt(HEAD_DIM_A) * jnp.max(jnp.abs(w.astype(F32))) * (s * 1.01)
    knorm = row_norm(k_norm_w[l], 1.0)
    qnorm = row_norm(q_norm_w[l], HEAD_DIM_A ** -0.5 * LOG2E)
    s_bound = qnorm * knorm
    use_bound = (2.0 * s_bound < ATTN_BOUND_MAX_GAP).astype(F32)
    lam = jnp.stack([lam, s_bound, use_bound, jnp.zeros((), F32)]).reshape(1, 4)
    p = dict(
        norm1_w=norm1_w[l], norm2_w=norm2_w[l], w_in=w_in[l].astype(BF16),
        q_norm_w=q_norm_w[l], k_norm_w=k_norm_w[l], lam=lam, subln_w=subln_w[l],
        hgrn_lb=hgrn_lb, hgrn_norm_w=hgrn_norm_w[l],
        w_proj_a=w_proj_a[l].astype(BF16), w_proj_b=w_proj_b[l].astype(BF16), w_out=w_out[l].astype(BF16),
        w_router=w_router[l], b_router=b_router[l],
        w_sh_gate=w_sh_gate[l].astype(BF16), w_sh_up=w_sh_up[l].astype(BF16), w_sh_down=w_sh_down[l].astype(BF16),
    )
    mod = _ada(jnp.concatenate([c_prompt, c_sample], axis=0), w_ada[l], b_ada[l])
    mod = mod.reshape(bp + bs, 6, 1, D_MODEL)
    mods_p = [mod[:bp, j] for j in range(6)]
    mods_s = [mod[bp:, j] for j in range(6)]

    zero_state = jnp.zeros((bp, N_HEADS, HEAD_W, HEAD_W), F32)
    zero_count = jnp.zeros((N_EXPERTS, 1), F32)
    gp = _mix(x_prompt, mods_p, None, None, zero_state, zero_count, p)
    gs = _mix(x_sample, mods_s, cache_attn_k[l], cache_attn_v[l], state_hgrn[l], gp["counts"], p)
    groups = (gp, gs)

    counts = gs["counts"].reshape(N_EXPERTS).astype(I32)
    pcounts = (counts + ROW_BLOCK - 1) // ROW_BLOCK * ROW_BLOCK
    pend = jnp.cumsum(pcounts)
    pstart = pend - pcounts
    n_assign = (bp * lp + bs * ls) * TOP_K
    nb = -(-(n_assign + N_EXPERTS * (ROW_BLOCK - 1)) // ROW_BLOCK)
    blk_start = jnp.arange(nb, dtype=I32)[:, None] * ROW_BLOCK
    blk_e = jnp.minimum(jnp.sum((pend[None, :] <= blk_start).astype(I32), axis=1), N_EXPERTS - 1)
    n_used = (pend[-1] // ROW_BLOCK).astype(I32).reshape(1)
    pstart_col = pstart.astype(F32).reshape(N_EXPERTS, 1)

    last_blk = jnp.maximum(pend // ROW_BLOCK - 1, 0).astype(I32)
    xs = _zero_tails(last_blk, nb)
    for g in groups:
        g["dest"] = _dest(g["eidx"], g["pos"], pstart_col)
        xs = _dispatch(g["hp"], g["dest"], xs, g["tm"])
    ys = _experts(xs, blk_e, n_used, w_exp_gate[l], w_exp_up[l], w_exp_down[l])
    yp, ysm = [_combine(g["base"], g["wts"].T, g["gate2"], ys, g["dest"], g["seq"], g["tm"]).reshape(g["shape"])
               for g in groups]

    return (yp, ysm,
            gp["k"].reshape(1, bp, lp, N_HEADS, 2, HEAD_DIM_A), gp["v"].reshape(1, bp, lp, N_HEADS, HEAD_W),
            gp["s"][None],
            gs["k"].reshape(1, bs, ls, N_HEADS, 2, HEAD_DIM_A), gs["v"].reshape(1, bs, ls, N_HEADS, HEAD_W),
            gs["s"][None])
```

```python
import functools
import math

import jax
import jax.numpy as jnp
from jax import lax
from jax.experimental import pallas as pl
from jax.experimental.pallas import tpu as pltpu

F32 = jnp.float32
BF16 = jnp.bfloat16
U32 = jnp.uint32
I32 = jnp.int32

D_MODEL = 1024
N_HEADS = 8
HEAD_W = 128
HEAD_DIM_A = 64
CHUNK = 64
N_EXPERTS = 256
TOP_K = 8
N_GROUPS = 8
GROUP_W = N_EXPERTS // N_GROUPS
TOPK_GROUPS = 4
D_EXPERT = 256
ROUTED_SCALE = 2.5
EPS = 1e-6
LAMBDA_INIT = 0.8 - 0.6 * math.exp(-0.3 * 0)
LANES = 128
NEG = -1e30
LOG2E = math.log2(math.e)

ATTN_BOUND_MAX_GAP = 100.0
GLA_CHUNK = 32
ROW_BLOCK = 512
VMEM_LIMIT = 48 * 1024 * 1024


def _sigmoid(x):
    return 0.5 * jnp.tanh(0.5 * x) + 0.5


def _silu(x):
    return x * _sigmoid(x)


def _cparams(sem):
    return pltpu.CompilerParams(dimension_semantics=sem, vmem_limit_bytes=VMEM_LIMIT)


ROW_PARTS = 4


def _store_packed(ref, x, row0=0):
    m, half = x.shape[0], x.shape[1] // 2
    lo = pltpu.bitcast(x[:, :half].astype(BF16).astype(F32), U32)
    hi = pltpu.bitcast(x[:, half:].astype(BF16).astype(F32), U32)
    w = (hi & jnp.uint32(0xFFFF0000)) | (lo >> 16)
    for j in range(ROW_PARTS):
        ref[pl.ds(ROW_PARTS * row0 + j, m, stride=ROW_PARTS), :] = w[:, j * LANES:(j + 1) * LANES]


def _load_packed(ref, m, row0=0):
    parts = [ref[pl.ds(ROW_PARTS * row0 + j, m, stride=ROW_PARTS), :] for j in range(ROW_PARTS)]
    lo = [pltpu.bitcast(w << 16, F32) for w in parts]
    hi = [pltpu.bitcast(w & jnp.uint32(0xFFFF0000), F32) for w in parts]
    return jnp.concatenate(lo + hi, axis=1)


def _mod_spec(a, seq, tm):
    if a.ndim == 3:
        per_b = seq // tm
        return pl.BlockSpec((1, 1, D_MODEL), lambda i: (i // per_b, 0, 0))
    return pl.BlockSpec((tm, D_MODEL), lambda i: (i, 0))


def _mod_val(ref):
    v = ref[...]
    return v.reshape(v.shape[-2], v.shape[-1])


def _ada_kernel(c_ref, w_ref, b_ref, o_ref):
    c = _silu(c_ref[...])
    o_ref[...] = jnp.dot(c, w_ref[...], precision=lax.Precision.HIGHEST,
                         preferred_element_type=F32) + b_ref[...]


def _ada(c, w_ada, b_ada):
    n = c.shape[0]
    nj = w_ada.shape[1] // D_MODEL
    return pl.pallas_call(
        _ada_kernel,
        grid=(nj,),
        in_specs=[pl.BlockSpec((n, D_MODEL), lambda j: (0, 0)),
                  pl.BlockSpec((D_MODEL, D_MODEL), lambda j: (0, j)),
                  pl.BlockSpec((1, D_MODEL), lambda j: (0, j))],
        out_specs=pl.BlockSpec((n, D_MODEL), lambda j: (0, j)),
        out_shape=jax.ShapeDtypeStruct((n, w_ada.shape[1]), F32),
        compiler_params=_cparams(("arbitrary",)),
    )(c, w_ada, b_ada.reshape(1, -1))


def _norm_mod_kernel(x_ref, w_ref, sc_ref, sh_ref, o_ref):
    x = x_ref[...]
    y = x * lax.rsqrt(jnp.mean(x * x, axis=-1, keepdims=True) + EPS) * w_ref[...]
    o_ref[...] = (y * (1.0 + _mod_val(sc_ref)) + _mod_val(sh_ref)).astype(o_ref.dtype)


def _norm_mod(x2, w, scale, shift, seq, tm):
    t = x2.shape[0]
    return pl.pallas_call(
        _norm_mod_kernel,
        grid=(t // tm,),
        in_specs=[pl.BlockSpec((tm, D_MODEL), lambda i: (i, 0)),
                  pl.BlockSpec((1, D_MODEL), lambda i: (0, 0)),
                  _mod_spec(scale, seq, tm), _mod_spec(shift, seq, tm)],
        out_specs=pl.BlockSpec((tm, D_MODEL), lambda i: (i, 0)),
        out_shape=jax.ShapeDtypeStruct((t, D_MODEL), BF16),
        compiler_params=_cparams(("parallel",)),
    )(x2, w.reshape(1, -1), scale, shift)


def _head_norm(acc, w):
    r = lax.broadcasted_iota(I32, (HEAD_W, HEAD_W), 0) // HEAD_DIM_A
    c = lax.broadcasted_iota(I32, (HEAD_W, HEAD_W), 1) // HEAD_DIM_A
    ones = (r == c).astype(BF16)
    outs = []
    for h in range(N_HEADS):
        a = acc[:, h * HEAD_W:(h + 1) * HEAD_W]
        sq = a * a
        hi = sq.astype(BF16)
        lo = (sq - hi.astype(F32)).astype(BF16)
        ss = (jnp.dot(hi, ones, preferred_element_type=F32)
              + jnp.dot(lo, ones, preferred_element_type=F32))
        outs.append(a * lax.rsqrt(ss * (1.0 / HEAD_DIM_A) + EPS) * w[:, h * HEAD_W:(h + 1) * HEAD_W])
    return jnp.concatenate(outs, axis=1)


N_MAPS = D_MODEL // HEAD_DIM_A


def _proj_kernel(h_ref, w_ref, nw_ref, *o_refs, head_norm, out_scale, split_first):
    acc = jnp.dot(h_ref[...], w_ref[...], preferred_element_type=F32)
    if head_norm:
        acc = _head_norm(acc, nw_ref[...])
    for n, (o_ref, s) in enumerate(zip(o_refs, out_scale)):
        val = (acc if s == 1.0 else acc * s).astype(o_ref.dtype)
        if split_first and n == 0:
            for g in range(N_MAPS):
                o_ref[pl.ds(g, val.shape[0], stride=N_MAPS), :] = val[:, g * HEAD_DIM_A:(g + 1) * HEAD_DIM_A]
        else:
            o_ref[...] = val


def _proj(h, w_in_bf, group, out_dtypes, norm_w=None, out_scale=None, split_first=False, tm=1024):
    t = h.shape[0]
    tm = min(tm, t)
    nw = jnp.ones((1, D_MODEL), F32) if norm_w is None else norm_w
    out_scale = tuple(out_scale or (1.0,) * len(out_dtypes))
    specs = [pl.BlockSpec((tm, D_MODEL), lambda i: (i, 0)) for _ in out_dtypes]
    shapes = [jax.ShapeDtypeStruct((t, D_MODEL), dt) for dt in out_dtypes]
    if split_first:
        specs[0] = pl.BlockSpec((N_MAPS * tm, HEAD_DIM_A), lambda i: (i, 0))
        shapes[0] = jax.ShapeDtypeStruct((N_MAPS * t, HEAD_DIM_A), out_dtypes[0])
    outs = pl.pallas_call(
        functools.partial(_proj_kernel, head_norm=norm_w is not None, out_scale=out_scale,
                          split_first=split_first),
        grid=(t // tm,),
        in_specs=[pl.BlockSpec((tm, D_MODEL), lambda i: (i, 0)),
                  pl.BlockSpec((D_MODEL, D_MODEL), lambda i: (0, group)),
                  pl.BlockSpec((1, D_MODEL), lambda i: (0, 0))],
        out_specs=specs,
        out_shape=shapes,
        compiler_params=_cparams(("parallel",)),
    )(h, w_in_bf, nw)
    return outs


def _split_maps(q):
    lane = lax.broadcasted_iota(I32, q.shape, 1)
    zero = jnp.zeros_like(q)
    return jnp.where(lane < HEAD_DIM_A, q, zero), jnp.where(lane >= HEAD_DIM_A, q, zero)


def _finish_head(acc1, l1, acc2, l2, lam, sw):
    o = acc1 * (1.0 / l1) - acc2 * (lam / l2)
    o = o * lax.rsqrt(jnp.mean(o * o, axis=-1, keepdims=True) + EPS) * sw
    return o * (1.0 - LAMBDA_INIT)


def _attn_kernel(q_ref, k_ref, v_ref, lam_ref, sw_ref, o_ref, mx_s, mrep_s, ls_s, acc_s,
                 *, tq, tk, hp):
    qi = pl.program_id(2)
    nf = tk // LANES
    qs = []
    for hh in range(hp):
        qs.extend(_split_maps(q_ref[:, hh * HEAD_W:(hh + 1) * HEAD_W]))
    ns = 2 * hp

    r_chunk = lax.broadcasted_iota(I32, (tq, tk), 0) // CHUNK
    c_chunk = lax.broadcasted_iota(I32, (tq, tk), 1) // CHUNK
    diag_ok = c_chunk <= r_chunk

    def scores(kt, s, masked):
        k = k_ref[pl.ds(pl.multiple_of(kt * tk, tk), tk), (s // 2) * HEAD_W:(s // 2 + 1) * HEAD_W]
        sc = lax.dot_general(qs[s], k, (((1,), (1,)), ((), ())), preferred_element_type=F32)
        return jnp.where(diag_ok, sc, NEG) if masked else sc

    def fold(x, op):
        r = x[:, :LANES]
        for j in range(1, nf):
            r = op(r, x[:, j * LANES:(j + 1) * LANES])
        return r

    def pass1(kt, first):
        for s in range(ns):
            m = fold(scores(kt, s, first), jnp.maximum)
            mx_s[s] = m if first else jnp.maximum(mx_s[s], m)

    def pass2(kt, first):
        for s in range(ns):
            m = mrep_s[s]
            p = jnp.exp2(scores(kt, s, first) - jnp.concatenate([m] * nf, axis=1))
            v = v_ref[pl.ds(pl.multiple_of(kt * tk, tk), tk), (s // 2) * HEAD_W:(s // 2 + 1) * HEAD_W]
            pv = jnp.dot(p.astype(BF16), v, preferred_element_type=F32)
            ls_s[s] = fold(p, jnp.add) if first else ls_s[s] + fold(p, jnp.add)
            acc_s[s] = pv if first else acc_s[s] + pv

    def loop(fn):
        def body(kt, c):
            fn(kt, False)
            return c

        fn(qi, True)
        lax.fori_loop(0, qi, body, 0)

    use_bound = lam_ref[0, 2] > 0.5

    @pl.when(use_bound)
    def _():
        mrep_s[...] = jnp.full(mrep_s.shape, lam_ref[0, 1], F32)

    @pl.when(jnp.logical_not(use_bound))
    def _():
        loop(pass1)
        for s in range(ns):
            mrep_s[s] = jnp.broadcast_to(jnp.max(mx_s[s], axis=-1, keepdims=True), (tq, LANES))

    loop(pass2)

    lam = lam_ref[0, 0]
    ones = jnp.ones((LANES, LANES), BF16)

    def lane_sum(x):
        hi = x.astype(BF16)
        lo = (x - hi.astype(F32)).astype(BF16)
        return (jnp.dot(hi, ones, preferred_element_type=F32)
                + jnp.dot(lo, ones, preferred_element_type=F32))

    for hh in range(hp):
        o = _finish_head(acc_s[2 * hh], lane_sum(ls_s[2 * hh]), acc_s[2 * hh + 1], lane_sum(ls_s[2 * hh + 1]),
                         lam, sw_ref[...])
        o_ref[:, hh * HEAD_W:(hh + 1) * HEAD_W] = o.astype(o_ref.dtype)


def _attention_prompt(q, kb, vb, lam, subln_w, batch, seq):
    tq = tk = min(seq, 256)
    hp = 8
    nq = seq // tq
    kern = functools.partial(_attn_kernel, tq=tq, tk=tk, hp=hp)
    w = hp * HEAD_W
    return pl.pallas_call(
        kern,
        grid=(batch, N_HEADS // hp, nq),
        in_specs=[pl.BlockSpec((tq, w), lambda b, h, i: (b * nq + i, h)),
                  pl.BlockSpec((seq, w), lambda b, h, i: (b, h)),
                  pl.BlockSpec((seq, w), lambda b, h, i: (b, h)),
                  pl.BlockSpec((1, 4), lambda b, h, i: (0, 0), memory_space=pltpu.SMEM),
                  pl.BlockSpec((1, HEAD_W), lambda b, h, i: (0, 0))],
        out_specs=pl.BlockSpec((tq, w), lambda b, h, i: (b * nq + i, h)),
        out_shape=jax.ShapeDtypeStruct((batch * seq, D_MODEL), BF16),
        scratch_shapes=[pltpu.VMEM((2 * hp, tq, LANES), F32) for _ in range(4)],
        compiler_params=_cparams(("parallel", "parallel", "arbitrary")),
    )(q, kb, vb, lam, subln_w.reshape(1, -1))


def _attn_cached_kernel(q_ref, k_ref, v_ref, lam_ref, sw_ref, o_ref, *, lq, lk):
    qpos = (lk - lq) + lax.broadcasted_iota(I32, (lq, lk), 0)
    kpos = lax.broadcasted_iota(I32, (lq, lk), 1)
    allowed = (kpos // CHUNK) <= (qpos // CHUNK)
    lam = lam_ref[0, 0]
    for h in range(N_HEADS):
        sl = slice(h * HEAD_W, (h + 1) * HEAD_W)
        k = k_ref[:, sl]
        v = v_ref[:, sl]
        accs, ls = [], []
        for qm in _split_maps(q_ref[:, sl]):
            sc = lax.dot_general(qm, k, (((1,), (1,)), ((), ())), preferred_element_type=F32)
            sc = jnp.where(allowed, sc, NEG)
            p = jnp.exp2(sc - jnp.max(sc, axis=-1, keepdims=True))
            ls.append(jnp.sum(p, axis=-1, keepdims=True))
            accs.append(jnp.dot(p.astype(BF16), v, preferred_element_type=F32))
        o = _finish_head(accs[0], ls[0], accs[1], ls[1], lam, sw_ref[...])
        o_ref[:, sl] = o.astype(o_ref.dtype)


def _attention_cached(q, kb, vb, lam, subln_w, batch, lq, lk):
    return pl.pallas_call(
        functools.partial(_attn_cached_kernel, lq=lq, lk=lk),
        grid=(batch,),
        in_specs=[pl.BlockSpec((lq, D_MODEL), lambda b: (b, 0)),
                  pl.BlockSpec((lk, D_MODEL), lambda b: (b, 0)),
                  pl.BlockSpec((lk, D_MODEL), lambda b: (b, 0)),
                  pl.BlockSpec((1, 4), lambda b: (0, 0), memory_space=pltpu.SMEM),
                  pl.BlockSpec((1, HEAD_W), lambda b: (0, 0))],
        out_specs=pl.BlockSpec((lq, D_MODEL), lambda b: (b, 0)),
        out_shape=jax.ShapeDtypeStruct((batch * lq, D_MODEL), BF16),
        compiler_params=_cparams(("parallel",)),
    )(q, kb, vb, lam, subln_w.reshape(1, -1))


def _hgrn_kernel(qh_ref, fh_ref, ih_ref, gh_ref, lb_ref, nw_ref, s0_ref, o_ref, s_out_ref,
                 st_s, qin_s, qmid_s, kmid_s, kend_s, dec_s, *, ct, c):
    t = pl.program_id(1)
    nt = pl.num_programs(1)
    nc = ct // c

    @pl.when(t == 0)
    def _():
        for h in range(N_HEADS):
            st_s[h] = s0_ref[0, h].T

    a = lb_ref[...]
    amax = jnp.max(a, axis=0, keepdims=True)
    e = jnp.exp(a - amax)
    lb = e[0:1] / jnp.sum(e, axis=0, keepdims=True)

    row = lax.broadcasted_iota(I32, (c, c), 0)
    col = lax.broadcasted_iota(I32, (c, c), 1)
    causal = col <= row
    t_idx = lax.broadcasted_iota(I32, (c, D_MODEL), 0)

    def cumsum_rows(g):
        sh = 1
        while sh < c:
            g = g + jnp.where(t_idx >= sh, pltpu.roll(g, sh, axis=0), 0.0)
            sh *= 2
        return g

    for ci in range(nc):
        rows = slice(ci * c, (ci + 1) * c)
        f = lb + (1.0 - lb) * _sigmoid(fh_ref[rows, :])
        b = cumsum_rows(jnp.log2(f))
        b_last = b[c - 1:c, :]
        b_mid = b[c // 2 - 1:c // 2, :]
        q = _silu(qh_ref[rows, :].astype(F32)) * (HEAD_W ** -0.5)
        kk = 1.0 - f
        e_dn = jnp.exp2(b - b_mid)
        e_up = jnp.exp2(b_mid - b)
        qm = q * e_dn
        km = kk * e_up
        qmid_s[rows, :] = qm.astype(BF16)
        kmid_s[rows, :] = km.astype(BF16)
        qin_s[rows, :] = (qm * jnp.exp2(b_mid)).astype(BF16)
        kend_s[rows, :] = (km * jnp.exp2(b_last - b_mid)).astype(BF16)
        dec_s[ci:ci + 1, :] = jnp.exp2(b_last)

    nw = nw_ref[...]
    st = [st_s[h] for h in range(N_HEADS)]
    for ci in range(nc):
        rows = slice(ci * c, (ci + 1) * c)
        gate = _silu(gh_ref[rows, :].astype(F32))
        decay = dec_s[ci:ci + 1, :]
        for h in range(N_HEADS):
            sl = slice(h * HEAD_W, (h + 1) * HEAD_W)
            v = ih_ref[rows, sl]
            inter = lax.dot_general(qin_s[rows, sl], st[h].astype(BF16), (((1,), (1,)), ((), ())),
                                    preferred_element_type=F32)
            att = lax.dot_general(qmid_s[rows, sl], kmid_s[rows, sl], (((1,), (1,)), ((), ())),
                                  preferred_element_type=F32)
            att = jnp.where(causal, att, 0.0)
            o = inter + jnp.dot(att.astype(BF16), v, preferred_element_type=F32)
            upd = lax.dot_general(v, kend_s[rows, sl], (((0,), (0,)), ((), ())),
                                  preferred_element_type=F32)
            st[h] = decay[:, sl] * st[h] + upd
            o = o * lax.rsqrt(jnp.mean(o * o, axis=-1, keepdims=True) + EPS) * nw[:, sl]
            o_ref[rows, sl] = (o * gate[:, sl]).astype(o_ref.dtype)
    for h in range(N_HEADS):
        st_s[h] = st[h]

    @pl.when(t == nt - 1)
    def _():
        for h in range(N_HEADS):
            s_out_ref[0, h] = st_s[h].T


def _hgrn(qh, fh, ih, gh, hgrn_lb, norm_w, s0, batch, seq):
    ct = min(seq, 512)
    c = min(GLA_CHUNK, ct)
    nt = seq // ct
    kern = functools.partial(_hgrn_kernel, ct=ct, c=c)
    tok = pl.BlockSpec((ct, D_MODEL), lambda b, t: (b * nt + t, 0))
    st_spec = pl.BlockSpec((1, N_HEADS, HEAD_W, HEAD_W), lambda b, t: (b, 0, 0, 0))
    return pl.pallas_call(
        kern,
        grid=(batch, nt),
        in_specs=[tok, tok, tok, tok,
                  pl.BlockSpec((2, D_MODEL), lambda b, t: (0, 0)),
                  pl.BlockSpec((1, D_MODEL), lambda b, t: (0, 0)),
                  st_spec],
        out_specs=[tok, st_spec],
        out_shape=[jax.ShapeDtypeStruct((batch * seq, D_MODEL), BF16),
                   jax.ShapeDtypeStruct((batch, N_HEADS, HEAD_W, HEAD_W), F32)],
        scratch_shapes=[pltpu.VMEM((N_HEADS, HEAD_W, HEAD_W), F32)]
        + [pltpu.VMEM((ct, D_MODEL), BF16) for _ in range(4)]
        + [pltpu.VMEM((max(ct // c, 8), D_MODEL), F32)],
        compiler_params=_cparams(("parallel", "arbitrary")),
    )(qh, fh, ih, gh, hgrn_lb, jnp.tile(norm_w.reshape(1, -1), (1, N_HEADS)), s0)


def _route(h2, wrt_ref, brt_ref, carry_s, tm):
    nt_dims = (((1,), (1,)), ((), ()))
    w_hi = wrt_ref[0]
    w_lo = wrt_ref[1]
    h_hi = h2.astype(BF16)
    h_lo = (h2 - h_hi.astype(F32)).astype(BF16)
    logits = (lax.dot_general(w_hi, h_hi, nt_dims, preferred_element_type=F32)
              + lax.dot_general(w_hi, h_lo, nt_dims, preferred_element_type=F32)
              + lax.dot_general(w_lo, h_hi, nt_dims, preferred_element_type=F32))
    scores = _sigmoid(logits)
    biased = scores + brt_ref[...]
    big = float(2 * N_EXPERTS)

    x3 = biased.reshape(N_GROUPS, GROUP_W, tm)
    i3 = lax.broadcasted_iota(I32, x3.shape, 1).astype(F32)
    m1 = jnp.max(x3, axis=1, keepdims=True)
    i1 = jnp.min(jnp.where(x3 == m1, i3, big), axis=1, keepdims=True)
    m2 = jnp.max(jnp.where(i3 == i1, NEG, x3), axis=1, keepdims=True)
    gs = (m1 + m2).reshape(N_GROUPS, tm)

    g_iota = lax.broadcasted_iota(I32, (N_GROUPS, tm), 0).astype(F32)
    rem = gs
    gsel = jnp.zeros((N_GROUPS, tm), F32)
    for _ in range(TOPK_GROUPS):
        gm = jnp.max(rem, axis=0, keepdims=True)
        first = jnp.min(jnp.where(rem == gm, g_iota, big), axis=0, keepdims=True)
        sel = g_iota == first
        gsel = jnp.where(sel, 1.0, gsel)
        rem = jnp.where(sel, NEG, rem)
    keep = jnp.broadcast_to(gsel.reshape(N_GROUPS, 1, tm), x3.shape) > 0.5
    masked = jnp.where(keep, x3, NEG).reshape(N_EXPERTS, tm)

    e_iota = lax.broadcasted_iota(I32, (N_EXPERTS, tm), 0).astype(F32)
    onehot = jnp.zeros((N_EXPERTS, tm), F32)
    idxs, ws = [], []
    for _ in range(TOP_K):
        mk = jnp.max(masked, axis=0, keepdims=True)
        ik = jnp.min(jnp.where(masked == mk, e_iota, big), axis=0, keepdims=True)
        selk = e_iota == ik
        ws.append(jnp.sum(jnp.where(selk, scores, 0.0), axis=0, keepdims=True))
        idxs.append(ik)
        masked = jnp.where(selk, NEG, masked)
        onehot = jnp.where(selk, 1.0, onehot)
    wsum = ws[0]
    for k in range(1, TOP_K):
        wsum = wsum + ws[k]

    r = lax.broadcasted_iota(I32, (tm, tm), 0)
    cidx = lax.broadcasted_iota(I32, (tm, tm), 1)
    before = (r < cidx).astype(BF16)
    cum = jnp.dot(onehot.astype(BF16), before, preferred_element_type=F32) + carry_s[...]
    carry_s[...] = carry_s[...] + jnp.sum(onehot, axis=1, keepdims=True)

    k_iota = lax.broadcasted_iota(I32, (TOP_K, tm), 0)
    e_out = jnp.zeros((TOP_K, tm), F32)
    w_out = jnp.zeros((TOP_K, tm), F32)
    p_out = jnp.zeros((TOP_K, tm), F32)
    for k in range(TOP_K):
        pk = jnp.sum(jnp.where(e_iota == idxs[k], cum, 0.0), axis=0, keepdims=True)
        e_out = jnp.where(k_iota == k, idxs[k], e_out)
        w_out = jnp.where(k_iota == k, ws[k] / wsum * ROUTED_SCALE, w_out)
        p_out = jnp.where(k_iota == k, pk, p_out)
    return e_out, w_out, p_out


def _post_kernel(oa_ref, ob_ref, ga_ref, gb_ref, x_ref, g1_ref, sc2_ref, sh2_ref, g2_ref,
                 wpa_ref, wpb_ref, wo_ref, n2_ref, wrt_ref, brt_ref, wsg_ref, wsu_ref, wsd_ref, c0_ref,
                 base_ref, hp_ref, eidx_ref, wts_ref, pos_ref, cnt_ref, carry_s, *, tm):
    i = pl.program_id(0)

    @pl.when(i == 0)
    def _():
        carry_s[...] = c0_ref[...]

    pa = jnp.dot(oa_ref[...], wpa_ref[...], preferred_element_type=F32)
    pb = jnp.dot(ob_ref[...], wpb_ref[...], preferred_element_type=F32)
    u = _sigmoid(ga_ref[...].astype(F32)) * pa + _sigmoid(gb_ref[...].astype(F32)) * pb
    x1 = x_ref[...] + _mod_val(g1_ref) * jnp.dot(u.astype(BF16), wo_ref[...], preferred_element_type=F32)
    y = x1 * lax.rsqrt(jnp.mean(x1 * x1, axis=-1, keepdims=True) + EPS) * n2_ref[...]
    h2 = y * (1.0 + _mod_val(sc2_ref)) + _mod_val(sh2_ref)
    _store_packed(hp_ref, h2)

    hb = h2.astype(BF16)
    sg = jnp.dot(hb, wsg_ref[...], preferred_element_type=F32)
    su = jnp.dot(hb, wsu_ref[...], preferred_element_type=F32)
    shared = jnp.dot((_silu(sg) * su).astype(BF16), wsd_ref[...], preferred_element_type=F32)
    base_ref[...] = x1 + _mod_val(g2_ref) * shared

    e_out, w_out, p_out = _route(h2, wrt_ref, brt_ref, carry_s, tm)
    eidx_ref[...] = e_out.astype(I32)
    wts_ref[...] = w_out
    pos_ref[...] = p_out.astype(I32)
    cnt_ref[...] = carry_s[...]


def _post(oa, ob, ga, gb, x2, g1, sc2, sh2, g2, wpa, wpb, wo, n2, wr, br, wsg, wsu, wsd, count0, seq, tm):
    t = x2.shape[0]
    tok = lambda: pl.BlockSpec((tm, D_MODEL), lambda i: (i, 0))
    mod = lambda: _mod_spec(g1, seq, tm)
    full = lambda a: pl.BlockSpec(a.shape, lambda i: (0,) * a.ndim)
    n2 = n2.reshape(1, -1)
    wrt_hi = wr.T.astype(BF16)
    wrt = jnp.stack([wrt_hi, (wr.T - wrt_hi.astype(F32)).astype(BF16)])
    brt = br.reshape(-1, 1)
    k_out = lambda: pl.BlockSpec((TOP_K, tm), lambda i: (0, i))
    return pl.pallas_call(
        functools.partial(_post_kernel, tm=tm),
        grid=(t // tm,),
        in_specs=[tok(), tok(), tok(), tok(), tok(), mod(), mod(), mod(), mod(),
                  full(wpa), full(wpb), full(wo), full(n2), full(wrt), full(brt),
                  full(wsg), full(wsu), full(wsd), full(count0)],
        out_specs=[tok(), pl.BlockSpec((ROW_PARTS * tm, LANES), lambda i: (i, 0)),
                   k_out(), k_out(), k_out(),
                   pl.BlockSpec((N_EXPERTS, 1), lambda i: (0, 0))],
        out_shape=[jax.ShapeDtypeStruct((t, D_MODEL), F32),
                   jax.ShapeDtypeStruct((ROW_PARTS * t, LANES), U32),
                   jax.ShapeDtypeStruct((TOP_K, t), I32),
                   jax.ShapeDtypeStruct((TOP_K, t), F32),
                   jax.ShapeDtypeStruct((TOP_K, t), I32),
                   jax.ShapeDtypeStruct((N_EXPERTS, 1), F32)],
        scratch_shapes=[pltpu.VMEM((N_EXPERTS, 1), F32)],
        compiler_params=_cparams(("arbitrary",)),
    )(oa, ob, ga, gb, x2, g1, sc2, sh2, g2, wpa, wpb, wo, n2, wrt, brt, wsg, wsu, wsd, count0)


def _dest_kernel(eidx_ref, pos_ref, pstart_ref, o_ref, *, tm):
    e_iota = lax.broadcasted_iota(I32, (N_EXPERTS, tm), 0)
    k_iota = lax.broadcasted_iota(I32, (TOP_K, tm), 0)
    eidx = eidx_ref[...]
    start = jnp.zeros((TOP_K, tm), F32)
    for k in range(TOP_K):
        sel = e_iota == eidx[k:k + 1, :]
        sk = jnp.sum(jnp.where(sel, pstart_ref[...], 0.0), axis=0, keepdims=True)
        start = jnp.where(k_iota == k, sk, start)
    o_ref[...] = (start.astype(I32) + pos_ref[...]) * ROW_PARTS


def _dest(eidx, pos, pstart_col):
    t = eidx.shape[1]
    tm = min(t, 1024)
    blk = lambda: pl.BlockSpec((TOP_K, tm), lambda i: (0, i))
    return pl.pallas_call(
        functools.partial(_dest_kernel, tm=tm),
        grid=(t // tm,),
        in_specs=[blk(), blk(), pl.BlockSpec((N_EXPERTS, 1), lambda i: (0, 0))],
        out_specs=blk(),
        out_shape=jax.ShapeDtypeStruct((TOP_K, t), I32),
        compiler_params=_cparams(("parallel",)),
    )(eidx, pos, pstart_col)


def _zero_tail_kernel(last_ref, o_ref):
    del last_ref
    o_ref[...] = jnp.zeros(o_ref.shape, o_ref.dtype)


def _zero_tails(last_blk, nb):
    grid_spec = pltpu.PrefetchScalarGridSpec(
        num_scalar_prefetch=1,
        grid=(N_EXPERTS,),
        in_specs=[],
        out_specs=pl.BlockSpec((ROW_PARTS * ROW_BLOCK, LANES), lambda e, last: (last[e], 0)),
    )
    return pl.pallas_call(
        _zero_tail_kernel,
        grid_spec=grid_spec,
        out_shape=jax.ShapeDtypeStruct((ROW_PARTS * nb * ROW_BLOCK, LANES), U32),
        compiler_params=_cparams(("arbitrary",)),
    )(last_blk)


def _dispatch_kernel(dest_ref, hp_ref, xs_in_ref, xs_ref, sem, *, tm):
    del xs_in_ref

    def row_copy(t, d):
        return pltpu.make_async_copy(
            hp_ref.at[pl.ds(pl.multiple_of(t * ROW_PARTS, ROW_PARTS), ROW_PARTS), :],
            xs_ref.at[pl.ds(pl.multiple_of(d, ROW_PARTS), ROW_PARTS), :], sem)

    def issue(t, c):
        for k in range(TOP_K):
            row_copy(t, dest_ref[t * TOP_K + k]).start(priority=k % 2)
        return c

    lax.fori_loop(0, tm, issue, 0)
    for _ in range(TOP_K):
        pltpu.make_async_copy(hp_ref, xs_ref.at[pl.ds(0, ROW_PARTS * tm), :], sem).wait()


def _dispatch(hp, dest, xs, tm):
    t = dest.shape[0] // TOP_K
    return pl.pallas_call(
        functools.partial(_dispatch_kernel, tm=tm),
        grid=(t // tm,),
        in_specs=[pl.BlockSpec((TOP_K * tm,), lambda i: (i,), memory_space=pltpu.SMEM),
                  pl.BlockSpec((ROW_PARTS * tm, LANES), lambda i: (i, 0)),
                  pl.BlockSpec(memory_space=pl.ANY)],
        out_specs=pl.BlockSpec(memory_space=pl.ANY),
        out_shape=jax.ShapeDtypeStruct(xs.shape, xs.dtype),
        scratch_shapes=[pltpu.SemaphoreType.DMA],
        input_output_aliases={2: 0},
        compiler_params=_cparams(("arbitrary",)),
    )(dest, hp, xs)


def _expert_kernel(be_ref, nu_ref, xs_ref, wg_ref, wu_ref, wd_ref, ys_ref, wg_s, wu_s, wd_s):
    i = pl.program_id(0)
    used = i < nu_ref[0]

    @pl.when(jnp.logical_and(used, jnp.logical_or(i == 0, be_ref[i] != be_ref[jnp.maximum(i - 1, 0)])))
    def _():
        wg_s[...] = wg_ref[0].astype(BF16)
        wu_s[...] = wu_ref[0].astype(BF16)
        wd_s[...] = wd_ref[0].astype(BF16)

    @pl.when(used)
    def _():
        x = _load_packed(xs_ref, ROW_BLOCK).astype(BF16)
        g = jnp.dot(x, wg_s[...], preferred_element_type=F32)
        u = jnp.dot(x, wu_s[...], preferred_element_type=F32)
        hmid = (_silu(g) * u).astype(BF16)
        _store_packed(ys_ref, jnp.dot(hmid, wd_s[...], preferred_element_type=F32))

    @pl.when(jnp.logical_not(used))
    def _():
        ys_ref[...] = jnp.zeros(ys_ref.shape, ys_ref.dtype)


def _experts(xs, blk_e, n_used, w_eg, w_eu, w_ed):
    nb = xs.shape[0] // (ROW_PARTS * ROW_BLOCK)
    rows_spec = lambda: pl.BlockSpec((ROW_PARTS * ROW_BLOCK, LANES), lambda i, be, nu: (i, 0))
    used_rows_spec = pl.BlockSpec((ROW_PARTS * ROW_BLOCK, LANES),
                                  lambda i, be, nu: (jnp.maximum(jnp.minimum(i, nu[0] - 1), 0), 0))
    w_up = lambda: pl.BlockSpec((1, D_MODEL, D_EXPERT), lambda i, be, nu: (be[i], 0, 0))
    grid_spec = pltpu.PrefetchScalarGridSpec(
        num_scalar_prefetch=2,
        grid=(nb,),
        in_specs=[used_rows_spec, w_up(), w_up(),
                  pl.BlockSpec((1, D_EXPERT, D_MODEL), lambda i, be, nu: (be[i], 0, 0))],
        out_specs=rows_spec(),
        scratch_shapes=[pltpu.VMEM((D_MODEL, D_EXPERT), BF16), pltpu.VMEM((D_MODEL, D_EXPERT), BF16),
                        pltpu.VMEM((D_EXPERT, D_MODEL), BF16)],
    )
    return pl.pallas_call(
        _expert_kernel,
        grid_spec=grid_spec,
        out_shape=jax.ShapeDtypeStruct(xs.shape, U32),
        compiler_params=_cparams(("arbitrary",)),
    )(blk_e, n_used, xs, w_eg, w_eu, w_ed)


def _combine_kernel(dest_ref, base_ref, wts_ref, g2_ref, ys_ref, o_ref, buf, sem, *, tm):
    def row_copy(d, k, t):
        return pltpu.make_async_copy(
            ys_ref.at[pl.ds(pl.multiple_of(d, ROW_PARTS), ROW_PARTS), :],
            buf.at[k, pl.ds(pl.multiple_of(t * ROW_PARTS, ROW_PARTS), ROW_PARTS), :], sem)

    def issue(t, c):
        for k in range(TOP_K):
            row_copy(dest_ref[t * TOP_K + k], k, t).start(priority=k % 2)
        return c

    lax.fori_loop(0, tm, issue, 0)
    for k in range(TOP_K):
        pltpu.make_async_copy(ys_ref.at[pl.ds(0, ROW_PARTS * tm), :], buf.at[k], sem).wait()

    w = wts_ref[...]
    y = jnp.zeros((tm, D_MODEL), F32)
    for k in range(TOP_K):
        y = y + w[:, k:k + 1] * _load_packed(buf.at[k], tm)
    o_ref[...] = base_ref[...] + _mod_val(g2_ref) * y


def _combine(base, wts_t, g2, ys, dest, seq, tm):
    t = base.shape[0]
    return pl.pallas_call(
        functools.partial(_combine_kernel, tm=tm),
        grid=(t // tm,),
        in_specs=[pl.BlockSpec((TOP_K * tm,), lambda i: (i,), memory_space=pltpu.SMEM),
                  pl.BlockSpec((tm, D_MODEL), lambda i: (i, 0)),
                  pl.BlockSpec((tm, TOP_K), lambda i: (i, 0)),
                  _mod_spec(g2, seq, tm),
                  pl.BlockSpec(memory_space=pl.ANY)],
        out_specs=pl.BlockSpec((tm, D_MODEL), lambda i: (i, 0)),
        out_shape=jax.ShapeDtypeStruct((t, D_MODEL), F32),
        scratch_shapes=[pltpu.VMEM((TOP_K, ROW_PARTS * tm, LANES), U32), pltpu.SemaphoreType.DMA],
        compiler_params=_cparams(("arbitrary",)),
    )(dest, base, wts_t, g2, ys)


def _mix(x, mods, past_k, past_v, s0, count0, p):
    batch, seq, _ = x.shape
    t = batch * seq
    x2 = x.reshape(t, D_MODEL)
    if seq % 256 == 0:
        tile = lambda cap: min(seq, cap)
    else:
        mods = [jnp.broadcast_to(m, (batch, seq, D_MODEL)).reshape(t, D_MODEL) for m in mods]
        tile = lambda cap: min(t, cap)
    shift1, scale1, gate1, shift2, scale2, gate2 = mods

    h = _norm_mod(x2, p["norm1_w"], scale1, shift1, seq, tile(512))
    qn = jnp.tile(p["q_norm_w"].reshape(1, -1), (1, D_MODEL // HEAD_DIM_A))
    kn = jnp.tile(p["k_norm_w"].reshape(1, -1), (1, D_MODEL // HEAD_DIM_A))
    w_in = p["w_in"]
    (q,) = _proj(h, w_in, 0, [BF16], qn, out_scale=(HEAD_DIM_A ** -0.5 * LOG2E,))
    k, kb = _proj(h, w_in, 1, [F32, BF16], kn, split_first=True, tm=512)
    v, vb = _proj(h, w_in, 2, [F32, BF16])
    (qh,) = _proj(h, w_in, 3, [BF16])
    (fh,) = _proj(h, w_in, 4, [F32])
    (ih,) = _proj(h, w_in, 5, [BF16])
    (gh,) = _proj(h, w_in, 6, [BF16])
    (ga,) = _proj(h, w_in, 7, [BF16])
    (gb,) = _proj(h, w_in, 8, [BF16])

    if past_k is None:
        o_a = _attention_prompt(q, kb, vb, p["lam"], p["subln_w"], batch, seq)
    else:
        past = past_k.shape[1]
        lk = past + seq
        kb = jnp.concatenate([past_k.reshape(batch, past, D_MODEL).astype(BF16),
                              kb.reshape(batch, seq, D_MODEL)], axis=1).reshape(batch * lk, D_MODEL)
        vb = jnp.concatenate([past_v.reshape(batch, past, D_MODEL).astype(BF16),
                              vb.reshape(batch, seq, D_MODEL)], axis=1).reshape(batch * lk, D_MODEL)
        o_a = _attention_cached(q, kb, vb, p["lam"], p["subln_w"], batch, seq, lk)
    o_b, s_new = _hgrn(qh, fh, ih, gh, p["hgrn_lb"], p["hgrn_norm_w"], s0, batch, seq)

    base, hp, eidx, wts, pos, counts = _post(
        o_a, o_b, ga, gb, x2, gate1, scale2, shift2, gate2,
        p["w_proj_a"], p["w_proj_b"], p["w_out"], p["norm2_w"], p["w_router"], p["b_router"],
        p["w_sh_gate"], p["w_sh_up"], p["w_sh_down"], count0, seq, tile(512))
    return dict(base=base, hp=hp, eidx=eidx, wts=wts, pos=pos, counts=counts, k=k, v=v, s=s_new,
                gate2=gate2, seq=seq, tm=tile(512), shape=x.shape)


def kernel(x_prompt, x_sample, cache_attn_k, cache_attn_v, state_hgrn, c_prompt, c_sample, w_ada, b_ada, norm1_w, norm2_w, w_in, q_norm_w, k_norm_w, lambda_q1, lambda_k1, lambda_q2, lambda_k2, subln_w, hgrn_lb, hgrn_norm_w, w_proj_a, w_proj_b, w_out, w_router, b_router, w_exp_gate, w_exp_up, w_exp_down, w_sh_gate, w_sh_up, w_sh_down):
    bp, lp, _ = x_prompt.shape
    bs, ls, _ = x_sample.shape
    l = 0
    lam = (jnp.exp(jnp.sum(lambda_q1[l] * lambda_k1[l])) - jnp.exp(jnp.sum(lambda_q2[l] * lambda_k2[l]))
           + LAMBDA_INIT).astype(F32)
    row_norm = lambda w, s: math.sqrt(HEAD_DIM_A) * jnp.max(jnp.abs(w.astype(F32))) * (s * 1.01)
    knorm = row_norm(k_norm_w[l], 1.0)
    qnorm = row_norm(q_norm_w[l], HEAD_DIM_A ** -0.5 * LOG2E)
    s_bound = qnorm * knorm
    use_bound = (2.0 * s_bound < ATTN_BOUND_MAX_GAP).astype(F32)
    lam = jnp.stack([lam, s_bound, use_bound, jnp.zeros((), F32)]).reshape(1, 4)
    p = dict(
        norm1_w=norm1_w[l], norm2_w=norm2_w[l], w_in=w_in[l].astype(BF16),
        q_norm_w=q_norm_w[l], k_norm_w=k_norm_w[l], lam=lam, subln_w=subln_w[l],
        hgrn_lb=hgrn_lb, hgrn_norm_w=hgrn_norm_w[l],
        w_proj_a=w_proj_a[l].astype(BF16), w_proj_b=w_proj_b[l].astype(BF16), w_out=w_out[l].astype(BF16),
        w_router=w_router[l], b_router=b_router[l],
        w_sh_gate=w_sh_gate[l].astype(BF16), w_sh_up=w_sh_up[l].astype(BF16), w_sh_down=w_sh_down[l].astype(BF16),
    )
    mod = _ada(jnp.concatenate([c_prompt, c_sample], axis=0), w_ada[l], b_ada[l])
    mod = mod.reshape(bp + bs, 6, 1, D_MODEL)
    mods_p = [mod[:bp, j] for j in range(6)]
    mods_s = [mod[bp:, j] for j in range(6)]

    zero_state = jnp.zeros((bp, N_HEADS, HEAD_W, HEAD_W), F32)
    zero_count = jnp.zeros((N_EXPERTS, 1), F32)
    gp = _mix(x_prompt, mods_p, None, None, zero_state, zero_count, p)
    gs = _mix(x_sample, mods_s, cache_attn_k[l], cache_attn_v[l], state_hgrn[l], gp["counts"], p)
    groups = (gp, gs)

    counts = gs["counts"].reshape(N_EXPERTS).astype(I32)
    pcounts = (counts + ROW_BLOCK - 1) // ROW_BLOCK * ROW_BLOCK
    pend = jnp.cumsum(pcounts)
    pstart = pend - pcounts
    n_assign = (bp * lp + bs * ls) * TOP_K
    nb = -(-(n_assign + N_EXPERTS * (ROW_BLOCK - 1)) // ROW_BLOCK)
    blk_start = jnp.arange(nb, dtype=I32)[:, None] * ROW_BLOCK
    blk_e = jnp.minimum(jnp.sum((pend[None, :] <= blk_start).astype(I32), axis=1), N_EXPERTS - 1)
    n_used = (pend[-1] // ROW_BLOCK).astype(I32).reshape(1)
    pstart_col = pstart.astype(F32).reshape(N_EXPERTS, 1)

    last_blk = jnp.maximum(pend // ROW_BLOCK - 1, 0).astype(I32)
    xs = _zero_tails(last_blk, nb)
    for g in groups:
        g["dest"] = _dest(g["eidx"], g["pos"], pstart_col).T.reshape(-1)
        xs = _dispatch(g["hp"], g["dest"], xs, g["tm"])
    ys = _experts(xs, blk_e, n_used, w_exp_gate[l], w_exp_up[l], w_exp_down[l])
    yp, ysm = [_combine(g["base"], g["wts"].T, g["gate2"], ys, g["dest"], g["seq"], g["tm"]).reshape(g["shape"])
               for g in groups]

    return (yp, ysm,
            gp["k"].reshape(1, bp, lp, N_HEADS, 2, HEAD_DIM_A), gp["v"].reshape(1, bp, lp, N_HEADS, HEAD_W),
            gp["s"][None],
            gs["k"].reshape(1, bs, ls, N_HEADS, 2, HEAD_DIM_A), gs["v"].reshape(1, bs, ls, N_HEADS, HEAD_W),
            gs["s"][None])
```

```python
import functools
import math

import jax
import jax.numpy as jnp
from jax import lax
from jax.experimental import pallas as pl
from jax.experimental.pallas import tpu as pltpu

F32 = jnp.float32
BF16 = jnp.bfloat16
U32 = jnp.uint32
I32 = jnp.int32

D_MODEL = 1024
N_HEADS = 8
HEAD_W = 128
HEAD_DIM_A = 64
CHUNK = 64
N_EXPERTS = 256
TOP_K = 8
N_GROUPS = 8
GROUP_W = N_EXPERTS // N_GROUPS
TOPK_GROUPS = 4
D_EXPERT = 256
ROUTED_SCALE = 2.5
EPS = 1e-6
LAMBDA_INIT = 0.8 - 0.6 * math.exp(-0.3 * 0)
LANES = 128
NEG = -1e30
LOG2E = math.log2(math.e)

ATTN_BOUND_MAX_GAP = 100.0
GLA_CHUNK = 32
ROW_BLOCK = 512
VMEM_LIMIT = 48 * 1024 * 1024


def _sigmoid(x):
    return 0.5 * jnp.tanh(0.5 * x) + 0.5


def _silu(x):
    return x * _sigmoid(x)


def _cparams(sem):
    return pltpu.CompilerParams(dimension_semantics=sem, vmem_limit_bytes=VMEM_LIMIT)


ROW_PARTS = 4


def _store_packed(ref, x, row0=0):
    m, half = x.shape[0], x.shape[1] // 2
    lo = pltpu.bitcast(x[:, :half].astype(BF16).astype(F32), U32)
    hi = pltpu.bitcast(x[:, half:].astype(BF16).astype(F32), U32)
    w = (hi & jnp.uint32(0xFFFF0000)) | (lo >> 16)
    for j in range(ROW_PARTS):
        ref[pl.ds(ROW_PARTS * row0 + j, m, stride=ROW_PARTS), :] = w[:, j * LANES:(j + 1) * LANES]


def _load_packed(ref, m, row0=0):
    parts = [ref[pl.ds(ROW_PARTS * row0 + j, m, stride=ROW_PARTS), :] for j in range(ROW_PARTS)]
    lo = [pltpu.bitcast(w << 16, F32) for w in parts]
    hi = [pltpu.bitcast(w & jnp.uint32(0xFFFF0000), F32) for w in parts]
    return jnp.concatenate(lo + hi, axis=1)


def _mod_spec(a, seq, tm):
    if a.ndim == 3:
        per_b = seq // tm
        return pl.BlockSpec((1, 1, D_MODEL), lambda i: (i // per_b, 0, 0))
    return pl.BlockSpec((tm, D_MODEL), lambda i: (i, 0))


def _mod_val(ref):
    v = ref[...]
    return v.reshape(v.shape[-2], v.shape[-1])


def _ada_kernel(c_ref, w_ref, b_ref, o_ref):
    c = _silu(c_ref[...])
    o_ref[...] = jnp.dot(c, w_ref[...], precision=lax.Precision.HIGHEST,
                         preferred_element_type=F32) + b_ref[...]


def _ada(c, w_ada, b_ada):
    n = c.shape[0]
    nj = w_ada.shape[1] // D_MODEL
    return pl.pallas_call(
        _ada_kernel,
        grid=(nj,),
        in_specs=[pl.BlockSpec((n, D_MODEL), lambda j: (0, 0)),
                  pl.BlockSpec((D_MODEL, D_MODEL), lambda j: (0, j)),
                  pl.BlockSpec((1, D_MODEL), lambda j: (0, j))],
        out_specs=pl.BlockSpec((n, D_MODEL), lambda j: (0, j)),
        out_shape=jax.ShapeDtypeStruct((n, w_ada.shape[1]), F32),
        compiler_params=_cparams(("arbitrary",)),
    )(c, w_ada, b_ada.reshape(1, -1))


def _norm_mod_kernel(x_ref, w_ref, sc_ref, sh_ref, o_ref):
    x = x_ref[...]
    y = x * lax.rsqrt(jnp.mean(x * x, axis=-1, keepdims=True) + EPS) * w_ref[...]
    o_ref[...] = (y * (1.0 + _mod_val(sc_ref)) + _mod_val(sh_ref)).astype(o_ref.dtype)


def _norm_mod(x2, w, scale, shift, seq, tm):
    t = x2.shape[0]
    return pl.pallas_call(
        _norm_mod_kernel,
        grid=(t // tm,),
        in_specs=[pl.BlockSpec((tm, D_MODEL), lambda i: (i, 0)),
                  pl.BlockSpec((1, D_MODEL), lambda i: (0, 0)),
                  _mod_spec(scale, seq, tm), _mod_spec(shift, seq, tm)],
        out_specs=pl.BlockSpec((tm, D_MODEL), lambda i: (i, 0)),
        out_shape=jax.ShapeDtypeStruct((t, D_MODEL), BF16),
        compiler_params=_cparams(("parallel",)),
    )(x2, w.reshape(1, -1), scale, shift)


def _head_norm(acc, w):
    r = lax.broadcasted_iota(I32, (HEAD_W, HEAD_W), 0) // HEAD_DIM_A
    c = lax.broadcasted_iota(I32, (HEAD_W, HEAD_W), 1) // HEAD_DIM_A
    ones = (r == c).astype(BF16)
    outs = []
    for h in range(N_HEADS):
        a = acc[:, h * HEAD_W:(h + 1) * HEAD_W]
        sq = a * a
        hi = sq.astype(BF16)
        lo = (sq - hi.astype(F32)).astype(BF16)
        ss = (jnp.dot(hi, ones, preferred_element_type=F32)
              + jnp.dot(lo, ones, preferred_element_type=F32))
        outs.append(a * lax.rsqrt(ss * (1.0 / HEAD_DIM_A) + EPS) * w[:, h * HEAD_W:(h + 1) * HEAD_W])
    return jnp.concatenate(outs, axis=1)


N_MAPS = D_MODEL // HEAD_DIM_A


def _proj_kernel(h_ref, w_ref, nw_ref, *o_refs, head_norm, out_scale, split_first):
    acc = jnp.dot(h_ref[...], w_ref[...], preferred_element_type=F32)
    if head_norm:
        acc = _head_norm(acc, nw_ref[...])
    for n, (o_ref, s) in enumerate(zip(o_refs, out_scale)):
        val = (acc if s == 1.0 else acc * s).astype(o_ref.dtype)
        if split_first and n == 0:
            for g in range(N_MAPS):
                o_ref[pl.ds(g, val.shape[0], stride=N_MAPS), :] = val[:, g * HEAD_DIM_A:(g + 1) * HEAD_DIM_A]
        else:
            o_ref[...] = val


def _proj(h, w_in_bf, group, out_dtypes, norm_w=None, out_scale=None, split_first=False, tm=1024):
    t = h.shape[0]
    tm = min(tm, t)
    nw = jnp.ones((1, D_MODEL), F32) if norm_w is None else norm_w
    out_scale = tuple(out_scale or (1.0,) * len(out_dtypes))
    specs = [pl.BlockSpec((tm, D_MODEL), lambda i: (i, 0)) for _ in out_dtypes]
    shapes = [jax.ShapeDtypeStruct((t, D_MODEL), dt) for dt in out_dtypes]
    if split_first:
        specs[0] = pl.BlockSpec((N_MAPS * tm, HEAD_DIM_A), lambda i: (i, 0))
        shapes[0] = jax.ShapeDtypeStruct((N_MAPS * t, HEAD_DIM_A), out_dtypes[0])
    outs = pl.pallas_call(
        functools.partial(_proj_kernel, head_norm=norm_w is not None, out_scale=out_scale,
                          split_first=split_first),
        grid=(t // tm,),
        in_specs=[pl.BlockSpec((tm, D_MODEL), lambda i: (i, 0)),
                  pl.BlockSpec((D_MODEL, D_MODEL), lambda i: (0, group)),
                  pl.BlockSpec((1, D_MODEL), lambda i: (0, 0))],
        out_specs=specs,
        out_shape=shapes,
        compiler_params=_cparams(("parallel",)),
    )(h, w_in_bf, nw)
    return outs


def _split_maps(q):
    lane = lax.broadcasted_iota(I32, q.shape, 1)
    zero = jnp.zeros_like(q)
    return jnp.where(lane < HEAD_DIM_A, q, zero), jnp.where(lane >= HEAD_DIM_A, q, zero)


def _finish_head(acc1, l1, acc2, l2, lam, sw):
    o = acc1 * (1.0 / l1) - acc2 * (lam / l2)
    o = o * lax.rsqrt(jnp.mean(o * o, axis=-1, keepdims=True) + EPS) * sw
    return o * (1.0 - LAMBDA_INIT)


def _attn_kernel(q_ref, k_ref, v_ref, lam_ref, sw_ref, o_ref, mx_s, mrep_s, ls_s, acc_s,
                 *, tq, tk, hp):
    qi = pl.program_id(2)
    nf = tk // LANES
    qs = []
    for hh in range(hp):
        qs.extend(_split_maps(q_ref[:, hh * HEAD_W:(hh + 1) * HEAD_W]))
    ns = 2 * hp

    r_chunk = lax.broadcasted_iota(I32, (tq, tk), 0) // CHUNK
    c_chunk = lax.broadcasted_iota(I32, (tq, tk), 1) // CHUNK
    diag_ok = c_chunk <= r_chunk

    def scores(kt, s, masked):
        k = k_ref[pl.ds(pl.multiple_of(kt * tk, tk), tk), (s // 2) * HEAD_W:(s // 2 + 1) * HEAD_W]
        sc = lax.dot_general(qs[s], k, (((1,), (1,)), ((), ())), preferred_element_type=F32)
        return jnp.where(diag_ok, sc, NEG) if masked else sc

    def fold(x, op):
        r = x[:, :LANES]
        for j in range(1, nf):
            r = op(r, x[:, j * LANES:(j + 1) * LANES])
        return r

    def pass1(kt, first):
        for s in range(ns):
            m = fold(scores(kt, s, first), jnp.maximum)
            mx_s[s] = m if first else jnp.maximum(mx_s[s], m)

    def pass2(kt, first):
        for s in range(ns):
            m = mrep_s[s]
            p = jnp.exp2(scores(kt, s, first) - jnp.concatenate([m] * nf, axis=1))
            v = v_ref[pl.ds(pl.multiple_of(kt * tk, tk), tk), (s // 2) * HEAD_W:(s // 2 + 1) * HEAD_W]
            pv = jnp.dot(p.astype(BF16), v, preferred_element_type=F32)
            ls_s[s] = fold(p, jnp.add) if first else ls_s[s] + fold(p, jnp.add)
            acc_s[s] = pv if first else acc_s[s] + pv

    def loop(fn):
        def body(kt, c):
            fn(kt, False)
            return c

        fn(qi, True)
        lax.fori_loop(0, qi, body, 0)

    use_bound = lam_ref[0, 2] > 0.5

    @pl.when(use_bound)
    def _():
        mrep_s[...] = jnp.full(mrep_s.shape, lam_ref[0, 1], F32)

    @pl.when(jnp.logical_not(use_bound))
    def _():
        loop(pass1)
        for s in range(ns):
            mrep_s[s] = jnp.broadcast_to(jnp.max(mx_s[s], axis=-1, keepdims=True), (tq, LANES))

    loop(pass2)

    lam = lam_ref[0, 0]
    ones = jnp.ones((LANES, LANES), BF16)

    def lane_sum(x):
        hi = x.astype(BF16)
        lo = (x - hi.astype(F32)).astype(BF16)
        return (jnp.dot(hi, ones, preferred_element_type=F32)
                + jnp.dot(lo, ones, preferred_element_type=F32))

    for hh in range(hp):
        o = _finish_head(acc_s[2 * hh], lane_sum(ls_s[2 * hh]), acc_s[2 * hh + 1], lane_sum(ls_s[2 * hh + 1]),
                         lam, sw_ref[...])
        o_ref[:, hh * HEAD_W:(hh + 1) * HEAD_W] = o.astype(o_ref.dtype)


def _attention_prompt(q, kb, vb, lam, subln_w, batch, seq):
    tq = tk = min(seq, 256)
    hp = 8
    nq = seq // tq
    kern = functools.partial(_attn_kernel, tq=tq, tk=tk, hp=hp)
    w = hp * HEAD_W
    return pl.pallas_call(
        kern,
        grid=(batch, N_HEADS // hp, nq),
        in_specs=[pl.BlockSpec((tq, w), lambda b, h, i: (b * nq + i, h)),
                  pl.BlockSpec((seq, w), lambda b, h, i: (b, h)),
                  pl.BlockSpec((seq, w), lambda b, h, i: (b, h)),
                  pl.BlockSpec((1, 4), lambda b, h, i: (0, 0), memory_space=pltpu.SMEM),
                  pl.BlockSpec((1, HEAD_W), lambda b, h, i: (0, 0))],
        out_specs=pl.BlockSpec((tq, w), lambda b, h, i: (b * nq + i, h)),
        out_shape=jax.ShapeDtypeStruct((batch * seq, D_MODEL), BF16),
        scratch_shapes=[pltpu.VMEM((2 * hp, tq, LANES), F32) for _ in range(4)],
        compiler_params=_cparams(("parallel", "parallel", "arbitrary")),
    )(q, kb, vb, lam, subln_w.reshape(1, -1))


def _attn_cached_kernel(q_ref, k_ref, v_ref, lam_ref, sw_ref, o_ref, *, lq, lk):
    qpos = (lk - lq) + lax.broadcasted_iota(I32, (lq, lk), 0)
    kpos = lax.broadcasted_iota(I32, (lq, lk), 1)
    allowed = (kpos // CHUNK) <= (qpos // CHUNK)
    lam = lam_ref[0, 0]
    for h in range(N_HEADS):
        sl = slice(h * HEAD_W, (h + 1) * HEAD_W)
        k = k_ref[:, sl]
        v = v_ref[:, sl]
        accs, ls = [], []
        for qm in _split_maps(q_ref[:, sl]):
            sc = lax.dot_general(qm, k, (((1,), (1,)), ((), ())), preferred_element_type=F32)
            sc = jnp.where(allowed, sc, NEG)
            p = jnp.exp2(sc - jnp.max(sc, axis=-1, keepdims=True))
            ls.append(jnp.sum(p, axis=-1, keepdims=True))
            accs.append(jnp.dot(p.astype(BF16), v, preferred_element_type=F32))
        o = _finish_head(accs[0], ls[0], accs[1], ls[1], lam, sw_ref[...])
        o_ref[:, sl] = o.astype(o_ref.dtype)


def _attention_cached(q, kb, vb, lam, subln_w, batch, lq, lk):
    return pl.pallas_call(
        functools.partial(_attn_cached_kernel, lq=lq, lk=lk),
        grid=(batch,),
        in_specs=[pl.BlockSpec((lq, D_MODEL), lambda b: (b, 0)),
                  pl.BlockSpec((lk, D_MODEL), lambda b: (b, 0)),
                  pl.BlockSpec((lk, D_MODEL), lambda b: (b, 0)),
                  pl.BlockSpec((1, 4), lambda b: (0, 0), memory_space=pltpu.SMEM),
                  pl.BlockSpec((1, HEAD_W), lambda b: (0, 0))],
        out_specs=pl.BlockSpec((lq, D_MODEL), lambda b: (b, 0)),
        out_shape=jax.ShapeDtypeStruct((batch * lq, D_MODEL), BF16),
        compiler_params=_cparams(("parallel",)),
    )(q, kb, vb, lam, subln_w.reshape(1, -1))


def _hgrn_kernel(qh_ref, fh_ref, ih_ref, gh_ref, lb_ref, nw_ref, s0_ref, o_ref, s_out_ref,
                 st_s, qin_s, qmid_s, kmid_s, kend_s, dec_s, *, ct, c):
    t = pl.program_id(1)
    nt = pl.num_programs(1)
    nc = ct // c

    @pl.when(t == 0)
    def _():
        for h in range(N_HEADS):
            st_s[h] = s0_ref[0, h].T

    a = lb_ref[...]
    amax = jnp.max(a, axis=0, keepdims=True)
    e = jnp.exp(a - amax)
    lb = e[0:1] / jnp.sum(e, axis=0, keepdims=True)

    row = lax.broadcasted_iota(I32, (c, c), 0)
    col = lax.broadcasted_iota(I32, (c, c), 1)
    causal = col <= row
    t_idx = lax.broadcasted_iota(I32, (c, D_MODEL), 0)

    def cumsum_rows(g):
        sh = 1
        while sh < c:
            g = g + jnp.where(t_idx >= sh, pltpu.roll(g, sh, axis=0), 0.0)
            sh *= 2
        return g

    for ci in range(nc):
        rows = slice(ci * c, (ci + 1) * c)
        f = lb + (1.0 - lb) * _sigmoid(fh_ref[rows, :])
        b = cumsum_rows(jnp.log2(f))
        b_last = b[c - 1:c, :]
        b_mid = b[c // 2 - 1:c // 2, :]
        q = _silu(qh_ref[rows, :].astype(F32)) * (HEAD_W ** -0.5)
        kk = 1.0 - f
        e_dn = jnp.exp2(b - b_mid)
        e_up = jnp.exp2(b_mid - b)
        qm = q * e_dn
        km = kk * e_up
        qmid_s[rows, :] = qm.astype(BF16)
        kmid_s[rows, :] = km.astype(BF16)
        qin_s[rows, :] = (qm * jnp.exp2(b_mid)).astype(BF16)
        kend_s[rows, :] = (km * jnp.exp2(b_last - b_mid)).astype(BF16)
        dec_s[ci:ci + 1, :] = jnp.exp2(b_last)

    nw = nw_ref[...]
    st = [st_s[h] for h in range(N_HEADS)]
    for ci in range(nc):
        rows = slice(ci * c, (ci + 1) * c)
        gate = _silu(gh_ref[rows, :].astype(F32))
        decay = dec_s[ci:ci + 1, :]
        for h in range(N_HEADS):
            sl = slice(h * HEAD_W, (h + 1) * HEAD_W)
            v = ih_ref[rows, sl]
            inter = lax.dot_general(qin_s[rows, sl], st[h].astype(BF16), (((1,), (1,)), ((), ())),
                                    preferred_element_type=F32)
            att = lax.dot_general(qmid_s[rows, sl], kmid_s[rows, sl], (((1,), (1,)), ((), ())),
                                  preferred_element_type=F32)
            att = jnp.where(causal, att, 0.0)
            o = inter + jnp.dot(att.astype(BF16), v, preferred_element_type=F32)
            upd = lax.dot_general(v, kend_s[rows, sl], (((0,), (0,)), ((), ())),
                                  preferred_element_type=F32)
            st[h] = decay[:, sl] * st[h] + upd
            o = o * lax.rsqrt(jnp.mean(o * o, axis=-1, keepdims=True) + EPS) * nw[:, sl]
            o_ref[rows, sl] = (o * gate[:, sl]).astype(o_ref.dtype)
    for h in range(N_HEADS):
        st_s[h] = st[h]

    @pl.when(t == nt - 1)
    def _():
        for h in range(N_HEADS):
            s_out_ref[0, h] = st_s[h].T


def _hgrn(qh, fh, ih, gh, hgrn_lb, norm_w, s0, batch, seq):
    ct = min(seq, 512)
    c = min(GLA_CHUNK, ct)
    nt = seq // ct
    kern = functools.partial(_hgrn_kernel, ct=ct, c=c)
    tok = pl.BlockSpec((ct, D_MODEL), lambda b, t: (b * nt + t, 0))
    st_spec = pl.BlockSpec((1, N_HEADS, HEAD_W, HEAD_W), lambda b, t: (b, 0, 0, 0))
    return pl.pallas_call(
        kern,
        grid=(batch, nt),
        in_specs=[tok, tok, tok, tok,
                  pl.BlockSpec((2, D_MODEL), lambda b, t: (0, 0)),
                  pl.BlockSpec((1, D_MODEL), lambda b, t: (0, 0)),
                  st_spec],
        out_specs=[tok, st_spec],
        out_shape=[jax.ShapeDtypeStruct((batch * seq, D_MODEL), BF16),
                   jax.ShapeDtypeStruct((batch, N_HEADS, HEAD_W, HEAD_W), F32)],
        scratch_shapes=[pltpu.VMEM((N_HEADS, HEAD_W, HEAD_W), F32)]
        + [pltpu.VMEM((ct, D_MODEL), BF16) for _ in range(4)]
        + [pltpu.VMEM((max(ct // c, 8), D_MODEL), F32)],
        compiler_params=_cparams(("parallel", "arbitrary")),
    )(qh, fh, ih, gh, hgrn_lb, jnp.tile(norm_w.reshape(1, -1), (1, N_HEADS)), s0)


def _route(h2, wrt_ref, brt_ref, carry_s, tm):
    nt_dims = (((1,), (1,)), ((), ()))
    w_hi = wrt_ref[0]
    w_lo = wrt_ref[1]
    h_hi = h2.astype(BF16)
    h_lo = (h2 - h_hi.astype(F32)).astype(BF16)
    logits = (lax.dot_general(w_hi, h_hi, nt_dims, preferred_element_type=F32)
              + lax.dot_general(w_hi, h_lo, nt_dims, preferred_element_type=F32)
              + lax.dot_general(w_lo, h_hi, nt_dims, preferred_element_type=F32))
    scores = _sigmoid(logits)
    biased = scores + brt_ref[...]
    big = float(2 * N_EXPERTS)

    x3 = biased.reshape(N_GROUPS, GROUP_W, tm)
    i3 = lax.broadcasted_iota(I32, x3.shape, 1).astype(F32)
    m1 = jnp.max(x3, axis=1, keepdims=True)
    i1 = jnp.min(jnp.where(x3 == m1, i3, big), axis=1, keepdims=True)
    m2 = jnp.max(jnp.where(i3 == i1, NEG, x3), axis=1, keepdims=True)
    gs = (m1 + m2).reshape(N_GROUPS, tm)

    g_iota = lax.broadcasted_iota(I32, (N_GROUPS, tm), 0).astype(F32)
    rem = gs
    gsel = jnp.zeros((N_GROUPS, tm), F32)
    for _ in range(TOPK_GROUPS):
        gm = jnp.max(rem, axis=0, keepdims=True)
        first = jnp.min(jnp.where(rem == gm, g_iota, big), axis=0, keepdims=True)
        sel = g_iota == first
        gsel = jnp.where(sel, 1.0, gsel)
        rem = jnp.where(sel, NEG, rem)
    keep = jnp.broadcast_to(gsel.reshape(N_GROUPS, 1, tm), x3.shape) > 0.5
    masked = jnp.where(keep, x3, NEG).reshape(N_EXPERTS, tm)

    e_iota = lax.broadcasted_iota(I32, (N_EXPERTS, tm), 0).astype(F32)
    onehot = jnp.zeros((N_EXPERTS, tm), F32)
    idxs, ws = [], []
    for _ in range(TOP_K):
        mk = jnp.max(masked, axis=0, keepdims=True)
        ik = jnp.min(jnp.where(masked == mk, e_iota, big), axis=0, keepdims=True)
        selk = e_iota == ik
        ws.append(jnp.sum(jnp.where(selk, scores, 0.0), axis=0, keepdims=True))
        idxs.append(ik)
        masked = jnp.where(selk, NEG, masked)
        onehot = jnp.where(selk, 1.0, onehot)
    wsum = ws[0]
    for k in range(1, TOP_K):
        wsum = wsum + ws[k]

    r = lax.broadcasted_iota(I32, (tm, tm), 0)
    cidx = lax.broadcasted_iota(I32, (tm, tm), 1)
    before = (r < cidx).astype(BF16)
    cum = jnp.dot(onehot.astype(BF16), before, preferred_element_type=F32) + carry_s[...]
    carry_s[...] = carry_s[...] + jnp.sum(onehot, axis=1, keepdims=True)

    k_iota = lax.broadcasted_iota(I32, (TOP_K, tm), 0)
    e_out = jnp.zeros((TOP_K, tm), F32)
    w_out = jnp.zeros((TOP_K, tm), F32)
    p_out = jnp.zeros((TOP_K, tm), F32)
    for k in range(TOP_K):
        pk = jnp.sum(jnp.where(e_iota == idxs[k], cum, 0.0), axis=0, keepdims=True)
        e_out = jnp.where(k_iota == k, idxs[k], e_out)
        w_out = jnp.where(k_iota == k, ws[k] / wsum * ROUTED_SCALE, w_out)
        p_out = jnp.where(k_iota == k, pk, p_out)
    return e_out, w_out, p_out


def _post_kernel(oa_ref, ob_ref, ga_ref, gb_ref, x_ref, g1_ref, sc2_ref, sh2_ref, g2_ref,
                 wpa_ref, wpb_ref, wo_ref, n2_ref, wrt_ref, brt_ref, wsg_ref, wsu_ref, wsd_ref, c0_ref,
                 base_ref, hp_ref, eidx_ref, wts_ref, pos_ref, cnt_ref, carry_s, *, tm):
    i = pl.program_id(0)

    @pl.when(i == 0)
    def _():
        carry_s[...] = c0_ref[...]

    pa = jnp.dot(oa_ref[...], wpa_ref[...], preferred_element_type=F32)
    pb = jnp.dot(ob_ref[...], wpb_ref[...], preferred_element_type=F32)
    u = _sigmoid(ga_ref[...].astype(F32)) * pa + _sigmoid(gb_ref[...].astype(F32)) * pb
    x1 = x_ref[...] + _mod_val(g1_ref) * jnp.dot(u.astype(BF16), wo_ref[...], preferred_element_type=F32)
    y = x1 * lax.rsqrt(jnp.mean(x1 * x1, axis=-1, keepdims=True) + EPS) * n2_ref[...]
    h2 = y * (1.0 + _mod_val(sc2_ref)) + _mod_val(sh2_ref)
    _store_packed(hp_ref, h2)

    hb = h2.astype(BF16)
    sg = jnp.dot(hb, wsg_ref[...], preferred_element_type=F32)
    su = jnp.dot(hb, wsu_ref[...], preferred_element_type=F32)
    shared = jnp.dot((_silu(sg) * su).astype(BF16), wsd_ref[...], preferred_element_type=F32)
    base_ref[...] = x1 + _mod_val(g2_ref) * shared

    e_out, w_out, p_out = _route(h2, wrt_ref, brt_ref, carry_s, tm)
    eidx_ref[...] = e_out.astype(I32)
    wts_ref[...] = w_out
    pos_ref[...] = p_out.astype(I32)
    cnt_ref[...] = carry_s[...]


def _post(oa, ob, ga, gb, x2, g1, sc2, sh2, g2, wpa, wpb, wo, n2, wr, br, wsg, wsu, wsd, count0, seq, tm):
    t = x2.shape[0]
    tok = lambda: pl.BlockSpec((tm, D_MODEL), lambda i: (i, 0))
    mod = lambda: _mod_spec(g1, seq, tm)
    full = lambda a: pl.BlockSpec(a.shape, lambda i: (0,) * a.ndim)
    n2 = n2.reshape(1, -1)
    wrt_hi = wr.T.astype(BF16)
    wrt = jnp.stack([wrt_hi, (wr.T - wrt_hi.astype(F32)).astype(BF16)])
    brt = br.reshape(-1, 1)
    k_out = lambda: pl.BlockSpec((TOP_K, tm), lambda i: (0, i))
    return pl.pallas_call(
        functools.partial(_post_kernel, tm=tm),
        grid=(t // tm,),
        in_specs=[tok(), tok(), tok(), tok(), tok(), mod(), mod(), mod(), mod(),
                  full(wpa), full(wpb), full(wo), full(n2), full(wrt), full(brt),
                  full(wsg), full(wsu), full(wsd), full(count0)],
        out_specs=[tok(), pl.BlockSpec((ROW_PARTS * tm, LANES), lambda i: (i, 0)),
                   k_out(), k_out(), k_out(),
                   pl.BlockSpec((N_EXPERTS, 1), lambda i: (0, 0))],
        out_shape=[jax.ShapeDtypeStruct((t, D_MODEL), F32),
                   jax.ShapeDtypeStruct((ROW_PARTS * t, LANES), U32),
                   jax.ShapeDtypeStruct((TOP_K, t), I32),
                   jax.ShapeDtypeStruct((TOP_K, t), F32),
                   jax.ShapeDtypeStruct((TOP_K, t), I32),
                   jax.ShapeDtypeStruct((N_EXPERTS, 1), F32)],
        scratch_shapes=[pltpu.VMEM((N_EXPERTS, 1), F32)],
        compiler_params=_cparams(("arbitrary",)),
    )(oa, ob, ga, gb, x2, g1, sc2, sh2, g2, wpa, wpb, wo, n2, wrt, brt, wsg, wsu, wsd, count0)


def _dest_kernel(eidx_ref, pos_ref, pstart_ref, o_ref, *, tm):
    e_iota = lax.broadcasted_iota(I32, (N_EXPERTS, tm), 0)
    k_iota = lax.broadcasted_iota(I32, (TOP_K, tm), 0)
    eidx = eidx_ref[...]
    start = jnp.zeros((TOP_K, tm), F32)
    for k in range(TOP_K):
        sel = e_iota == eidx[k:k + 1, :]
        sk = jnp.sum(jnp.where(sel, pstart_ref[...], 0.0), axis=0, keepdims=True)
        start = jnp.where(k_iota == k, sk, start)
    o_ref[...] = (start.astype(I32) + pos_ref[...]) * ROW_PARTS


def _dest(eidx, pos, pstart_col):
    t = eidx.shape[1]
    tm = min(t, 1024)
    blk = lambda: pl.BlockSpec((TOP_K, tm), lambda i: (0, i))
    return pl.pallas_call(
        functools.partial(_dest_kernel, tm=tm),
        grid=(t // tm,),
        in_specs=[blk(), blk(), pl.BlockSpec((N_EXPERTS, 1), lambda i: (0, 0))],
        out_specs=blk(),
        out_shape=jax.ShapeDtypeStruct((TOP_K, t), I32),
        compiler_params=_cparams(("parallel",)),
    )(eidx, pos, pstart_col)


def _zero_tail_kernel(last_ref, o_ref):
    del last_ref
    o_ref[...] = jnp.zeros(o_ref.shape, o_ref.dtype)


def _zero_tails(last_blk, nb):
    grid_spec = pltpu.PrefetchScalarGridSpec(
        num_scalar_prefetch=1,
        grid=(N_EXPERTS,),
        in_specs=[],
        out_specs=pl.BlockSpec((ROW_PARTS * ROW_BLOCK, LANES), lambda e, last: (last[e], 0)),
    )
    return pl.pallas_call(
        _zero_tail_kernel,
        grid_spec=grid_spec,
        out_shape=jax.ShapeDtypeStruct((ROW_PARTS * nb * ROW_BLOCK, LANES), U32),
        compiler_params=_cparams(("arbitrary",)),
    )(last_blk)


def _dispatch_kernel(dest_ref, hp_ref, xs_in_ref, xs_ref, sem, *, tm):
    del xs_in_ref

    def row_copy(t, d):
        return pltpu.make_async_copy(
            hp_ref.at[pl.ds(pl.multiple_of(t * ROW_PARTS, ROW_PARTS), ROW_PARTS), :],
            xs_ref.at[pl.ds(pl.multiple_of(d, ROW_PARTS), ROW_PARTS), :], sem)

    def issue(t, c):
        for k in range(TOP_K):
            row_copy(t, dest_ref[t * TOP_K + k]).start(priority=k % 2)
        return c

    lax.fori_loop(0, tm, issue, 0)
    for _ in range(TOP_K):
        pltpu.make_async_copy(hp_ref, xs_ref.at[pl.ds(0, ROW_PARTS * tm), :], sem).wait()


def _dispatch(hp, dest, xs, tm):
    t = dest.shape[0] // TOP_K
    return pl.pallas_call(
        functools.partial(_dispatch_kernel, tm=tm),
        grid=(t // tm,),
        in_specs=[pl.BlockSpec((TOP_K * tm,), lambda i: (i,), memory_space=pltpu.SMEM),
                  pl.BlockSpec((ROW_PARTS * tm, LANES), lambda i: (i, 0)),
                  pl.BlockSpec(memory_space=pl.ANY)],
        out_specs=pl.BlockSpec(memory_space=pl.ANY),
        out_shape=jax.ShapeDtypeStruct(xs.shape, xs.dtype),
        scratch_shapes=[pltpu.SemaphoreType.DMA],
        input_output_aliases={2: 0},
        compiler_params=_cparams(("arbitrary",)),
    )(dest, hp, xs)


def _expert_kernel(be_ref, nu_ref, xs_ref, wg_ref, wu_ref, wd_ref, ys_ref, wg_s, wu_s, wd_s):
    i = pl.program_id(0)
    used = i < nu_ref[0]

    @pl.when(jnp.logical_and(used, jnp.logical_or(i == 0, be_ref[i] != be_ref[jnp.maximum(i - 1, 0)])))
    def _():
        wg_s[...] = wg_ref[0].astype(BF16)
        wu_s[...] = wu_ref[0].astype(BF16)
        wd_s[...] = wd_ref[0].astype(BF16)

    @pl.when(used)
    def _():
        x = _load_packed(xs_ref, ROW_BLOCK).astype(BF16)
        g = jnp.dot(x, wg_s[...], preferred_element_type=F32)
        u = jnp.dot(x, wu_s[...], preferred_element_type=F32)
        hmid = (_silu(g) * u).astype(BF16)
        _store_packed(ys_ref, jnp.dot(hmid, wd_s[...], preferred_element_type=F32))

    @pl.when(jnp.logical_not(used))
    def _():
        ys_ref[...] = jnp.zeros(ys_ref.shape, ys_ref.dtype)


def _experts(xs, blk_e, n_used, w_eg, w_eu, w_ed):
    nb = xs.shape[0] // (ROW_PARTS * ROW_BLOCK)
    rows_spec = lambda: pl.BlockSpec((ROW_PARTS * ROW_BLOCK, LANES), lambda i, be, nu: (i, 0))
    used_rows_spec = pl.BlockSpec((ROW_PARTS * ROW_BLOCK, LANES),
                                  lambda i, be, nu: (jnp.maximum(jnp.minimum(i, nu[0] - 1), 0), 0))
    w_up = lambda: pl.BlockSpec((1, D_MODEL, D_EXPERT), lambda i, be, nu: (be[i], 0, 0))
    grid_spec = pltpu.PrefetchScalarGridSpec(
        num_scalar_prefetch=2,
        grid=(nb,),
        in_specs=[used_rows_spec, w_up(), w_up(),
                  pl.BlockSpec((1, D_EXPERT, D_MODEL), lambda i, be, nu: (be[i], 0, 0))],
        out_specs=rows_spec(),
        scratch_shapes=[pltpu.VMEM((D_MODEL, D_EXPERT), BF16), pltpu.VMEM((D_MODEL, D_EXPERT), BF16),
                        pltpu.VMEM((D_EXPERT, D_MODEL), BF16)],
    )
    return pl.pallas_call(
        _expert_kernel,
        grid_spec=grid_spec,
        out_shape=jax.ShapeDtypeStruct(xs.shape, U32),
        compiler_params=_cparams(("arbitrary",)),
    )(blk_e, n_used, xs, w_eg, w_eu, w_ed)


def _combine_kernel(dest_ref, base_ref, wts_ref, g2_ref, ys_ref, o_ref, buf, sem, *, tm):
    def row_copy(d, k, t):
        return pltpu.make_async_copy(
            ys_ref.at[pl.ds(pl.multiple_of(d, ROW_PARTS), ROW_PARTS), :],
            buf.at[k, pl.ds(pl.multiple_of(t * ROW_PARTS, ROW_PARTS), ROW_PARTS), :], sem)

    def issue(t, c):
        for k in range(TOP_K):
            row_copy(dest_ref[t * TOP_K + k], k, t).start(priority=k % 2)
        return c

    lax.fori_loop(0, tm, issue, 0)
    for k in range(TOP_K):
        pltpu.make_async_copy(ys_ref.at[pl.ds(0, ROW_PARTS * tm), :], buf.at[k], sem).wait()

    w = wts_ref[...]
    y = jnp.zeros((tm, D_MODEL), F32)
    for k in range(TOP_K):
        y = y + w[:, k:k + 1] * _load_packed(buf.at[k], tm)
    o_ref[...] = base_ref[...] + _mod_val(g2_ref) * y


def _combine(base, wts_t, g2, ys, dest, seq, tm):
    t = base.shape[0]
    return pl.pallas_call(
        functools.partial(_combine_kernel, tm=tm),
        grid=(t // tm,),
        in_specs=[pl.BlockSpec((TOP_K * tm,), lambda i: (i,), memory_space=pltpu.SMEM),
                  pl.BlockSpec((tm, D_MODEL), lambda i: (i, 0)),
                  pl.BlockSpec((tm, TOP_K), lambda i: (i, 0)),
                  _mod_spec(g2, seq, tm),
                  pl.BlockSpec(memory_space=pl.ANY)],
        out_specs=pl.BlockSpec((tm, D_MODEL), lambda i: (i, 0)),
        out_shape=jax.ShapeDtypeStruct((t, D_MODEL), F32),
        scratch_shapes=[pltpu.VMEM((TOP_K, ROW_PARTS * tm, LANES), U32), pltpu.SemaphoreType.DMA],
        compiler_params=_cparams(("arbitrary",)),
    )(dest, base, wts_t, g2, ys)


def _mix(x, mods, past_k, past_v, s0, count0, p):
    batch, seq, _ = x.shape
    t = batch * seq
    x2 = x.reshape(t, D_MODEL)
    if seq % 256 == 0:
        tile = lambda cap: min(seq, cap)
    else:
        mods = [jnp.broadcast_to(m, (batch, seq, D_MODEL)).reshape(t, D_MODEL) for m in mods]
        tile = lambda cap: min(t, cap)
    shift1, scale1, gate1, shift2, scale2, gate2 = mods

    h = _norm_mod(x2, p["norm1_w"], scale1, shift1, seq, tile(512))
    qn = jnp.tile(p["q_norm_w"].reshape(1, -1), (1, D_MODEL // HEAD_DIM_A))
    kn = jnp.tile(p["k_norm_w"].reshape(1, -1), (1, D_MODEL // HEAD_DIM_A))
    w_in = p["w_in"]
    (q,) = _proj(h, w_in, 0, [BF16], qn, out_scale=(HEAD_DIM_A ** -0.5 * LOG2E,))
    k, kb = _proj(h, w_in, 1, [F32, BF16], kn, split_first=True, tm=512)
    v, vb = _proj(h, w_in, 2, [F32, BF16])
    (qh,) = _proj(h, w_in, 3, [BF16])
    (fh,) = _proj(h, w_in, 4, [F32])
    (ih,) = _proj(h, w_in, 5, [BF16])
    (gh,) = _proj(h, w_in, 6, [BF16])
    (ga,) = _proj(h, w_in, 7, [BF16])
    (gb,) = _proj(h, w_in, 8, [BF16])

    if past_k is None:
        o_a = _attention_prompt(q, kb, vb, p["lam"], p["subln_w"], batch, seq)
    else:
        past = past_k.shape[1]
        lk = past + seq
        kb = jnp.concatenate([past_k.reshape(batch, past, D_MODEL).astype(BF16),
                              kb.reshape(batch, seq, D_MODEL)], axis=1).reshape(batch * lk, D_MODEL)
        vb = jnp.concatenate([past_v.reshape(batch, past, D_MODEL).astype(BF16),
                              vb.reshape(batch, seq, D_MODEL)], axis=1).reshape(batch * lk, D_MODEL)
        o_a = _attention_cached(q, kb, vb, p["lam"], p["subln_w"], batch, seq, lk)
    o_b, s_new = _hgrn(qh, fh, ih, gh, p["hgrn_lb"], p["hgrn_norm_w"], s0, batch, seq)

    base, hp, eidx, wts, pos, counts = _post(
        o_a, o_b, ga, gb, x2, gate1, scale2, shift2, gate2,
        p["w_proj_a"], p["w_proj_b"], p["w_out"], p["norm2_w"], p["w_router"], p["b_router"],
        p["w_sh_gate"], p["w_sh_up"], p["w_sh_down"], count0, seq, tile(512))
    return dict(base=base, hp=hp, eidx=eidx, wts=wts, pos=pos, counts=counts, k=k, v=v, s=s_new,
                gate2=gate2, seq=seq, tm=tile(1024), shape=x.shape)


def kernel(x_prompt, x_sample, cache_attn_k, cache_attn_v, state_hgrn, c_prompt, c_sample, w_ada, b_ada, norm1_w, norm2_w, w_in, q_norm_w, k_norm_w, lambda_q1, lambda_k1, lambda_q2, lambda_k2, subln_w, hgrn_lb, hgrn_norm_w, w_proj_a, w_proj_b, w_out, w_router, b_router, w_exp_gate, w_exp_up, w_exp_down, w_sh_gate, w_sh_up, w_sh_down):
    bp, lp, _ = x_prompt.shape
    bs, ls, _ = x_sample.shape
    l = 0
    lam = (jnp.exp(jnp.sum(lambda_q1[l] * lambda_k1[l])) - jnp.exp(jnp.sum(lambda_q2[l] * lambda_k2[l]))
           + LAMBDA_INIT).astype(F32)
    row_norm = lambda w, s: math.sqrt(HEAD_DIM_A) * jnp.max(jnp.abs(w.astype(F32))) * (s * 1.01)
    knorm = row_norm(k_norm_w[l], 1.0)
    qnorm = row_norm(q_norm_w[l], HEAD_DIM_A ** -0.5 * LOG2E)
    s_bound = qnorm * knorm
    use_bound = (2.0 * s_bound < ATTN_BOUND_MAX_GAP).astype(F32)
    lam = jnp.stack([lam, s_bound, use_bound, jnp.zeros((), F32)]).reshape(1, 4)
    p = dict(
        norm1_w=norm1_w[l], norm2_w=norm2_w[l], w_in=w_in[l].astype(BF16),
        q_norm_w=q_norm_w[l], k_norm_w=k_norm_w[l], lam=lam, subln_w=subln_w[l],
        hgrn_lb=hgrn_lb, hgrn_norm_w=hgrn_norm_w[l],
        w_proj_a=w_proj_a[l].astype(BF16), w_proj_b=w_proj_b[l].astype(BF16), w_out=w_out[l].astype(BF16),
        w_router=w_router[l], b_router=b_router[l],
        w_sh_gate=w_sh_gate[l].astype(BF16), w_sh_up=w_sh_up[l].astype(BF16), w_sh_down=w_sh_down[l].astype(BF16),
    )
    mod = _ada(jnp.concatenate([c_prompt, c_sample], axis=0), w_ada[l], b_ada[l])
    mod = mod.reshape(bp + bs, 6, 1, D_MODEL)
    mods_p = [mod[:bp, j] for j in range(6)]
    mods_s = [mod[bp:, j] for j in range(6)]

    zero_state = jnp.zeros((bp, N_HEADS, HEAD_W, HEAD_W), F32)
    zero_count = jnp.zeros((N_EXPERTS, 1), F32)
    gp = _mix(x_prompt, mods_p, None, None, zero_state, zero_count, p)
    gs = _mix(x_sample, mods_s, cache_attn_k[l], cache_attn_v[l], state_hgrn[l], gp["counts"], p)
    groups = (gp, gs)

    counts = gs["counts"].reshape(N_EXPERTS).astype(I32)
    pcounts = (counts + ROW_BLOCK - 1) // ROW_BLOCK * ROW_BLOCK
    pend = jnp.cumsum(pcounts)
    pstart = pend - pcounts
    n_assign = (bp * lp + bs * ls) * TOP_K
    nb = -(-(n_assign + N_EXPERTS * (ROW_BLOCK - 1)) // ROW_BLOCK)
    blk_start = jnp.arange(nb, dtype=I32)[:, None] * ROW_BLOCK
    blk_e = jnp.minimum(jnp.sum((pend[None, :] <= blk_start).astype(I32), axis=1), N_EXPERTS - 1)
    n_used = (pend[-1] // ROW_BLOCK).astype(I32).reshape(1)
    pstart_col = pstart.astype(F32).reshape(N_EXPERTS, 1)

    last_blk = jnp.maximum(pend // ROW_BLOCK - 1, 0).astype(I32)
    xs = _zero_tails(last_blk, nb)
    for g in groups:
        g["dest"] = _dest(g["eidx"], g["pos"], pstart_col).T.reshape(-1)
        xs = _dispatch(g["hp"], g["dest"], xs, g["tm"])
    ys = _experts(xs, blk_e, n_used, w_exp_gate[l], w_exp_up[l], w_exp_down[l])
    yp, ysm = [_combine(g["base"], g["wts"].T, g["gate2"], ys, g["dest"], g["seq"], g["tm"]).reshape(g["shape"])
               for g in groups]

    return (yp, ysm,
            gp["k"].reshape(1, bp, lp, N_HEADS, 2, HEAD_DIM_A), gp["v"].reshape(1, bp, lp, N_HEADS, HEAD_W),
            gp["s"][None],
            gs["k"].reshape(1, bs, ls, N_HEADS, 2, HEAD_DIM_A), gs["v"].reshape(1, bs, ls, N_HEADS, HEAD_W),
            gs["s"][None])
```
